```python
import jax, jax.numpy as jnp
from jax import lax
import numpy as np

D_MODEL = 2048
BATCH = 2
SEQ = 16384
DEPTH = 2

D_MIX = D_MODEL
OUT_HEAD = 128
EPS = 1e-6

CONV_WIDTH = D_MIX // 4
CONV_K = 3

GLA_HEADS = 4
GLA_DV = D_MIX // 4 // GLA_HEADS
GLA_DK = GLA_DV // 2
GLA_LOWRANK = 16
GLA_TAU = 16.0
GLA_CHUNK = 64

NSA_HEADS = 8
NSA_KV_HEADS = 2
NSA_HD = D_MIX // 2 // NSA_HEADS
CMP_LEN = 32
CMP_STRIDE = 16
SEL_BLOCK = 64
SEL_TOPK = 16
SEL_LOCAL = 2
WINDOW = 512
Q_BLOCK = 128
BIG = 1e9

PEER_HEADS = 8
PEER_NKEYS = 128
PEER_EXPERTS = PEER_NKEYS * PEER_NKEYS
PEER_TOPK = 16
PEER_DKEY = 256
PEER_CHUNK = 128

IN_SIZES = (
    CONV_WIDTH, CONV_WIDTH, CONV_WIDTH,
    GLA_HEADS * GLA_DK, GLA_HEADS * GLA_DK, GLA_HEADS * GLA_DV,
    GLA_HEADS * GLA_DV, GLA_LOWRANK,
    NSA_HEADS * NSA_HD,
    NSA_KV_HEADS * NSA_HD, NSA_KV_HEADS * NSA_HD,
    NSA_KV_HEADS * NSA_HD, NSA_KV_HEADS * NSA_HD,
    NSA_KV_HEADS * NSA_HD, NSA_KV_HEADS * NSA_HD,
    NSA_HEADS * 3,
)
D_IN = sum(IN_SIZES)

kernel_name = 'hybrid_conv_gla_nsa_peer_trunk'


def rms_norm(x, w):
    xf = x.astype(jnp.float32)
    y = xf * lax.rsqrt(jnp.mean(xf * xf, axis=-1, keepdims=True) + EPS)
    return (y * w.astype(jnp.float32)).astype(x.dtype)


def head_rms(y, w):
    B, S, C = y.shape
    n = C // OUT_HEAD
    return rms_norm(y.reshape(B, S, n, OUT_HEAD), w.reshape(n, OUT_HEAD)).reshape(B, S, C)


def alibi_slopes(n):
    return jnp.asarray([2.0 ** (-8.0 * (h + 1) / n) for h in range(n)], dtype=jnp.float32)


def masked_softmax(s, mask):
    s = jnp.where(mask, s.astype(jnp.float32), -jnp.inf)
    m = jnp.max(s, axis=-1, keepdims=True)
    m = jnp.where(jnp.isfinite(m), m, 0.0)
    p = jnp.exp(s - m)
    return p / jnp.maximum(jnp.sum(p, axis=-1, keepdims=True), 1e-30)


def short_conv_mixer(xs, b, c, conv_w):
    f32 = jnp.float32
    z = c.astype(f32) * xs.astype(f32)
    S = z.shape[1]
    zp = jnp.pad(z, ((0, 0), (CONV_K - 1, 0), (0, 0)))
    w = conv_w.astype(f32)
    y = zp[:, CONV_K - 1:CONV_K - 1 + S] * w[:, 0]
    for j in range(1, CONV_K):
        y = y + zp[:, CONV_K - 1 - j:CONV_K - 1 - j + S] * w[:, j]
    return b.astype(f32) * y


def gla_mixer(q, k, v, a_lr, w_a2, b_a):
    f32 = jnp.float32
    B, S, _ = q.shape
    H, C = GLA_HEADS, GLA_CHUNK
    N = S // C
    q = q.astype(f32).reshape(B, N, C, H, GLA_DK) * GLA_DK ** -0.5
    k = k.astype(f32).reshape(B, N, C, H, GLA_DK)
    v = v.astype(f32).reshape(B, N, C, H, GLA_DV)
    log_a = jax.nn.log_sigmoid(a_lr.astype(f32) @ w_a2.astype(f32) + b_a.astype(f32)) / GLA_TAU
    bcum = jnp.cumsum(log_a.reshape(B, N, C, H, GLA_DK), axis=2)
    b_last = bcum[:, :, -1:]
    q_dec = q * jnp.exp(bcum)
    k_inv = k * jnp.exp(-bcum)
    k_end = k * jnp.exp(b_last - bcum)
    att = jnp.einsum('bnthd,bnshd->bnhts', q_dec, k_inv)
    att = jnp.where(jnp.tril(jnp.ones((C, C), dtype=bool)), att, 0.0)
    o_intra = jnp.einsum('bnhts,bnshv->bnthv', att, v)
    d_state = jnp.einsum('bnshd,bnshv->nbhdv', k_end, v)
    decay = jnp.exp(b_last[:, :, 0]).transpose(1, 0, 2, 3)

    def step(state, inp):
        d, ds = inp
        return d[..., None] * state + ds, state

    _, s_start = lax.scan(step, jnp.zeros((B, H, GLA_DK, GLA_DV), f32), (decay, d_state))
    o_inter = jnp.einsum('bnthd,nbhdv->bnthv', q_dec, s_start)
    return (o_intra + o_inter).reshape(B, S, H * GLA_DV)


def compress_blocks(t, pos, w1, w2):
    B, S, G, D = t.shape
    r = CMP_LEN // CMP_STRIDE
    n_piece = S // CMP_STRIDE
    n_cmp = n_piece - r + 1
    pieces = t.reshape(B, n_piece, CMP_STRIDE, G, D)
    blk = jnp.concatenate([pieces[:, j:j + n_cmp] for j in range(r)], axis=2)
    blk = (blk + pos.astype(t.dtype)[:, None, :]).transpose(0, 1, 3, 2, 4).reshape(B, n_cmp, G, CMP_LEN * D)
    return jax.nn.gelu(blk @ w1.astype(t.dtype)) @ w2.astype(t.dtype)


def nsa_mixer(q, kc, vc, ks, vs, kw, vw, gate_logits, q_norm, k_norm, cmp_pos, cmp_w1, cmp_w2):
    f32 = jnp.float32
    B, S, _ = q.shape
    G, R, D = NSA_KV_HEADS, NSA_HEADS // NSA_KV_HEADS, NSA_HD
    kv_shape = (B, S, G, D)
    q = rms_norm(q.astype(f32).reshape(B, S, G, R, D), q_norm)
    kc = kc.astype(f32).reshape(kv_shape)
    vc = vc.astype(f32).reshape(kv_shape)
    ks = rms_norm(ks.astype(f32).reshape(kv_shape), k_norm[1])
    vs = vs.astype(f32).reshape(kv_shape)
    kw = rms_norm(kw.astype(f32).reshape(kv_shape), k_norm[2])
    vw = vw.astype(f32).reshape(kv_shape)
    kcc = rms_norm(compress_blocks(kc, cmp_pos[0], cmp_w1[0], cmp_w2[0]), k_norm[0])
    vcc = compress_blocks(vc, cmp_pos[1], cmp_w1[1], cmp_w2[1])
    gates = jax.nn.sigmoid(gate_logits.astype(f32)).reshape(B, S, G, R, 3)

    n_cmp = kcc.shape[1]
    n_sel = S // SEL_BLOCK
    k_sel = min(SEL_TOPK, n_sel)
    cmp_start = jnp.arange(n_cmp) * CMP_STRIDE
    cmp_end = cmp_start + (CMP_LEN - 1)
    sel_start = jnp.arange(n_sel) * SEL_BLOCK
    overlap = jnp.clip(jnp.minimum(cmp_start[:, None] + CMP_LEN, sel_start[None, :] + SEL_BLOCK)
                       - jnp.maximum(cmp_start[:, None], sel_start[None, :]), 0, None).astype(f32) / CMP_LEN
    ks_blk = ks.reshape(B, n_sel, SEL_BLOCK, G, D).transpose(0, 3, 1, 2, 4)
    vs_blk = vs.reshape(B, n_sel, SEL_BLOCK, G, D).transpose(0, 3, 1, 2, 4)
    kw_pad = jnp.pad(kw, ((0, 0), (WINDOW, 0), (0, 0), (0, 0)))
    vw_pad = jnp.pad(vw, ((0, 0), (WINDOW, 0), (0, 0), (0, 0)))
    slopes = alibi_slopes(NSA_HEADS).reshape(G, R)[None, :, :, None, None]
    scale = D ** -0.5
    bi = jnp.arange(B)[:, None, None, None]
    gi = jnp.arange(G)[None, :, None, None]
    blk_id = jnp.arange(n_sel)
    n_keys_sel = k_sel * SEL_BLOCK

    def one_block(qb):
        q0 = qb * Q_BLOCK
        t = q0 + jnp.arange(Q_BLOCK)
        qblk = lax.dynamic_slice_in_dim(q, q0, Q_BLOCK, axis=1)
        gblk = lax.dynamic_slice_in_dim(gates, q0, Q_BLOCK, axis=1)
        dist_c = t[:, None] - cmp_end[None, :]
        s_c = jnp.einsum('btgrd,bngd->bgrtn', qblk, kcc) * scale - slopes * dist_c
        p_c = masked_softmax(s_c, dist_c >= 0)
        o_c = jnp.einsum('bgrtn,bngd->btgrd', p_c, vcc)
        imp = jnp.einsum('bgtn,nj->bgtj', p_c.sum(axis=2), overlap)
        lag = t[:, None] // SEL_BLOCK - blk_id[None, :]
        forced = (blk_id[None, :] == 0) | ((lag >= 0) & (lag < SEL_LOCAL))
        score = jnp.where(forced, BIG, imp)
        score = jnp.where(lag >= 0, score, -1.0)
        _, idx = lax.top_k(score, k_sel)
        k_g = ks_blk[bi, gi, idx].reshape(B, G, Q_BLOCK, n_keys_sel, D)
        v_g = vs_blk[bi, gi, idx].reshape(B, G, Q_BLOCK, n_keys_sel, D)
        pos_s = (idx[..., None] * SEL_BLOCK + jnp.arange(SEL_BLOCK)).reshape(B, G, Q_BLOCK, n_keys_sel)
        dist_s = (t[:, None] - pos_s)[:, :, None]
        s_s = jnp.einsum('btgrd,bgtmd->bgrtm', qblk, k_g) * scale - slopes * dist_s
        p_s = masked_softmax(s_s, dist_s >= 0)
        o_s = jnp.einsum('bgrtm,bgtmd->btgrd', p_s, v_g)
        kwin = lax.dynamic_slice_in_dim(kw_pad, q0, Q_BLOCK + WINDOW, axis=1)
        vwin = lax.dynamic_slice_in_dim(vw_pad, q0, Q_BLOCK + WINDOW, axis=1)
        pos_w = q0 - WINDOW + jnp.arange(Q_BLOCK + WINDOW)
        dist_w = t[:, None] - pos_w[None, :]
        mask_w = (dist_w >= 0) & (dist_w < WINDOW) & (pos_w[None, :] >= 0)
        s_w = jnp.einsum('btgrd,bmgd->bgrtm', qblk, kwin) * scale - slopes * dist_w
        p_w = masked_softmax(s_w, mask_w)
        o_w = jnp.einsum('bgrtm,bmgd->btgrd', p_w, vwin)
        return gblk[..., 0:1] * o_c + gblk[..., 1:2] * o_s + gblk[..., 2:3] * o_w

    out = lax.map(one_block, jnp.arange(S // Q_BLOCK))
    return jnp.moveaxis(out, 0, 1).reshape(B, S, NSA_HEADS * D)


def peer_ffn(x, w_q, sub_keys, u, v):
    f32 = jnp.float32
    B, S, D = x.shape
    K = PEER_TOPK
    qry = (x @ w_q).astype(f32).reshape(B, S, PEER_HEADS, 2, PEER_DKEY // 2)
    sc = jnp.einsum('bshpd,hpnd->bshpn', qry, sub_keys.astype(f32))
    s_top, i_top = lax.top_k(sc, K)
    cand = (s_top[..., 0, :, None] + s_top[..., 1, None, :]).reshape(B, S, PEER_HEADS, K * K)
    cand_id = (i_top[..., 0, :, None] * PEER_NKEYS + i_top[..., 1, None, :]).reshape(B, S, PEER_HEADS, K * K)
    best, pos = lax.top_k(cand, K)
    expert = jnp.take_along_axis(cand_id, pos, axis=-1)
    gate = jax.nn.softmax(best, axis=-1)

    def chunk(c):
        s0 = c * PEER_CHUNK
        xc = lax.dynamic_slice_in_dim(x, s0, PEER_CHUNK, axis=1).astype(f32)
        ec = lax.dynamic_slice_in_dim(expert, s0, PEER_CHUNK, axis=1)
        gc = lax.dynamic_slice_in_dim(gate, s0, PEER_CHUNK, axis=1)
        act = jax.nn.gelu(jnp.einsum('btd,bthkd->bthk', xc, u[ec].astype(f32)))
        return jnp.einsum('bthk,bthkd->btd', gc * act, v[ec].astype(f32))

    y = lax.map(chunk, jnp.arange(S // PEER_CHUNK))
    return jnp.moveaxis(y, 0, 1).reshape(B, S, D)


def setup_inputs(seed: int = 0) -> dict:
    key = jax.random.key(seed)
    ks = jax.random.split(key, 18)
    f32 = jnp.float32
    nrm = lambda k, shape, sc: jax.random.normal(k, shape, f32) * sc
    gain = lambda k, shape: 1.0 + 0.05 * jax.random.normal(k, shape, f32)
    return {
        'x': nrm(ks[0], (BATCH, SEQ, D_MODEL), 1.0),
        'norm_mix': gain(ks[1], (DEPTH, D_MODEL)),
        'w_in': nrm(ks[2], (DEPTH, D_MODEL, D_IN), D_MODEL ** -0.5),
        'conv_w': nrm(ks[3], (DEPTH, CONV_WIDTH, CONV_K), CONV_K ** -0.5),
        'gla_w_a2': nrm(ks[4], (DEPTH, GLA_LOWRANK, GLA_HEADS * GLA_DK), GLA_LOWRANK ** -0.5),
        'gla_b_a': nrm(ks[5], (DEPTH, GLA_HEADS * GLA_DK), 0.1),
        'nsa_q_norm': gain(ks[6], (DEPTH, NSA_HD)),
        'nsa_k_norm': gain(ks[7], (DEPTH, 3, NSA_HD)),
        'nsa_cmp_pos': nrm(ks[8], (DEPTH, 2, CMP_LEN, NSA_HD), 0.02),
        'nsa_cmp_w1': nrm(ks[9], (DEPTH, 2, CMP_LEN * NSA_HD, NSA_HD), (CMP_LEN * NSA_HD) ** -0.5),
        'nsa_cmp_w2': nrm(ks[10], (DEPTH, 2, NSA_HD, NSA_HD), NSA_HD ** -0.5),
        'out_norm': gain(ks[11], (DEPTH, D_MIX)),
        'w_out': nrm(ks[12], (DEPTH, D_MIX, D_MODEL), D_MIX ** -0.5),
        'norm_ffn': gain(ks[13], (DEPTH, D_MODEL)),
        'peer_w_q': nrm(ks[14], (DEPTH, D_MODEL, PEER_HEADS * PEER_DKEY), D_MODEL ** -0.5),
        'peer_keys': nrm(ks[15], (DEPTH, PEER_HEADS, 2, PEER_NKEYS, PEER_DKEY // 2), (PEER_DKEY // 2) ** -0.5),
        'peer_u': nrm(ks[16], (DEPTH, PEER_EXPERTS, D_MODEL), D_MODEL ** -0.5),
        'peer_v': nrm(ks[17], (DEPTH, PEER_EXPERTS, D_MODEL), PEER_HEADS ** -0.5),
    }


def reference(x, norm_mix, w_in, conv_w, gla_w_a2, gla_b_a, nsa_q_norm, nsa_k_norm,
              nsa_cmp_pos, nsa_cmp_w1, nsa_cmp_w2, out_norm, w_out, norm_ffn,
              peer_w_q, peer_keys, peer_u, peer_v):
    split_at = [int(c) for c in np.cumsum(IN_SIZES)[:-1]]
    norm_split = [CONV_WIDTH, CONV_WIDTH + GLA_HEADS * GLA_DV]
    h = x
    for l in range(DEPTH):
        xn = rms_norm(h, norm_mix[l])
        (c_x, c_b, c_c, g_q, g_k, g_v, g_g, g_a,
         n_q, n_kc, n_vc, n_ks, n_vs, n_kw, n_vw, n_gate) = jnp.split(xn @ w_in[l], split_at, axis=-1)
        y_conv = short_conv_mixer(c_x, c_b, c_c, conv_w[l])
        y_gla = gla_mixer(g_q, g_k, g_v, g_a, gla_w_a2[l], gla_b_a[l])
        y_nsa = nsa_mixer(n_q, n_kc, n_vc, n_ks, n_vs, n_kw, n_vw, n_gate,
                          nsa_q_norm[l], nsa_k_norm[l], nsa_cmp_pos[l], nsa_cmp_w1[l], nsa_cmp_w2[l])
        on_conv, on_gla, on_nsa = jnp.split(out_norm[l], norm_split)
        mix = jnp.concatenate([
            head_rms(y_conv, on_conv),
            head_rms(y_gla, on_gla) * jax.nn.silu(g_g.astype(jnp.float32)),
            head_rms(y_nsa, on_nsa)], axis=-1)
        h = h + mix.astype(h.dtype) @ w_out[l]
        h = h + peer_ffn(rms_norm(h, norm_ffn[l]), peer_w_q[l], peer_keys[l],
                         peer_u[l], peer_v[l]).astype(h.dtype)
    return h
```

```python
import functools
import math

import numpy as np
import jax
import jax.numpy as jnp
from jax import lax
from jax.experimental import pallas as pl
from jax.experimental.pallas import tpu as pltpu

f32 = jnp.float32
bf16 = jnp.bfloat16

D_MODEL = 2048
EPS = 1e-6
HEAD = 128
CONV_W = 512
CONV_K = 3
GLA_H, GLA_DK, GLA_DV = 4, 64, 128
GLA_LR = 16
GLA_TAU = 16.0
GLA_CHUNK = 64
NSA_G, NSA_R, NSA_D = 2, 4, 128
CMP_LEN, CMP_STRIDE = 32, 16
SEL_BLOCK, SEL_TOPK, SEL_LOCAL = 64, 16, 2
WINDOW = 512
BIG = 1e9
PEER_H, PEER_NK, PEER_TOPK = 8, 128, 16
PEER_E = PEER_NK * PEER_NK
NEG = -1e30

OFF_CX, OFF_CB, OFF_CC = 0, 512, 1024
OFF_GQ, OFF_GK, OFF_GV, OFF_GG = 1536, 1792, 2048, 2560
OFF_NQ = 3072
OFF_NKC, OFF_NVC, OFF_NKS, OFF_NVS, OFF_NKW, OFF_NVW = 4096, 4352, 4608, 4864, 5120, 5376
OFF_MISC = 5632
N_IN = 5760
ORIG_GA, ORIG_NQ, ORIG_GATE, ORIG_END = 3072, 3088, 5648, 5672

TM_PROJ = 512
TN_IN = 1152
TS_ELT = 512
TC_GLA = 512
TQ = 128
TK = 512
TM_PEER = 512
TE_PEER = 512
TM_ROUTE = 256
VMEM_LIMIT = 56 * 1024 * 1024


def _cp(sem):
    return pltpu.CompilerParams(dimension_semantics=sem, vmem_limit_bytes=VMEM_LIMIT)


def _gelu(x):
    return 0.5 * x * (1.0 + jnp.tanh(0.7978845608028654 * (x + 0.044715 * (x * x * x))))


def _split_bf16(x):
    hi = x.astype(bf16)
    lo = (x - hi.astype(f32)).astype(bf16)
    return hi, lo


def _group_rms(y, gain):
    ms = jnp.mean(y * y, axis=-1, keepdims=True)
    return y * lax.rsqrt(ms + EPS) * gain


def _norm_matmul_kernel(x_ref, g_ref, w_ref, *refs, emit_xn):
    if emit_xn:
        o_ref, xo_ref, xn_ref = refs
    else:
        o_ref, xn_ref = refs

    @pl.when(pl.program_id(1) == 0)
    def _():
        x = x_ref[...]
        ms = jnp.mean(x * x, axis=-1, keepdims=True)
        xn = (x * lax.rsqrt(ms + EPS) * g_ref[...]).astype(bf16)
        xn_ref[...] = xn
        if emit_xn:
            xo_ref[...] = xn

    o_ref[...] = jnp.dot(xn_ref[...], w_ref[...], preferred_element_type=f32)


def _norm_matmul(x, gain, w, tn, emit_xn=False):
    t, d = x.shape
    n = w.shape[1]
    tm = TM_PROJ
    out_shape = [jax.ShapeDtypeStruct((t, n), f32)]
    out_specs = [pl.BlockSpec((tm, tn), lambda i, j: (i, j))]
    if emit_xn:
        out_shape.append(jax.ShapeDtypeStruct((t, d), bf16))
        out_specs.append(pl.BlockSpec((tm, d), lambda i, j: (i, 0)))
    outs = pl.pallas_call(
        functools.partial(_norm_matmul_kernel, emit_xn=emit_xn),
        grid=(t // tm, n // tn),
        in_specs=[
            pl.BlockSpec((tm, d), lambda i, j: (i, 0)),
            pl.BlockSpec((1, d), lambda i, j: (0, 0)),
            pl.BlockSpec((d, tn), lambda i, j: (0, j)),
        ],
        out_specs=out_specs,
        out_shape=out_shape,
        scratch_shapes=[pltpu.VMEM((tm, d), bf16)],
        compiler_params=_cp(("parallel", "arbitrary")),
        name="norm_matmul",
    )(x, gain.reshape(1, d), w)
    return outs if emit_xn else outs[0]


def _conv_kernel(cx_ref, cb_ref, cc_ref, pcx_ref, pcc_ref, w_ref, g_ref, o_ref, z_ref, *, ts, tiles_per_seq):
    first = (pl.program_id(0) % tiles_per_seq) == 0
    zp = pcx_ref[...] * pcc_ref[...]
    z_ref[0:8, :] = jnp.where(first, 0.0, zp)
    z_ref[8:8 + ts, :] = cx_ref[...] * cc_ref[...]
    y = (z_ref[8:8 + ts, :] * w_ref[0:1, :]
         + z_ref[7:7 + ts, :] * w_ref[1:2, :]
         + z_ref[6:6 + ts, :] * w_ref[2:3, :])
    y = cb_ref[...] * y
    for c in range(CONV_W // HEAD):
        sl = slice(c * HEAD, (c + 1) * HEAD)
        o_ref[:, sl] = _group_rms(y[:, sl], g_ref[:, sl]).astype(bf16)


def _conv_mixer(p, conv_w, gain, seq):
    t = p.shape[0]
    ts = TS_ELT
    w_t = conv_w.astype(f32).T
    cur = lambda off: pl.BlockSpec((ts, CONV_W), lambda i, off=off: (i, off // CONV_W))
    prev = lambda off: pl.BlockSpec((8, CONV_W), lambda i, off=off: (jnp.maximum(i * (ts // 8) - 1, 0), off // CONV_W))
    return pl.pallas_call(
        functools.partial(_conv_kernel, ts=ts, tiles_per_seq=seq // ts),
        grid=(t // ts,),
        in_specs=[cur(OFF_CX), cur(OFF_CB), cur(OFF_CC), prev(OFF_CX), prev(OFF_CC),
                  pl.BlockSpec((CONV_K, CONV_W), lambda i: (0, 0)),
                  pl.BlockSpec((1, CONV_W), lambda i: (0, 0))],
        out_specs=pl.BlockSpec((ts, CONV_W), lambda i: (i, 0)),
        out_shape=jax.ShapeDtypeStruct((t, CONV_W), bf16),
        scratch_shapes=[pltpu.VMEM((ts + 8, CONV_W), f32)],
        compiler_params=_cp(("parallel",)),
        name="conv_mixer",
    )(p, p, p, p, p, w_t, gain.reshape(1, CONV_W))


def _gla_kernel(q_ref, k_ref, v_ref, gg_ref, misc_ref, wa_ref, ba_ref, gn_ref, o_ref,
                st_ref, la_ref, y_ref, *, tc):
    @pl.when(pl.program_id(1) == 0)
    def _():
        st_ref[...] = jnp.zeros_like(st_ref)

    a_hi, a_lo = _split_bf16(misc_ref[:, 0:GLA_LR])
    w_hi, w_lo = _split_bf16(wa_ref[...])
    z = (jnp.dot(a_hi, w_hi, preferred_element_type=f32)
         + jnp.dot(a_hi, w_lo, preferred_element_type=f32)
         + jnp.dot(a_lo, w_hi, preferred_element_type=f32)) + ba_ref[...]
    la_ref[...] = (jnp.minimum(z, 0.0) - jnp.log(1.0 + jnp.exp(-jnp.abs(z)))) * (1.0 / GLA_TAU)

    c = GLA_CHUNK
    row = lax.broadcasted_iota(jnp.int32, (c, c), 0)
    col = lax.broadcasted_iota(jnp.int32, (c, c), 1)
    causal = row >= col
    tril = causal.astype(bf16)
    scale = GLA_DK ** -0.5

    def chunk(ci, carry):
        r0 = pl.multiple_of(ci * c, c)
        la = la_ref[pl.ds(r0, c), :]
        la_hi, la_lo = _split_bf16(la)
        la_lo2 = (la - la_hi.astype(f32) - la_lo.astype(f32)).astype(bf16)
        bc = (jnp.dot(tril, la_hi, preferred_element_type=f32)
              + jnp.dot(tril, la_lo, preferred_element_type=f32)
              + jnp.dot(tril, la_lo2, preferred_element_type=f32))
        bl = bc[c - 1:c, :]
        qc = q_ref[pl.ds(r0, c), :]
        kc = k_ref[pl.ds(r0, c), :]
        vc = v_ref[pl.ds(r0, c), :].astype(bf16)
        q_dec = (qc * scale * jnp.exp(bc)).astype(bf16)
        k_inv = (kc * jnp.exp(-bc)).astype(bf16)
        k_end = (kc * jnp.exp(bl - bc)).astype(bf16)
        decay = jnp.exp(bl)
        for h in range(GLA_H):
            ks = slice(h * GLA_DK, (h + 1) * GLA_DK)
            vs = slice(h * GLA_DV, (h + 1) * GLA_DV)
            att = lax.dot_general(q_dec[:, ks], k_inv[:, ks], (((1,), (1,)), ((), ())),
                                  preferred_element_type=f32)
            att = jnp.where(causal, att, 0.0).astype(bf16)
            o = jnp.dot(att, vc[:, vs], preferred_element_type=f32)
            st = st_ref[h]
            o = o + lax.dot_general(q_dec[:, ks], st.astype(bf16), (((1,), (1,)), ((), ())),
                                    preferred_element_type=f32)
            y_ref[pl.ds(r0, c), vs] = o
            upd = lax.dot_general(vc[:, vs], k_end[:, ks], (((0,), (0,)), ((), ())),
                                  preferred_element_type=f32)
            st_ref[h] = st * decay[:, ks] + upd
        return carry

    lax.fori_loop(0, tc // c, chunk, 0)

    g = gg_ref[...]
    sw = g * (1.0 / (1.0 + jnp.exp(-g)))
    for h in range(GLA_H):
        vs = slice(h * GLA_DV, (h + 1) * GLA_DV)
        o_ref[:, vs] = (_group_rms(y_ref[:, vs], gn_ref[:, vs]) * sw[:, vs]).astype(bf16)


def _gla_mixer(p, w_a2, b_a, gain, batch, seq):
    t = p.shape[0]
    tc = TC_GLA
    nst = seq // tc
    blk = lambda width, off: pl.BlockSpec((tc, width), lambda b, s, off=off, width=width: (b * nst + s, off // width))
    const = lambda shape: pl.BlockSpec(shape, lambda b, s: (0,) * len(shape))
    return pl.pallas_call(
        functools.partial(_gla_kernel, tc=tc),
        grid=(batch, nst),
        in_specs=[blk(256, OFF_GQ), blk(256, OFF_GK), blk(512, OFF_GV), blk(512, OFF_GG), blk(128, OFF_MISC),
                  const((GLA_LR, GLA_H * GLA_DK)), const((1, GLA_H * GLA_DK)), const((1, GLA_H * GLA_DV))],
        out_specs=pl.BlockSpec((tc, GLA_H * GLA_DV), lambda b, s: (b * nst + s, 0)),
        out_shape=jax.ShapeDtypeStruct((t, GLA_H * GLA_DV), bf16),
        scratch_shapes=[pltpu.VMEM((GLA_H, GLA_DV, GLA_DK), f32),
                        pltpu.VMEM((tc, GLA_H * GLA_DK), f32),
                        pltpu.VMEM((tc, GLA_H * GLA_DV), f32)],
        compiler_params=_cp(("parallel", "arbitrary")),
        name="gla_mixer",
    )(p, p, p, p, p, w_a2.astype(f32), b_a.astype(f32).reshape(1, -1), gain.reshape(1, -1))


def _compress_kernel(t_ref, pos_ref, w1_ref, w2_ref, gn_ref, o_ref, b_scr, *, n_piece, apply_norm):
    half = CMP_STRIDE
    acc_a = jnp.zeros((n_piece, NSA_D), f32)
    acc_b = jnp.zeros((n_piece, NSA_D), f32)
    for r in range(half):
        rows = t_ref[pl.ds(r, n_piece, stride=CMP_STRIDE), :]
        xa = (rows + pos_ref[r:r + 1, :]).astype(bf16)
        acc_a = acc_a + jnp.dot(xa, w1_ref[r], preferred_element_type=f32)
        xb = (rows + pos_ref[half + r:half + r + 1, :]).astype(bf16)
        acc_b = acc_b + jnp.dot(xb, w1_ref[half + r], preferred_element_type=f32)
    b_scr[0:n_piece, :] = acc_b
    b_scr[n_piece:n_piece + 8, :] = jnp.zeros((8, NSA_D), f32)
    hid = acc_a + b_scr[1:n_piece + 1, :]
    out = jnp.dot(_gelu(hid).astype(bf16), w2_ref[...], preferred_element_type=f32)
    if apply_norm:
        out = _group_rms(out, gn_ref[...])
    o_ref[0] = out.astype(bf16)


def _compress(p, off, pos, w1, w2, gain, batch, seq, apply_norm):
    n_piece = seq // CMP_STRIDE
    return pl.pallas_call(
        functools.partial(_compress_kernel, n_piece=n_piece, apply_norm=apply_norm),
        grid=(batch, NSA_G),
        in_specs=[pl.BlockSpec((seq, NSA_D), lambda b, g: (b, off // NSA_D + g)),
                  pl.BlockSpec((CMP_LEN, NSA_D), lambda b, g: (0, 0)),
                  pl.BlockSpec((CMP_LEN, NSA_D, NSA_D), lambda b, g: (0, 0, 0)),
                  pl.BlockSpec((NSA_D, NSA_D), lambda b, g: (0, 0)),
                  pl.BlockSpec((1, NSA_D), lambda b, g: (0, 0))],
        out_specs=pl.BlockSpec((1, n_piece, NSA_D), lambda b, g: (b * NSA_G + g, 0, 0)),
        out_shape=jax.ShapeDtypeStruct((batch * NSA_G, n_piece, NSA_D), bf16),
        scratch_shapes=[pltpu.VMEM((n_piece + 8, NSA_D), f32)],
        compiler_params=_cp(("parallel", "parallel")),
        name="nsa_compress",
    )(p, pos.astype(f32), w1.astype(bf16).reshape(CMP_LEN, NSA_D, NSA_D), w2.astype(bf16),
      gain.astype(f32).reshape(1, NSA_D))


def _kvprep_kernel(ks_ref, vs_ref, kw_ref, vw_ref, gs_ref, gw_ref, oks_ref, ovs_ref, okw_ref, ovw_ref):
    oks_ref[0] = _group_rms(ks_ref[...], gs_ref[...]).astype(bf16)
    ovs_ref[0] = vs_ref[...].astype(bf16)
    okw_ref[0] = _group_rms(kw_ref[...], gw_ref[...]).astype(bf16)
    ovw_ref[0] = vw_ref[...].astype(bf16)


def _kv_prep(p, k_norm, batch, seq):
    ts = TS_ELT
    nst = seq // ts
    blk = lambda off: pl.BlockSpec((ts, NSA_D), lambda b, g, s, off=off: (b * nst + s, off // NSA_D + g))
    gspec = pl.BlockSpec((1, NSA_D), lambda b, g, s: (0, 0))
    ospec = pl.BlockSpec((1, ts, NSA_D), lambda b, g, s: (b * NSA_G + g, s, 0))
    oshape = jax.ShapeDtypeStruct((batch * NSA_G, seq, NSA_D), bf16)
    return pl.pallas_call(
        _kvprep_kernel,
        grid=(batch, NSA_G, nst),
        in_specs=[blk(OFF_NKS), blk(OFF_NVS), blk(OFF_NKW), blk(OFF_NVW), gspec, gspec],
        out_specs=[ospec] * 4,
        out_shape=[oshape] * 4,
        compiler_params=_cp(("parallel", "parallel", "parallel")),
        name="nsa_kv_prep",
    )(p, p, p, p, k_norm[1].astype(f32).reshape(1, NSA_D), k_norm[2].astype(f32).reshape(1, NSA_D))


def _stack_q(q_ref, qn_ref):
    parts = []
    for r in range(NSA_R):
        parts.append(_group_rms(q_ref[:, r * NSA_D:(r + 1) * NSA_D], qn_ref[...]).astype(bf16))
    return jnp.concatenate(parts, axis=0)


def _head_slope(g, r):
    return jnp.where(g == 0, 1.0, 2.0 ** -NSA_R).astype(f32) * (2.0 ** -(r + 1))


def _nsa_cmp_kernel(q_ref, qn_ref, kcc_ref, vcc_ref, ov_ref, oc_ref, sel_ref, any_ref, *, tq, n_cmp, n_sel):
    g = pl.program_id(1)
    q0 = pl.program_id(2) * tq
    qs = _stack_q(q_ref, qn_ref)
    kcc = kcc_ref[0]
    vcc = vcc_ref[0]
    s_all = lax.dot_general(qs, kcc, (((1,), (1,)), ((), ())), preferred_element_type=f32) * (NSA_D ** -0.5)
    tpos = q0 + lax.broadcasted_iota(jnp.int32, (tq, n_cmp), 0)
    cend = lax.broadcasted_iota(jnp.int32, (tq, n_cmp), 1) * CMP_STRIDE + (CMP_LEN - 1)
    dist = tpos - cend
    valid = dist >= 0
    distf = dist.astype(f32)
    psum = jnp.zeros((tq, n_cmp), f32)
    for r in range(NSA_R):
        s = s_all[r * tq:(r + 1) * tq, :] - _head_slope(g, r) * distf
        s = jnp.where(valid, s, NEG)
        m = jnp.max(s, axis=-1, keepdims=True)
        pr = jnp.where(valid, jnp.exp(s - m), 0.0)
        den = jnp.maximum(jnp.sum(pr, axis=-1, keepdims=True), 1e-30)
        pr = pr * (1.0 / den)
        oc_ref[:, r * NSA_D:(r + 1) * NSA_D] = jnp.dot(pr.astype(bf16), vcc, preferred_element_type=f32)
        psum = psum + pr
    p_hi, p_lo = _split_bf16(psum)
    imp = (jnp.dot(p_hi, ov_ref[...], preferred_element_type=f32)
           + jnp.dot(p_lo, ov_ref[...], preferred_element_type=f32))
    blk = lax.broadcasted_iota(jnp.int32, (tq, n_sel), 1)
    tblk = (q0 + lax.broadcasted_iota(jnp.int32, (tq, n_sel), 0)) // SEL_BLOCK
    lag = tblk - blk
    score = jnp.where(blk == 0, BIG, jnp.where(lag < SEL_LOCAL, BIG, imp))
    score = jnp.where(lag >= 0, score, -1.0)
    sel = jnp.zeros((tq, n_sel), f32)
    for _ in range(SEL_TOPK):
        m = jnp.max(score, axis=-1, keepdims=True)
        first = jnp.min(jnp.where(score == m, blk, n_sel), axis=-1, keepdims=True)
        hit = blk == first
        sel = jnp.where(hit, 1.0, sel)
        score = jnp.where(hit, -2.0, score)
    sel_ref[...] = sel.astype(bf16)
    any_ref[0] = jnp.max(sel, axis=0, keepdims=True)


def _nsa_cmp(p, q_norm, kcc, vcc, batch, seq):
    tq = TQ
    nq = seq // tq
    n_cmp = seq // CMP_STRIDE
    n_sel = seq // SEL_BLOCK
    ci = np.arange(n_cmp)[:, None] * CMP_STRIDE
    sj = np.arange(n_sel)[None, :] * SEL_BLOCK
    ov = np.clip(np.minimum(ci + CMP_LEN, sj + SEL_BLOCK) - np.maximum(ci, sj), 0, None) / CMP_LEN
    ov = jnp.asarray(ov, dtype=bf16)
    t = batch * seq
    return pl.pallas_call(
        functools.partial(_nsa_cmp_kernel, tq=tq, n_cmp=n_cmp, n_sel=n_sel),
        grid=(batch, NSA_G, nq),
        in_specs=[pl.BlockSpec((tq, NSA_R * NSA_D), lambda b, g, i: (b * nq + i, OFF_NQ // (NSA_R * NSA_D) + g)),
                  pl.BlockSpec((1, NSA_D), lambda b, g, i: (0, 0)),
                  pl.BlockSpec((1, n_cmp, NSA_D), lambda b, g, i: (b * NSA_G + g, 0, 0)),
                  pl.BlockSpec((1, n_cmp, NSA_D), lambda b, g, i: (b * NSA_G + g, 0, 0)),
                  pl.BlockSpec((n_cmp, n_sel), lambda b, g, i: (0, 0))],
        out_specs=[pl.BlockSpec((tq, NSA_R * NSA_D), lambda b, g, i: (b * nq + i, g)),
                   pl.BlockSpec((tq, n_sel), lambda b, g, i: ((b * NSA_G + g) * nq + i, 0)),
                   pl.BlockSpec((1, 1, n_sel), lambda b, g, i: ((b * NSA_G + g) * nq + i, 0, 0))],
        out_shape=[jax.ShapeDtypeStruct((t, NSA_G * NSA_R * NSA_D), f32),
                   jax.ShapeDtypeStruct((batch * NSA_G * seq, n_sel), bf16),
                   jax.ShapeDtypeStruct((batch * NSA_G * nq, 1, n_sel), f32)],
        compiler_params=_cp(("parallel", "parallel", "parallel")),
        name="nsa_cmp_select",
    )(p, q_norm.astype(f32).reshape(1, NSA_D), kcc, vcc, ov)


def _nsa_attn_kernel(flag_ref, q_ref, qn_ref, misc_ref, oc_ref, sel_ref, ks_ref, vs_ref, kw_ref, vw_ref, gn_ref,
                     o_ref, m_scr, l_scr, acc_scr, os_scr, *, tq, tk, n_sel, nkt):
    b = pl.program_id(0)
    g = pl.program_id(1)
    qi = pl.program_id(2)
    nq = pl.num_programs(2)
    q0 = qi * tq
    rows = NSA_R * tq
    qs = _stack_q(q_ref, qn_ref)
    scale = NSA_D ** -0.5
    tpos = q0 + lax.broadcasted_iota(jnp.int32, (tq, tk), 0)
    tpos = jnp.concatenate([tpos] * NSA_R, axis=0)
    slope = jnp.concatenate([jnp.full((tq, 1), 1.0, f32) * _head_slope(g, r) for r in range(NSA_R)], axis=0)
    kiota = lax.broadcasted_iota(jnp.int32, (rows, tk), 1)
    sel = sel_ref[...]
    kt_last = (q0 + tq - 1) // tk

    def reset():
        m_scr[...] = jnp.full((rows, 1), NEG, f32)
        l_scr[...] = jnp.zeros((rows, 1), f32)
        acc_scr[...] = jnp.zeros((rows, NSA_D), f32)

    def step(kt, k_ref, v_ref, allowed):
        k0 = pl.multiple_of(kt * tk, tk)
        kb = k_ref[0, pl.ds(k0, tk), :]
        vb = v_ref[0, pl.ds(k0, tk), :]
        dist = tpos - (k0 + kiota)
        s = lax.dot_general(qs, kb, (((1,), (1,)), ((), ())), preferred_element_type=f32) * scale
        s = s - slope * dist.astype(f32)
        s = jnp.where(dist >= 0, jnp.where(allowed(dist, kt), s, NEG), NEG)
        m_old = m_scr[...]
        m_new = jnp.maximum(m_old, jnp.max(s, axis=-1, keepdims=True))
        alpha = jnp.exp(m_old - m_new)
        pr = jnp.where(s > 0.5 * NEG, jnp.exp(s - m_new), 0.0)
        l_scr[...] = alpha * l_scr[...] + jnp.sum(pr, axis=-1, keepdims=True)
        acc_scr[...] = alpha * acc_scr[...] + jnp.dot(pr.astype(bf16), vb, preferred_element_type=f32)
        m_scr[...] = m_new

    def finish():
        return acc_scr[...] * (1.0 / jnp.maximum(l_scr[...], 1e-30))

    def sel_allowed(dist, kt):
        jb = lax.broadcasted_iota(jnp.int32, (n_sel, tk), 0)
        cb = lax.broadcasted_iota(jnp.int32, (n_sel, tk), 1) // SEL_BLOCK + kt * (tk // SEL_BLOCK)
        expand = (jb == cb).astype(bf16)
        mk = jnp.dot(sel, expand, preferred_element_type=f32)
        return jnp.concatenate([mk] * NSA_R, axis=0) > 0.5

    reset()
    fbase = ((b * NSA_G + g) * nq + qi) * nkt

    def sel_loop(kt, carry):
        @pl.when(flag_ref[fbase + kt] > 0)
        def _():
            step(kt, ks_ref, vs_ref, sel_allowed)
        return carry

    lax.fori_loop(0, kt_last + 1, sel_loop, 0)
    os_scr[...] = finish()

    reset()
    kt_first = jnp.maximum(q0 - (WINDOW - 1), 0) // tk

    def win_loop(kt, carry):
        step(kt, kw_ref, vw_ref, lambda dist, kt: dist < WINDOW)
        return carry

    lax.fori_loop(kt_first, kt_last + 1, win_loop, 0)
    o_w = finish()
    o_s = os_scr[...]

    for r in range(NSA_R):
        lanes = []
        for c in range(3):
            l0 = GLA_LR + r * 3 + c
            l1 = l0 + NSA_R * 3
            lanes.append(jnp.where(g == 0, misc_ref[:, l0:l0 + 1], misc_ref[:, l1:l1 + 1]))
        gc, gs, gw = [1.0 / (1.0 + jnp.exp(-x)) for x in lanes]
        hs = slice(r * NSA_D, (r + 1) * NSA_D)
        rs = slice(r * tq, (r + 1) * tq)
        y = gc * oc_ref[:, hs] + gs * o_s[rs, :] + gw * o_w[rs, :]
        o_ref[:, hs] = _group_rms(y, gn_ref[:, hs]).astype(bf16)


def _nsa_attn(p, q_norm, o_c, sel, flags, ks, vs, kw, vw, gain, batch, seq):
    tq, tk = TQ, TK
    nq = seq // tq
    nkt = seq // tk
    n_sel = seq // SEL_BLOCK
    t = batch * seq
    hw = NSA_R * NSA_D
    slab = lambda: pl.BlockSpec((1, seq, NSA_D), lambda b, g, i, f: (b * NSA_G + g, 0, 0),
                                pipeline_mode=pl.Buffered(1))
    grid_spec = pltpu.PrefetchScalarGridSpec(
        num_scalar_prefetch=1,
        grid=(batch, NSA_G, nq),
        in_specs=[pl.BlockSpec((tq, hw), lambda b, g, i, f: (b * nq + i, OFF_NQ // hw + g)),
                  pl.BlockSpec((1, NSA_D), lambda b, g, i, f: (0, 0)),
                  pl.BlockSpec((tq, 128), lambda b, g, i, f: (b * nq + i, OFF_MISC // 128)),
                  pl.BlockSpec((tq, hw), lambda b, g, i, f: (b * nq + i, g)),
                  pl.BlockSpec((tq, n_sel), lambda b, g, i, f: ((b * NSA_G + g) * nq + i, 0)),
                  slab(), slab(), slab(), slab(),
                  pl.BlockSpec((1, hw), lambda b, g, i, f: (0, g))],
        out_specs=pl.BlockSpec((tq, hw), lambda b, g, i, f: (b * nq + i, g)),
        scratch_shapes=[pltpu.VMEM((NSA_R * tq, 1), f32), pltpu.VMEM((NSA_R * tq, 1), f32),
                        pltpu.VMEM((NSA_R * tq, NSA_D), f32), pltpu.VMEM((NSA_R * tq, NSA_D), f32)],
    )
    return pl.pallas_call(
        functools.partial(_nsa_attn_kernel, tq=tq, tk=tk, n_sel=n_sel, nkt=nkt),
        grid_spec=grid_spec,
        out_shape=jax.ShapeDtypeStruct((t, NSA_G * hw), bf16),
        compiler_params=_cp(("parallel", "parallel", "arbitrary")),
        name="nsa_attention",
    )(flags, p, q_norm.astype(f32).reshape(1, NSA_D), p, o_c, sel, ks, vs, kw, vw, gain.reshape(1, -1))


def _out_proj_kernel(a_ref, b_ref, c_ref, wa_ref, wb_ref, wc_ref, h_ref, o_ref):
    o_ref[...] = (h_ref[...]
                  + jnp.dot(a_ref[...], wa_ref[...], preferred_element_type=f32)
                  + jnp.dot(b_ref[...], wb_ref[...], preferred_element_type=f32)
                  + jnp.dot(c_ref[...], wc_ref[...], preferred_element_type=f32))


def _out_proj(ya, yb, yc, w_out, h):
    t, d = h.shape
    tm, tn = TM_PROJ, 1024
    w = w_out.astype(bf16)
    ka, kb, kc = ya.shape[1], yb.shape[1], yc.shape[1]
    return pl.pallas_call(
        _out_proj_kernel,
        grid=(t // tm, d // tn),
        in_specs=[pl.BlockSpec((tm, ka), lambda i, j: (i, 0)),
                  pl.BlockSpec((tm, kb), lambda i, j: (i, 0)),
                  pl.BlockSpec((tm, kc), lambda i, j: (i, 0)),
                  pl.BlockSpec((ka, tn), lambda i, j: (0, j)),
                  pl.BlockSpec((kb, tn), lambda i, j: (1, j)),
                  pl.BlockSpec((kc, tn), lambda i, j: (1, j)),
                  pl.BlockSpec((tm, tn), lambda i, j: (i, j))],
        out_specs=pl.BlockSpec((tm, tn), lambda i, j: (i, j)),
        out_shape=jax.ShapeDtypeStruct((t, d), f32),
        compiler_params=_cp(("parallel", "parallel")),
        name="out_proj",
    )(ya, yb, yc, w, w, w, h)


def _top_values(v, k):
    rows = v.shape[0]
    ridx = lax.broadcasted_iota(jnp.int32, v.shape, 0)
    out = []
    for _ in range(k):
        m = jnp.max(v, axis=0, keepdims=True)
        out.append(m)
        first = jnp.min(jnp.where(v == m, ridx, rows), axis=0, keepdims=True)
        v = jnp.where(ridx == first, -jnp.inf, v)
    return out


def _peer_route_kernel(q_ref, key_ref, s1_ref, a_ref, s2_ref, e2_ref, tau_ref):
    taus = []
    for h in range(PEER_H):
        sc = []
        tops = []
        for half in range(2):
            c0 = (h * 2 + half) * PEER_NK
            qh = q_ref[:, c0:c0 + PEER_NK].astype(bf16)
            s = lax.dot_general(key_ref[h * 2 + half], qh, (((1,), (1,)), ((), ())),
                                preferred_element_type=f32)
            sc.append(s)
            tops.append(_top_values(s, PEER_TOPK))
        t2 = jnp.concatenate(tops[1], axis=0)
        cand = jnp.concatenate([tops[0][a] + t2 for a in range(PEER_TOPK)], axis=0)
        best = _top_values(cand, PEER_TOPK)
        mass = best[0] * 0.0
        for bk in best:
            mass = mass + jnp.exp(bk - best[0])
        s1_ref[h] = sc[0]
        a_ref[h] = jnp.exp(sc[0] - tops[0][0]) * (1.0 / mass)
        s2_ref[h] = sc[1]
        e2_ref[h] = jnp.exp(sc[1] - tops[1][0])
        taus.append(best[PEER_TOPK - 1])
    tau_ref[...] = jnp.concatenate(taus, axis=0)


def _peer_route(qry, keys):
    t = qry.shape[0]
    tm = TM_ROUTE
    big = jax.ShapeDtypeStruct((PEER_H, PEER_NK, t), f32)
    bspec = pl.BlockSpec((PEER_H, PEER_NK, tm), lambda i: (0, 0, i))
    return pl.pallas_call(
        _peer_route_kernel,
        grid=(t // tm,),
        in_specs=[pl.BlockSpec((tm, PEER_H * 2 * PEER_NK), lambda i: (i, 0)),
                  pl.BlockSpec((PEER_H * 2, PEER_NK, PEER_NK), lambda i: (0, 0, 0))],
        out_specs=[bspec, bspec, bspec, bspec, pl.BlockSpec((PEER_H, tm), lambda i: (0, i))],
        out_shape=[big, big, big, big, jax.ShapeDtypeStruct((PEER_H, t), f32)],
        compiler_params=_cp(("parallel",)),
        name="peer_route",
    )(qry, keys.astype(bf16).reshape(PEER_H * 2, PEER_NK, PEER_NK))


def _peer_expert_kernel(xn_ref, u_ref, v_ref, s1_ref, a_ref, s2_ref, e2_ref, tau_ref, h_ref, o_ref, acc_ref,
                        *, tm, te):
    j = pl.program_id(1)

    @pl.when(j == 0)
    def _():
        acc_ref[...] = jnp.zeros_like(acc_ref)

    act = _gelu(lax.dot_general(u_ref[...], xn_ref[...], (((1,), (1,)), ((), ())),
                                preferred_element_type=f32))
    parts = []
    for u in range(te // PEER_NK):
        i = j * (te // PEER_NK) + u
        w = jnp.zeros((PEER_NK, tm), f32)
        for h in range(PEER_H):
            s1 = s1_ref[h, pl.ds(i, 1), :]
            a = a_ref[h, pl.ds(i, 1), :]
            keep = (s1 + s2_ref[h]) >= tau_ref[h:h + 1, :]
            w = w + jnp.where(keep, e2_ref[h], 0.0) * a
        parts.append((w * act[u * PEER_NK:(u + 1) * PEER_NK, :]).astype(bf16))
    gated = jnp.concatenate(parts, axis=0)
    acc_ref[...] += lax.dot_general(gated, v_ref[...], (((0,), (0,)), ((), ())), preferred_element_type=f32)

    @pl.when(j == pl.num_programs(1) - 1)
    def _():
        o_ref[...] = h_ref[...] + acc_ref[...]


def _peer_experts(xn, u, v, s1, a, s2, e2, tau, h):
    t, d = h.shape
    tm, te = TM_PEER, TE_PEER
    rspec = pl.BlockSpec((PEER_H, PEER_NK, tm), lambda i, j: (0, 0, i))
    return pl.pallas_call(
        functools.partial(_peer_expert_kernel, tm=tm, te=te),
        grid=(t // tm, PEER_E // te),
        in_specs=[pl.BlockSpec((tm, d), lambda i, j: (i, 0)),
                  pl.BlockSpec((te, d), lambda i, j: (j, 0)),
                  pl.BlockSpec((te, d), lambda i, j: (j, 0)),
                  rspec, rspec, rspec, rspec,
                  pl.BlockSpec((PEER_H, tm), lambda i, j: (0, i)),
                  pl.BlockSpec((tm, d), lambda i, j: (i, 0))],
        out_specs=pl.BlockSpec((tm, d), lambda i, j: (i, 0)),
        out_shape=jax.ShapeDtypeStruct((t, d), f32),
        scratch_shapes=[pltpu.VMEM((tm, d), f32)],
        compiler_params=_cp(("parallel", "arbitrary")),
        name="peer_experts",
    )(xn, u, v, s1, a, s2, e2, tau, h)


def _permute_w_in(w):
    pad = jnp.zeros((w.shape[0], N_IN - ORIG_END), w.dtype)
    return jnp.concatenate([w[:, :ORIG_GA], w[:, ORIG_NQ:ORIG_GATE], w[:, ORIG_GA:ORIG_NQ],
                            w[:, ORIG_GATE:ORIG_END], pad], axis=1).astype(bf16)


def _mixers(h, norm_mix, w_in, conv_w, gla_w_a2, gla_b_a, q_norm, k_norm, cmp_pos, cmp_w1, cmp_w2, out_norm,
            batch, seq):
    p = _norm_matmul(h, norm_mix.astype(f32), _permute_w_in(w_in), TN_IN)
    gain = out_norm.astype(f32)
    y_conv = _conv_mixer(p, conv_w, gain[:CONV_W], seq)
    y_gla = _gla_mixer(p, gla_w_a2, gla_b_a, gain[CONV_W:2 * CONV_W], batch, seq)
    kcc = _compress(p, OFF_NKC, cmp_pos[0], cmp_w1[0], cmp_w2[0], k_norm[0], batch, seq, True)
    vcc = _compress(p, OFF_NVC, cmp_pos[1], cmp_w1[1], cmp_w2[1], k_norm[0], batch, seq, False)
    ks, vs, kw, vw = _kv_prep(p, k_norm, batch, seq)
    o_c, sel, blk_any = _nsa_cmp(p, q_norm, kcc, vcc, batch, seq)
    nkt = seq // TK
    flags = (blk_any.reshape(-1, nkt, TK // SEL_BLOCK).max(axis=-1) > 0.5).astype(jnp.int32).reshape(-1)
    y_nsa = _nsa_attn(p, q_norm, o_c, sel, flags, ks, vs, kw, vw, gain[2 * CONV_W:], batch, seq)
    return y_conv, y_gla, y_nsa


def _layer(h, norm_mix, w_in, conv_w, gla_w_a2, gla_b_a, q_norm, k_norm, cmp_pos, cmp_w1, cmp_w2, out_norm, w_out,
           norm_ffn, peer_w_q, peer_keys, peer_u, peer_v, batch, seq):
    y_conv, y_gla, y_nsa = _mixers(h, norm_mix, w_in, conv_w, gla_w_a2, gla_b_a, q_norm, k_norm, cmp_pos, cmp_w1,
                                   cmp_w2, out_norm, batch, seq)
    h = _out_proj(y_conv, y_gla, y_nsa, w_out, h)
    qry, xn = _norm_matmul(h, norm_ffn.astype(f32), peer_w_q.astype(bf16), 1024, emit_xn=True)
    s1, a, s2, e2, tau = _peer_route(qry, peer_keys)
    return _peer_experts(xn, peer_u.astype(bf16), peer_v.astype(bf16), s1, a, s2, e2, tau, h)


def kernel(x, norm_mix, w_in, conv_w, gla_w_a2, gla_b_a, nsa_q_norm, nsa_k_norm, nsa_cmp_pos, nsa_cmp_w1,
           nsa_cmp_w2, out_norm, w_out, norm_ffn, peer_w_q, peer_keys, peer_u, peer_v):
    batch, seq, d = x.shape
    h = x.reshape(batch * seq, d)
    for l in range(w_in.shape[0]):
        h = _layer(h, norm_mix[l], w_in[l], conv_w[l], gla_w_a2[l], gla_b_a[l], nsa_q_norm[l], nsa_k_norm[l],
                   nsa_cmp_pos[l], nsa_cmp_w1[l], nsa_cmp_w2[l], out_norm[l], w_out[l], norm_ffn[l], peer_w_q[l],
                   peer_keys[l], peer_u[l], peer_v[l], batch, seq)
    return h.reshape(batch, seq, d)
```

```python
import functools
import math

import numpy as np
import jax
import jax.numpy as jnp
from jax import lax
from jax.experimental import pallas as pl
from jax.experimental.pallas import tpu as pltpu

f32 = jnp.float32
bf16 = jnp.bfloat16

D_MODEL = 2048
EPS = 1e-6
HEAD = 128
CONV_W = 512
CONV_K = 3
GLA_H, GLA_DK, GLA_DV = 4, 64, 128
GLA_LR = 16
GLA_TAU = 16.0
GLA_CHUNK = 64
NSA_G, NSA_R, NSA_D = 2, 4, 128
CMP_LEN, CMP_STRIDE = 32, 16
SEL_BLOCK, SEL_TOPK, SEL_LOCAL = 64, 16, 2
WINDOW = 512
BIG = 1e9
PEER_H, PEER_NK, PEER_TOPK = 8, 128, 16
PEER_E = PEER_NK * PEER_NK
NEG = -1e30
BF16_LANES = 256

OFF_CX, OFF_CB, OFF_CC = 0, 512, 1024
OFF_GQ, OFF_GK, OFF_GV, OFF_GG = 1536, 1792, 2048, 2560
OFF_NQ = 3072
OFF_NKC, OFF_NVC, OFF_NKS, OFF_NVS, OFF_NKW, OFF_NVW = 4096, 4352, 4608, 4864, 5120, 5376
OFF_MISC = 5632
N_IN = 5760
ORIG_GA, ORIG_NQ, ORIG_GATE, ORIG_END = 3072, 3088, 5648, 5672

TM_PROJ = 512
TN_IN = 1152
TS_ELT = 512
TC_GLA = 512
TQ = 128
TK = 512
TM_PEER = 512
TE_PEER = 1024
TM_ROUTE = 256
VMEM_LIMIT = 56 * 1024 * 1024
PEER_EXPERT_FLAGS = None


def _cp(sem, flags=None):
    return pltpu.CompilerParams(dimension_semantics=sem, vmem_limit_bytes=VMEM_LIMIT, flags=flags)


def _gelu(x):
    return 0.5 * x * (1.0 + jnp.tanh(0.7978845608028654 * (x + 0.044715 * (x * x * x))))


def _split_bf16(x):
    hi = x.astype(bf16)
    lo = (x - hi.astype(f32)).astype(bf16)
    return hi, lo


def _group_rms(y, gain):
    ms = jnp.mean(y * y, axis=-1, keepdims=True)
    return y * lax.rsqrt(ms + EPS) * gain


def _norm_matmul_kernel(x_ref, g_ref, w_ref, *refs, emit_xn):
    if emit_xn:
        o_ref, xo_ref, xn_ref = refs
    else:
        o_ref, xn_ref = refs

    @pl.when(pl.program_id(1) == 0)
    def _():
        x = x_ref[...]
        ms = jnp.mean(x * x, axis=-1, keepdims=True)
        xn = (x * lax.rsqrt(ms + EPS) * g_ref[...]).astype(bf16)
        xn_ref[...] = xn
        if emit_xn:
            xo_ref[...] = xn

    o_ref[...] = jnp.dot(xn_ref[...], w_ref[...], preferred_element_type=f32)


def _norm_matmul(x, gain, w, tn, emit_xn=False):
    t, d = x.shape
    n = w.shape[1]
    tm = TM_PROJ
    out_shape = [jax.ShapeDtypeStruct((t, n), f32)]
    out_specs = [pl.BlockSpec((tm, tn), lambda i, j: (i, j))]
    if emit_xn:
        out_shape.append(jax.ShapeDtypeStruct((t, d), bf16))
        out_specs.append(pl.BlockSpec((tm, d), lambda i, j: (i, 0)))
    outs = pl.pallas_call(
        functools.partial(_norm_matmul_kernel, emit_xn=emit_xn),
        grid=(t // tm, n // tn),
        in_specs=[
            pl.BlockSpec((tm, d), lambda i, j: (i, 0)),
            pl.BlockSpec((1, d), lambda i, j: (0, 0)),
            pl.BlockSpec((d, tn), lambda i, j: (0, j)),
        ],
        out_specs=out_specs,
        out_shape=out_shape,
        scratch_shapes=[pltpu.VMEM((tm, d), bf16)],
        compiler_params=_cp(("parallel", "arbitrary")),
        name="norm_matmul",
    )(x, gain.reshape(1, d), w)
    return outs if emit_xn else outs[0]


def _conv_kernel(cx_ref, cb_ref, cc_ref, pcx_ref, pcc_ref, w_ref, g_ref, o_ref, z_ref, *, ts, tiles_per_seq):
    first = (pl.program_id(0) % tiles_per_seq) == 0
    zp = pcx_ref[...] * pcc_ref[...]
    z_ref[0:8, :] = jnp.where(first, 0.0, zp)
    z_ref[8:8 + ts, :] = cx_ref[...] * cc_ref[...]
    y = (z_ref[8:8 + ts, :] * w_ref[0:1, :]
         + z_ref[7:7 + ts, :] * w_ref[1:2, :]
         + z_ref[6:6 + ts, :] * w_ref[2:3, :])
    y = cb_ref[...] * y
    for c in range(CONV_W // HEAD):
        sl = slice(c * HEAD, (c + 1) * HEAD)
        o_ref[:, sl] = _group_rms(y[:, sl], g_ref[:, sl]).astype(bf16)


def _conv_mixer(p, conv_w, gain, seq):
    t = p.shape[0]
    ts = TS_ELT
    w_t = conv_w.astype(f32).T
    cur = lambda off: pl.BlockSpec((ts, CONV_W), lambda i, off=off: (i, off // CONV_W))
    prev = lambda off: pl.BlockSpec((8, CONV_W), lambda i, off=off: (jnp.maximum(i * (ts // 8) - 1, 0), off // CONV_W))
    return pl.pallas_call(
        functools.partial(_conv_kernel, ts=ts, tiles_per_seq=seq // ts),
        grid=(t // ts,),
        in_specs=[cur(OFF_CX), cur(OFF_CB), cur(OFF_CC), prev(OFF_CX), prev(OFF_CC),
                  pl.BlockSpec((CONV_K, CONV_W), lambda i: (0, 0)),
                  pl.BlockSpec((1, CONV_W), lambda i: (0, 0))],
        out_specs=pl.BlockSpec((ts, CONV_W), lambda i: (i, 0)),
        out_shape=jax.ShapeDtypeStruct((t, CONV_W), bf16),
        scratch_shapes=[pltpu.VMEM((ts + 8, CONV_W), f32)],
        compiler_params=_cp(("parallel",)),
        name="conv_mixer",
    )(p, p, p, p, p, w_t, gain.reshape(1, CONV_W))


def _gla_kernel(q_ref, k_ref, v_ref, gg_ref, misc_ref, wa_ref, ba_ref, gn_ref, o_ref,
                st_ref, la_ref, y_ref, *, tc):
    @pl.when(pl.program_id(1) == 0)
    def _():
        st_ref[...] = jnp.zeros_like(st_ref)

    a_hi, a_lo = _split_bf16(misc_ref[:, 0:GLA_LR])
    w_hi, w_lo = _split_bf16(wa_ref[...])
    z = (jnp.dot(a_hi, w_hi, preferred_element_type=f32)
         + jnp.dot(a_hi, w_lo, preferred_element_type=f32)
         + jnp.dot(a_lo, w_hi, preferred_element_type=f32)) + ba_ref[...]
    la_ref[...] = (jnp.minimum(z, 0.0) - jnp.log(1.0 + jnp.exp(-jnp.abs(z)))) * (1.0 / GLA_TAU)

    c = GLA_CHUNK
    row = lax.broadcasted_iota(jnp.int32, (c, c), 0)
    col = lax.broadcasted_iota(jnp.int32, (c, c), 1)
    causal = row >= col
    tril = causal.astype(bf16)
    scale = GLA_DK ** -0.5

    def chunk(ci, carry):
        r0 = pl.multiple_of(ci * c, c)
        la = la_ref[pl.ds(r0, c), :]
        la_hi, la_lo = _split_bf16(la)
        la_lo2 = (la - la_hi.astype(f32) - la_lo.astype(f32)).astype(bf16)
        bc = (jnp.dot(tril, la_hi, preferred_element_type=f32)
              + jnp.dot(tril, la_lo, preferred_element_type=f32)
              + jnp.dot(tril, la_lo2, preferred_element_type=f32))
        bl = bc[c - 1:c, :]
        qc = q_ref[pl.ds(r0, c), :]
        kc = k_ref[pl.ds(r0, c), :]
        vc = v_ref[pl.ds(r0, c), :].astype(bf16)
        q_dec = (qc * scale * jnp.exp(bc)).astype(bf16)
        k_inv = (kc * jnp.exp(-bc)).astype(bf16)
        k_end = (kc * jnp.exp(bl - bc)).astype(bf16)
        decay = jnp.exp(bl)
        for h in range(GLA_H):
            ks = slice(h * GLA_DK, (h + 1) * GLA_DK)
            vs = slice(h * GLA_DV, (h + 1) * GLA_DV)
            att = lax.dot_general(q_dec[:, ks], k_inv[:, ks], (((1,), (1,)), ((), ())),
                                  preferred_element_type=f32)
            att = jnp.where(causal, att, 0.0).astype(bf16)
            o = jnp.dot(att, vc[:, vs], preferred_element_type=f32)
            st = st_ref[h]
            o = o + lax.dot_general(q_dec[:, ks], st.astype(bf16), (((1,), (1,)), ((), ())),
                                    preferred_element_type=f32)
            y_ref[pl.ds(r0, c), vs] = o
            upd = lax.dot_general(vc[:, vs], k_end[:, ks], (((0,), (0,)), ((), ())),
                                  preferred_element_type=f32)
            st_ref[h] = st * decay[:, ks] + upd
        return carry

    lax.fori_loop(0, tc // c, chunk, 0)

    g = gg_ref[...]
    sw = g * (1.0 / (1.0 + jnp.exp(-g)))
    for h in range(GLA_H):
        vs = slice(h * GLA_DV, (h + 1) * GLA_DV)
        o_ref[:, vs] = (_group_rms(y_ref[:, vs], gn_ref[:, vs]) * sw[:, vs]).astype(bf16)


def _gla_mixer(p, w_a2, b_a, gain, batch, seq):
    t = p.shape[0]
    tc = TC_GLA
    nst = seq // tc
    blk = lambda width, off: pl.BlockSpec((tc, width), lambda b, s, off=off, width=width: (b * nst + s, off // width))
    const = lambda shape: pl.BlockSpec(shape, lambda b, s: (0,) * len(shape))
    return pl.pallas_call(
        functools.partial(_gla_kernel, tc=tc),
        grid=(batch, nst),
        in_specs=[blk(256, OFF_GQ), blk(256, OFF_GK), blk(512, OFF_GV), blk(512, OFF_GG), blk(128, OFF_MISC),
                  const((GLA_LR, GLA_H * GLA_DK)), const((1, GLA_H * GLA_DK)), const((1, GLA_H * GLA_DV))],
        out_specs=pl.BlockSpec((tc, GLA_H * GLA_DV), lambda b, s: (b * nst + s, 0)),
        out_shape=jax.ShapeDtypeStruct((t, GLA_H * GLA_DV), bf16),
        scratch_shapes=[pltpu.VMEM((GLA_H, GLA_DV, GLA_DK), f32),
                        pltpu.VMEM((tc, GLA_H * GLA_DK), f32),
                        pltpu.VMEM((tc, GLA_H * GLA_DV), f32)],
        compiler_params=_cp(("parallel", "arbitrary")),
        name="gla_mixer",
    )(p, p, p, p, p, w_a2.astype(f32), b_a.astype(f32).reshape(1, -1), gain.reshape(1, -1))


def _compress_kernel(t_ref, pos_ref, w1_ref, w2_ref, gn_ref, o_ref, b_scr, *, n_piece, apply_norm):
    half = CMP_STRIDE
    acc_a = jnp.zeros((n_piece, NSA_D), f32)
    acc_b = jnp.zeros((n_piece, NSA_D), f32)
    for r in range(half):
        rows = t_ref[pl.ds(r, n_piece, stride=CMP_STRIDE), :]
        xa = (rows + pos_ref[r:r + 1, :]).astype(bf16)
        acc_a = acc_a + jnp.dot(xa, w1_ref[r], preferred_element_type=f32)
        xb = (rows + pos_ref[half + r:half + r + 1, :]).astype(bf16)
        acc_b = acc_b + jnp.dot(xb, w1_ref[half + r], preferred_element_type=f32)
    b_scr[0:n_piece, :] = acc_b
    b_scr[n_piece:n_piece + 8, :] = jnp.zeros((8, NSA_D), f32)
    hid = acc_a + b_scr[1:n_piece + 1, :]
    out = jnp.dot(_gelu(hid).astype(bf16), w2_ref[...], preferred_element_type=f32)
    if apply_norm:
        out = _group_rms(out, gn_ref[...])
    o_ref[0] = out.astype(bf16)


def _compress(p, off, pos, w1, w2, gain, batch, seq, apply_norm):
    n_piece = seq // CMP_STRIDE
    return pl.pallas_call(
        functools.partial(_compress_kernel, n_piece=n_piece, apply_norm=apply_norm),
        grid=(batch, NSA_G),
        in_specs=[pl.BlockSpec((seq, NSA_D), lambda b, g: (b, off // NSA_D + g)),
                  pl.BlockSpec((CMP_LEN, NSA_D), lambda b, g: (0, 0)),
                  pl.BlockSpec((CMP_LEN, NSA_D, NSA_D), lambda b, g: (0, 0, 0)),
                  pl.BlockSpec((NSA_D, NSA_D), lambda b, g: (0, 0)),
                  pl.BlockSpec((1, NSA_D), lambda b, g: (0, 0))],
        out_specs=pl.BlockSpec((1, n_piece, NSA_D), lambda b, g: (b * NSA_G + g, 0, 0)),
        out_shape=jax.ShapeDtypeStruct((batch * NSA_G, n_piece, NSA_D), bf16),
        scratch_shapes=[pltpu.VMEM((n_piece + 8, NSA_D), f32)],
        compiler_params=_cp(("parallel", "parallel")),
        name="nsa_compress",
    )(p, pos.astype(f32), w1.astype(bf16).reshape(CMP_LEN, NSA_D, NSA_D), w2.astype(bf16),
      gain.astype(f32).reshape(1, NSA_D))


def _kvprep_kernel(ks_ref, vs_ref, kw_ref, vw_ref, gs_ref, gw_ref, oks_ref, ovs_ref, okw_ref, ovw_ref):
    oks_ref[0] = _group_rms(ks_ref[...], gs_ref[...]).astype(bf16)
    ovs_ref[0] = vs_ref[...].astype(bf16)
    okw_ref[0] = _group_rms(kw_ref[...], gw_ref[...]).astype(bf16)
    ovw_ref[0] = vw_ref[...].astype(bf16)


def _kv_prep(p, k_norm, batch, seq):
    ts = TS_ELT
    nst = seq // ts
    blk = lambda off: pl.BlockSpec((ts, NSA_D), lambda b, g, s, off=off: (b * nst + s, off // NSA_D + g))
    gspec = pl.BlockSpec((1, NSA_D), lambda b, g, s: (0, 0))
    ospec = pl.BlockSpec((1, ts, NSA_D), lambda b, g, s: (b * NSA_G + g, s, 0))
    oshape = jax.ShapeDtypeStruct((batch * NSA_G, seq, NSA_D), bf16)
    return pl.pallas_call(
        _kvprep_kernel,
        grid=(batch, NSA_G, nst),
        in_specs=[blk(OFF_NKS), blk(OFF_NVS), blk(OFF_NKW), blk(OFF_NVW), gspec, gspec],
        out_specs=[ospec] * 4,
        out_shape=[oshape] * 4,
        compiler_params=_cp(("parallel", "parallel", "parallel")),
        name="nsa_kv_prep",
    )(p, p, p, p, k_norm[1].astype(f32).reshape(1, NSA_D), k_norm[2].astype(f32).reshape(1, NSA_D))


def _stack_q(q_ref, qn_ref):
    parts = []
    for r in range(NSA_R):
        parts.append(_group_rms(q_ref[:, r * NSA_D:(r + 1) * NSA_D], qn_ref[...]).astype(bf16))
    return jnp.concatenate(parts, axis=0)


def _head_slope(g, r):
    return jnp.where(g == 0, 1.0, 2.0 ** -NSA_R).astype(f32) * (2.0 ** -(r + 1))


def _nsa_cmp_kernel(q_ref, qn_ref, kcc_ref, vcc_ref, ov_ref, oc_ref, sel_ref, any_ref, *, tq, n_cmp, n_sel):
    g = pl.program_id(1)
    q0 = pl.program_id(2) * tq
    qs = _stack_q(q_ref, qn_ref)
    kcc = kcc_ref[0]
    vcc = vcc_ref[0]
    s_all = lax.dot_general(qs, kcc, (((1,), (1,)), ((), ())), preferred_element_type=f32) * (NSA_D ** -0.5)
    tpos = q0 + lax.broadcasted_iota(jnp.int32, (tq, n_cmp), 0)
    cend = lax.broadcasted_iota(jnp.int32, (tq, n_cmp), 1) * CMP_STRIDE + (CMP_LEN - 1)
    dist = tpos - cend
    valid = dist >= 0
    distf = dist.astype(f32)
    psum = jnp.zeros((tq, n_cmp), f32)
    for r in range(NSA_R):
        s = s_all[r * tq:(r + 1) * tq, :] - _head_slope(g, r) * distf
        s = jnp.where(valid, s, NEG)
        m = jnp.max(s, axis=-1, keepdims=True)
        pr = jnp.where(valid, jnp.exp(s - m), 0.0)
        den = jnp.maximum(jnp.sum(pr, axis=-1, keepdims=True), 1e-30)
        pr = pr * (1.0 / den)
        oc_ref[:, r * NSA_D:(r + 1) * NSA_D] = jnp.dot(pr.astype(bf16), vcc, preferred_element_type=f32)
        psum = psum + pr
    p_hi, p_lo = _split_bf16(psum)
    imp = (jnp.dot(p_hi, ov_ref[...], preferred_element_type=f32)
           + jnp.dot(p_lo, ov_ref[...], preferred_element_type=f32))
    blk = lax.broadcasted_iota(jnp.int32, (tq, n_sel), 1)
    tblk = (q0 + lax.broadcasted_iota(jnp.int32, (tq, n_sel), 0)) // SEL_BLOCK
    lag = tblk - blk
    score = jnp.where(blk == 0, BIG, jnp.where(lag < SEL_LOCAL, BIG, imp))
    score = jnp.where(lag >= 0, score, -1.0)
    sel = jnp.zeros((tq, n_sel), f32)
    for _ in range(SEL_TOPK):
        m = jnp.max(score, axis=-1, keepdims=True)
        first = jnp.min(jnp.where(score == m, blk, n_sel), axis=-1, keepdims=True)
        hit = blk == first
        sel = jnp.where(hit, 1.0, sel)
        score = jnp.where(hit, -2.0, score)
    sel_ref[...] = sel.astype(bf16)
    any_ref[0] = jnp.max(sel, axis=0, keepdims=True)


def _nsa_cmp(p, q_norm, kcc, vcc, batch, seq):
    tq = TQ
    nq = seq // tq
    n_cmp = seq // CMP_STRIDE
    n_sel = seq // SEL_BLOCK
    ci = np.arange(n_cmp)[:, None] * CMP_STRIDE
    sj = np.arange(n_sel)[None, :] * SEL_BLOCK
    ov = np.clip(np.minimum(ci + CMP_LEN, sj + SEL_BLOCK) - np.maximum(ci, sj), 0, None) / CMP_LEN
    ov = jnp.asarray(ov, dtype=bf16)
    t = batch * seq
    return pl.pallas_call(
        functools.partial(_nsa_cmp_kernel, tq=tq, n_cmp=n_cmp, n_sel=n_sel),
        grid=(batch, NSA_G, nq),
        in_specs=[pl.BlockSpec((tq, NSA_R * NSA_D), lambda b, g, i: (b * nq + i, OFF_NQ // (NSA_R * NSA_D) + g)),
                  pl.BlockSpec((1, NSA_D), lambda b, g, i: (0, 0)),
                  pl.BlockSpec((1, n_cmp, NSA_D), lambda b, g, i: (b * NSA_G + g, 0, 0)),
                  pl.BlockSpec((1, n_cmp, NSA_D), lambda b, g, i: (b * NSA_G + g, 0, 0)),
                  pl.BlockSpec((n_cmp, n_sel), lambda b, g, i: (0, 0))],
        out_specs=[pl.BlockSpec((tq, NSA_R * NSA_D), lambda b, g, i: (b * nq + i, g)),
                   pl.BlockSpec((tq, n_sel), lambda b, g, i: ((b * NSA_G + g) * nq + i, 0)),
                   pl.BlockSpec((1, 1, n_sel), lambda b, g, i: ((b * NSA_G + g) * nq + i, 0, 0))],
        out_shape=[jax.ShapeDtypeStruct((t, NSA_G * NSA_R * NSA_D), f32),
                   jax.ShapeDtypeStruct((batch * NSA_G * seq, n_sel), bf16),
                   jax.ShapeDtypeStruct((batch * NSA_G * nq, 1, n_sel), f32)],
        compiler_params=_cp(("parallel", "parallel", "parallel")),
        name="nsa_cmp_select",
    )(p, q_norm.astype(f32).reshape(1, NSA_D), kcc, vcc, ov)


def _nsa_attn_kernel(flag_ref, q_ref, qn_ref, misc_ref, oc_ref, sel_ref, ks_ref, vs_ref, kw_ref, vw_ref, gn_ref,
                     o_ref, m_scr, l_scr, acc_scr, os_scr, *, tq, tk, n_sel, nkt):
    b = pl.program_id(0)
    g = pl.program_id(1)
    qi = pl.program_id(2)
    nq = pl.num_programs(2)
    q0 = qi * tq
    rows = NSA_R * tq
    qs = _stack_q(q_ref, qn_ref)
    scale = NSA_D ** -0.5
    tpos = q0 + lax.broadcasted_iota(jnp.int32, (tq, tk), 0)
    tpos = jnp.concatenate([tpos] * NSA_R, axis=0)
    slope = jnp.concatenate([jnp.full((tq, 1), 1.0, f32) * _head_slope(g, r) for r in range(NSA_R)], axis=0)
    kiota = lax.broadcasted_iota(jnp.int32, (rows, tk), 1)
    sel = sel_ref[...]
    kt_last = (q0 + tq - 1) // tk

    def reset():
        m_scr[...] = jnp.full((rows, 1), NEG, f32)
        l_scr[...] = jnp.zeros((rows, 1), f32)
        acc_scr[...] = jnp.zeros((rows, NSA_D), f32)

    def step(kt, k_ref, v_ref, allowed):
        k0 = pl.multiple_of(kt * tk, tk)
        kb = k_ref[0, pl.ds(k0, tk), :]
        vb = v_ref[0, pl.ds(k0, tk), :]
        dist = tpos - (k0 + kiota)
        s = lax.dot_general(qs, kb, (((1,), (1,)), ((), ())), preferred_element_type=f32) * scale
        s = s - slope * dist.astype(f32)
        s = jnp.where(dist >= 0, jnp.where(allowed(dist, kt), s, NEG), NEG)
        m_old = m_scr[...]
        m_new = jnp.maximum(m_old, jnp.max(s, axis=-1, keepdims=True))
        alpha = jnp.exp(m_old - m_new)
        pr = jnp.where(s > 0.5 * NEG, jnp.exp(s - m_new), 0.0)
        l_scr[...] = alpha * l_scr[...] + jnp.sum(pr, axis=-1, keepdims=True)
        acc_scr[...] = alpha * acc_scr[...] + jnp.dot(pr.astype(bf16), vb, preferred_element_type=f32)
        m_scr[...] = m_new

    def finish():
        return acc_scr[...] * (1.0 / jnp.maximum(l_scr[...], 1e-30))

    def sel_allowed(dist, kt):
        jb = lax.broadcasted_iota(jnp.int32, (n_sel, tk), 0)
        cb = lax.broadcasted_iota(jnp.int32, (n_sel, tk), 1) // SEL_BLOCK + kt * (tk // SEL_BLOCK)
        expand = (jb == cb).astype(bf16)
        mk = jnp.dot(sel, expand, preferred_element_type=f32)
        return jnp.concatenate([mk] * NSA_R, axis=0) > 0.5

    reset()
    fbase = ((b * NSA_G + g) * nq + qi) * nkt

    def sel_loop(kt, carry):
        @pl.when(flag_ref[fbase + kt] > 0)
        def _():
            step(kt, ks_ref, vs_ref, sel_allowed)
        return carry

    lax.fori_loop(0, kt_last + 1, sel_loop, 0)
    os_scr[...] = finish()

    reset()
    kt_first = jnp.maximum(q0 - (WINDOW - 1), 0) // tk

    def win_loop(kt, carry):
        step(kt, kw_ref, vw_ref, lambda dist, kt: dist < WINDOW)
        return carry

    lax.fori_loop(kt_first, kt_last + 1, win_loop, 0)
    o_w = finish()
    o_s = os_scr[...]

    for r in range(NSA_R):
        lanes = []
        for c in range(3):
            l0 = GLA_LR + r * 3 + c
            l1 = l0 + NSA_R * 3
            lanes.append(jnp.where(g == 0, misc_ref[:, l0:l0 + 1], misc_ref[:, l1:l1 + 1]))
        gc, gs, gw = [1.0 / (1.0 + jnp.exp(-x)) for x in lanes]
        hs = slice(r * NSA_D, (r + 1) * NSA_D)
        rs = slice(r * tq, (r + 1) * tq)
        y = gc * oc_ref[:, hs] + gs * o_s[rs, :] + gw * o_w[rs, :]
        o_ref[:, hs] = _group_rms(y, gn_ref[:, hs]).astype(bf16)


def _nsa_attn(p, q_norm, o_c, sel, flags, ks, vs, kw, vw, gain, batch, seq):
    tq, tk = TQ, TK
    nq = seq // tq
    nkt = seq // tk
    n_sel = seq // SEL_BLOCK
    t = batch * seq
    hw = NSA_R * NSA_D
    slab = lambda: pl.BlockSpec((1, seq, NSA_D), lambda b, g, i, f: (b * NSA_G + g, 0, 0),
                                pipeline_mode=pl.Buffered(1))
    grid_spec = pltpu.PrefetchScalarGridSpec(
        num_scalar_prefetch=1,
        grid=(batch, NSA_G, nq),
        in_specs=[pl.BlockSpec((tq, hw), lambda b, g, i, f: (b * nq + i, OFF_NQ // hw + g)),
                  pl.BlockSpec((1, NSA_D), lambda b, g, i, f: (0, 0)),
                  pl.BlockSpec((tq, 128), lambda b, g, i, f: (b * nq + i, OFF_MISC // 128)),
                  pl.BlockSpec((tq, hw), lambda b, g, i, f: (b * nq + i, g)),
                  pl.BlockSpec((tq, n_sel), lambda b, g, i, f: ((b * NSA_G + g) * nq + i, 0)),
                  slab(), slab(), slab(), slab(),
                  pl.BlockSpec((1, hw), lambda b, g, i, f: (0, g))],
        out_specs=pl.BlockSpec((tq, hw), lambda b, g, i, f: (b * nq + i, g)),
        scratch_shapes=[pltpu.VMEM((NSA_R * tq, 1), f32), pltpu.VMEM((NSA_R * tq, 1), f32),
                        pltpu.VMEM((NSA_R * tq, NSA_D), f32), pltpu.VMEM((NSA_R * tq, NSA_D), f32)],
    )
    return pl.pallas_call(
        functools.partial(_nsa_attn_kernel, tq=tq, tk=tk, n_sel=n_sel, nkt=nkt),
        grid_spec=grid_spec,
        out_shape=jax.ShapeDtypeStruct((t, NSA_G * hw), bf16),
        compiler_params=_cp(("parallel", "parallel", "arbitrary")),
        name="nsa_attention",
    )(flags, p, q_norm.astype(f32).reshape(1, NSA_D), p, o_c, sel, ks, vs, kw, vw, gain.reshape(1, -1))


def _out_proj_kernel(a_ref, b_ref, c_ref, wa_ref, wb_ref, wc_ref, h_ref, o_ref):
    o_ref[...] = (h_ref[...]
                  + jnp.dot(a_ref[...], wa_ref[...], preferred_element_type=f32)
                  + jnp.dot(b_ref[...], wb_ref[...], preferred_element_type=f32)
                  + jnp.dot(c_ref[...], wc_ref[...], preferred_element_type=f32))


def _out_proj(ya, yb, yc, w_out, h):
    t, d = h.shape
    tm, tn = TM_PROJ, 1024
    w = w_out.astype(bf16)
    ka, kb, kc = ya.shape[1], yb.shape[1], yc.shape[1]
    return pl.pallas_call(
        _out_proj_kernel,
        grid=(t // tm, d // tn),
        in_specs=[pl.BlockSpec((tm, ka), lambda i, j: (i, 0)),
                  pl.BlockSpec((tm, kb), lambda i, j: (i, 0)),
                  pl.BlockSpec((tm, kc), lambda i, j: (i, 0)),
                  pl.BlockSpec((ka, tn), lambda i, j: (0, j)),
                  pl.BlockSpec((kb, tn), lambda i, j: (1, j)),
                  pl.BlockSpec((kc, tn), lambda i, j: (1, j)),
                  pl.BlockSpec((tm, tn), lambda i, j: (i, j))],
        out_specs=pl.BlockSpec((tm, tn), lambda i, j: (i, j)),
        out_shape=jax.ShapeDtypeStruct((t, d), f32),
        compiler_params=_cp(("parallel", "parallel")),
        name="out_proj",
    )(ya, yb, yc, w, w, w, h)


def _top_k_columns(v, k):
    rows = v.shape[0]
    ridx = lax.broadcasted_iota(jnp.int32, v.shape, 0)
    vals, firsts = [], []
    rank = jnp.full(v.shape, float(k), f32)
    for kk in range(k):
        m = jnp.max(v, axis=0, keepdims=True)
        first = jnp.min(jnp.where(v == m, ridx, rows), axis=0, keepdims=True)
        hit = ridx == first
        rank = jnp.where(hit, float(kk), rank)
        v = jnp.where(hit, -jnp.inf, v)
        vals.append(m)
        firsts.append(first)
    return vals, rank, firsts


def _peer_route_kernel(q_ref, key_ref, a_ref, cnt_ref, rank2_ref, e2_ref):
    tm = q_ref.shape[0]
    for h in range(PEER_H):
        sc, tops, ranks = [], [], []
        for half in range(2):
            c0 = (h * 2 + half) * PEER_NK
            qh = q_ref[:, c0:c0 + PEER_NK].astype(bf16)
            s = lax.dot_general(key_ref[h * 2 + half], qh, (((1,), (1,)), ((), ())),
                                preferred_element_type=f32)
            vals, rank, _ = _top_k_columns(s, PEER_TOPK)
            sc.append(s)
            tops.append(vals)
            ranks.append(rank)
        t2 = jnp.concatenate(tops[1], axis=0)
        cand = jnp.concatenate([tops[0][a] + t2 for a in range(PEER_TOPK)], axis=0)
        best, _, pos = _top_k_columns(cand, PEER_TOPK)
        mass = jnp.zeros_like(best[0])
        for bk in best:
            mass = mass + jnp.exp(bk - best[0])
        aidx = lax.broadcasted_iota(jnp.int32, (PEER_TOPK, tm), 0)
        per_rank = jnp.zeros((PEER_TOPK, tm), f32)
        for p in pos:
            per_rank = per_rank + jnp.where(aidx == p // PEER_TOPK, 1.0, 0.0)
        cnt = jnp.zeros((PEER_NK, tm), f32)
        for a in range(PEER_TOPK):
            cnt = cnt + jnp.where(ranks[0] == float(a), per_rank[a:a + 1, :], 0.0)
        hs = slice(h * PEER_NK, (h + 1) * PEER_NK)
        a_ref[hs, :] = (jnp.exp(sc[0] - tops[0][0]) * (0.5 / mass)).astype(bf16)
        cnt_ref[hs, :] = cnt.astype(bf16)
        rank2_ref[hs, :] = ranks[1].astype(bf16)
        e2_ref[hs, :] = jnp.exp(sc[1] - tops[1][0]).astype(bf16)


def _peer_route(qry, keys):
    t = qry.shape[0]
    tm = TM_ROUTE
    narrow = jax.ShapeDtypeStruct((PEER_H * PEER_NK, t), bf16)
    nspec = pl.BlockSpec((PEER_H * PEER_NK, tm), lambda i: (0, i))
    return pl.pallas_call(
        _peer_route_kernel,
        grid=(t // tm,),
        in_specs=[pl.BlockSpec((tm, PEER_H * 2 * PEER_NK), lambda i: (i, 0)),
                  pl.BlockSpec((PEER_H * 2, PEER_NK, PEER_NK), lambda i: (0, 0, 0))],
        out_specs=[nspec] * 4,
        out_shape=[narrow] * 4,
        compiler_params=_cp(("parallel",)),
        name="peer_route",
    )(qry, keys.astype(bf16).reshape(PEER_H * 2, PEER_NK, PEER_NK))


def _peer_expert_kernel(xn_ref, u_ref, vt_ref, a_ref, cnt_ref, rank2_ref, e2_ref, h_ref, o_ref,
                        acc_ref, act_ref, gt_ref, *, tm, te):
    j = pl.program_id(1)

    @pl.when(j == 0)
    def _():
        acc_ref[...] = jnp.zeros_like(acc_ref)

    n_first = te // PEER_NK
    i0 = pl.multiple_of(j * n_first, n_first)

    x = lax.dot_general(u_ref[...], xn_ref[...], (((1,), (1,)), ((), ())), preferred_element_type=f32)
    act_ref[...] = (x * (1.0 + jnp.tanh(x * (0.7978845608028654 + 0.035677408136300125 * (x * x))))).astype(bf16)

    zero = jnp.zeros((PEER_NK, BF16_LANES), bf16)
    for c in range(tm // BF16_LANES):
        ls = slice(c * BF16_LANES, (c + 1) * BF16_LANES)
        for u in range(n_first):
            rs = slice(u * PEER_NK, (u + 1) * PEER_NK)
            w = zero
            for h in range(PEER_H):
                hs = slice(h * PEER_NK, (h + 1) * PEER_NK)
                r0 = pl.multiple_of(h * PEER_NK + i0, n_first)
                a = a_ref[pl.ds(r0, n_first), ls][u:u + 1, :]
                cnt = cnt_ref[pl.ds(r0, n_first), ls][u:u + 1, :]
                keep = rank2_ref[hs, ls] < cnt
                w = w + jnp.where(keep, e2_ref[hs, ls], zero) * a
            gt_ref[rs, ls] = w * act_ref[rs, ls]

    acc_ref[...] += jnp.dot(vt_ref[...], gt_ref[...], preferred_element_type=f32)

    @pl.when(j == pl.num_programs(1) - 1)
    def _():
        o_ref[...] = h_ref[...] + acc_ref[...].T


def _peer_experts(xn, u, vt, a, cnt, rank2, e2, h):
    t, d = h.shape
    tm, te = TM_PEER, TE_PEER
    once = pl.Buffered(1)
    nspec = pl.BlockSpec((PEER_H * PEER_NK, tm), lambda i, j: (0, i), pipeline_mode=once)
    return pl.pallas_call(
        functools.partial(_peer_expert_kernel, tm=tm, te=te),
        grid=(t // tm, PEER_E // te),
        in_specs=[pl.BlockSpec((tm, d), lambda i, j: (i, 0), pipeline_mode=once),
                  pl.BlockSpec((te, d), lambda i, j: (j, 0)),
                  pl.BlockSpec((d, te), lambda i, j: (0, j)),
                  nspec, nspec, nspec, nspec,
                  pl.BlockSpec((tm, d), lambda i, j: (i, 0), pipeline_mode=once)],
        out_specs=pl.BlockSpec((tm, d), lambda i, j: (i, 0)),
        out_shape=jax.ShapeDtypeStruct((t, d), f32),
        scratch_shapes=[pltpu.VMEM((d, tm), f32), pltpu.VMEM((te, tm), bf16), pltpu.VMEM((te, tm), bf16)],
        compiler_params=_cp(("parallel", "arbitrary"), PEER_EXPERT_FLAGS),
        name="peer_experts",
    )(xn, u, vt, a, cnt, rank2, e2, h)


def _permute_w_in(w):
    pad = jnp.zeros((w.shape[0], N_IN - ORIG_END), w.dtype)
    return jnp.concatenate([w[:, :ORIG_GA], w[:, ORIG_NQ:ORIG_GATE], w[:, ORIG_GA:ORIG_NQ],
                            w[:, ORIG_GATE:ORIG_END], pad], axis=1).astype(bf16)


def _mixers(h, norm_mix, w_in, conv_w, gla_w_a2, gla_b_a, q_norm, k_norm, cmp_pos, cmp_w1, cmp_w2, out_norm,
            batch, seq):
    p = _norm_matmul(h, norm_mix.astype(f32), _permute_w_in(w_in), TN_IN)
    gain = out_norm.astype(f32)
    y_conv = _conv_mixer(p, conv_w, gain[:CONV_W], seq)
    y_gla = _gla_mixer(p, gla_w_a2, gla_b_a, gain[CONV_W:2 * CONV_W], batch, seq)
    kcc = _compress(p, OFF_NKC, cmp_pos[0], cmp_w1[0], cmp_w2[0], k_norm[0], batch, seq, True)
    vcc = _compress(p, OFF_NVC, cmp_pos[1], cmp_w1[1], cmp_w2[1], k_norm[0], batch, seq, False)
    ks, vs, kw, vw = _kv_prep(p, k_norm, batch, seq)
    o_c, sel, blk_any = _nsa_cmp(p, q_norm, kcc, vcc, batch, seq)
    nkt = seq // TK
    flags = (blk_any.reshape(-1, nkt, TK // SEL_BLOCK).max(axis=-1) > 0.5).astype(jnp.int32).reshape(-1)
    y_nsa = _nsa_attn(p, q_norm, o_c, sel, flags, ks, vs, kw, vw, gain[2 * CONV_W:], batch, seq)
    return y_conv, y_gla, y_nsa


def _layer(h, norm_mix, w_in, conv_w, gla_w_a2, gla_b_a, q_norm, k_norm, cmp_pos, cmp_w1, cmp_w2, out_norm, w_out,
           norm_ffn, peer_w_q, peer_keys, peer_u, peer_v, batch, seq):
    y_conv, y_gla, y_nsa = _mixers(h, norm_mix, w_in, conv_w, gla_w_a2, gla_b_a, q_norm, k_norm, cmp_pos, cmp_w1,
                                   cmp_w2, out_norm, batch, seq)
    h = _out_proj(y_conv, y_gla, y_nsa, w_out, h)
    qry, xn = _norm_matmul(h, norm_ffn.astype(f32), peer_w_q.astype(bf16), 1024, emit_xn=True)
    a, cnt, rank2, e2 = _peer_route(qry, peer_keys)
    return _peer_experts(xn, peer_u.astype(bf16), peer_v.astype(bf16).T, a, cnt, rank2, e2, h)


def kernel(x, norm_mix, w_in, conv_w, gla_w_a2, gla_b_a, nsa_q_norm, nsa_k_norm, nsa_cmp_pos, nsa_cmp_w1,
           nsa_cmp_w2, out_norm, w_out, norm_ffn, peer_w_q, peer_keys, peer_u, peer_v):
    batch, seq, d = x.shape
    h = x.reshape(batch * seq, d)
    for l in range(w_in.shape[0]):
        h = _layer(h, norm_mix[l], w_in[l], conv_w[l], gla_w_a2[l], gla_b_a[l], nsa_q_norm[l], nsa_k_norm[l],
                   nsa_cmp_pos[l], nsa_cmp_w1[l], nsa_cmp_w2[l], out_norm[l], w_out[l], norm_ffn[l], peer_w_q[l],
                   peer_keys[l], peer_u[l], peer_v[l], batch, seq)
    return h.reshape(batch, seq, d)
```

```python
import functools
import math

import numpy as np
import jax
import jax.numpy as jnp
from jax import lax
from jax.experimental import pallas as pl
from jax.experimental.pallas import tpu as pltpu

f32 = jnp.float32
bf16 = jnp.bfloat16

D_MODEL = 2048
EPS = 1e-6
HEAD = 128
CONV_W = 512
CONV_K = 3
GLA_H, GLA_DK, GLA_DV = 4, 64, 128
GLA_LR = 16
GLA_TAU = 16.0
GLA_CHUNK = 64
NSA_G, NSA_R, NSA_D = 2, 4, 128
CMP_LEN, CMP_STRIDE = 32, 16
SEL_BLOCK, SEL_TOPK, SEL_LOCAL = 64, 16, 2
WINDOW = 512
BIG = 1e9
PEER_H, PEER_NK, PEER_TOPK = 8, 128, 16
PEER_E = PEER_NK * PEER_NK
NEG = -1e30
BF16_LANES = 256

OFF_CX, OFF_CB, OFF_CC = 0, 512, 1024
OFF_GQ, OFF_GK, OFF_GV, OFF_GG = 1536, 1792, 2048, 2560
OFF_NQ = 3072
OFF_NKC, OFF_NVC, OFF_NKS, OFF_NVS, OFF_NKW, OFF_NVW = 4096, 4352, 4608, 4864, 5120, 5376
OFF_MISC = 5632
N_IN = 5760
ORIG_GA, ORIG_NQ, ORIG_GATE, ORIG_END = 3072, 3088, 5648, 5672

TM_PROJ = 512
TN_IN = 1152
TS_ELT = 512
TC_GLA = 512
TQ = 128
TK = 512
CMP_ROWS = 256
TM_PEER = 512
TE_PEER = 1024
TM_ROUTE = 256
VMEM_LIMIT = 56 * 1024 * 1024
PEER_EXPERT_FLAGS = None


def _cp(sem, flags=None):
    return pltpu.CompilerParams(dimension_semantics=sem, vmem_limit_bytes=VMEM_LIMIT, flags=flags)


def _gelu(x):
    return 0.5 * x * (1.0 + jnp.tanh(0.7978845608028654 * (x + 0.044715 * (x * x * x))))


def _split_bf16(x):
    hi = x.astype(bf16)
    lo = (x - hi.astype(f32)).astype(bf16)
    return hi, lo


def _group_rms(y, gain):
    ms = jnp.mean(y * y, axis=-1, keepdims=True)
    return y * lax.rsqrt(ms + EPS) * gain


def _norm_matmul_kernel(x_ref, g_ref, w_ref, *refs, emit_xn):
    if emit_xn:
        o_ref, xo_ref, xn_ref = refs
    else:
        o_ref, xn_ref = refs

    @pl.when(pl.program_id(1) == 0)
    def _():
        x = x_ref[...]
        ms = jnp.mean(x * x, axis=-1, keepdims=True)
        xn = (x * lax.rsqrt(ms + EPS) * g_ref[...]).astype(bf16)
        xn_ref[...] = xn
        if emit_xn:
            xo_ref[...] = xn

    o_ref[...] = jnp.dot(xn_ref[...], w_ref[...], preferred_element_type=f32)


def _norm_matmul(x, gain, w, tn, emit_xn=False):
    t, d = x.shape
    n = w.shape[1]
    tm = TM_PROJ
    out_shape = [jax.ShapeDtypeStruct((t, n), f32)]
    out_specs = [pl.BlockSpec((tm, tn), lambda i, j: (i, j))]
    if emit_xn:
        out_shape.append(jax.ShapeDtypeStruct((t, d), bf16))
        out_specs.append(pl.BlockSpec((tm, d), lambda i, j: (i, 0)))
    outs = pl.pallas_call(
        functools.partial(_norm_matmul_kernel, emit_xn=emit_xn),
        grid=(t // tm, n // tn),
        in_specs=[
            pl.BlockSpec((tm, d), lambda i, j: (i, 0)),
            pl.BlockSpec((1, d), lambda i, j: (0, 0)),
            pl.BlockSpec((d, tn), lambda i, j: (0, j)),
        ],
        out_specs=out_specs,
        out_shape=out_shape,
        scratch_shapes=[pltpu.VMEM((tm, d), bf16)],
        compiler_params=_cp(("parallel", "arbitrary")),
        name="norm_matmul",
    )(x, gain.reshape(1, d), w)
    return outs if emit_xn else outs[0]


def _conv_kernel(cx_ref, cb_ref, cc_ref, pcx_ref, pcc_ref, w_ref, g_ref, o_ref, z_ref, *, ts, tiles_per_seq):
    first = (pl.program_id(0) % tiles_per_seq) == 0
    zp = pcx_ref[...] * pcc_ref[...]
    z_ref[0:8, :] = jnp.where(first, 0.0, zp)
    z_ref[8:8 + ts, :] = cx_ref[...] * cc_ref[...]
    y = (z_ref[8:8 + ts, :] * w_ref[0:1, :]
         + z_ref[7:7 + ts, :] * w_ref[1:2, :]
         + z_ref[6:6 + ts, :] * w_ref[2:3, :])
    y = cb_ref[...] * y
    for c in range(CONV_W // HEAD):
        sl = slice(c * HEAD, (c + 1) * HEAD)
        o_ref[:, sl] = _group_rms(y[:, sl], g_ref[:, sl]).astype(bf16)


def _conv_mixer(p, conv_w, gain, seq):
    t = p.shape[0]
    ts = TS_ELT
    w_t = conv_w.astype(f32).T
    cur = lambda off: pl.BlockSpec((ts, CONV_W), lambda i, off=off: (i, off // CONV_W))
    prev = lambda off: pl.BlockSpec((8, CONV_W), lambda i, off=off: (jnp.maximum(i * (ts // 8) - 1, 0), off // CONV_W))
    return pl.pallas_call(
        functools.partial(_conv_kernel, ts=ts, tiles_per_seq=seq // ts),
        grid=(t // ts,),
        in_specs=[cur(OFF_CX), cur(OFF_CB), cur(OFF_CC), prev(OFF_CX), prev(OFF_CC),
                  pl.BlockSpec((CONV_K, CONV_W), lambda i: (0, 0)),
                  pl.BlockSpec((1, CONV_W), lambda i: (0, 0))],
        out_specs=pl.BlockSpec((ts, CONV_W), lambda i: (i, 0)),
        out_shape=jax.ShapeDtypeStruct((t, CONV_W), bf16),
        scratch_shapes=[pltpu.VMEM((ts + 8, CONV_W), f32)],
        compiler_params=_cp(("parallel",)),
        name="conv_mixer",
    )(p, p, p, p, p, w_t, gain.reshape(1, CONV_W))


def _gla_kernel(q_ref, k_ref, v_ref, gg_ref, misc_ref, wa_ref, ba_ref, gn_ref, o_ref,
                st_ref, la_ref, y_ref, *, tc):
    @pl.when(pl.program_id(1) == 0)
    def _():
        st_ref[...] = jnp.zeros_like(st_ref)

    a_hi, a_lo = _split_bf16(misc_ref[:, 0:GLA_LR])
    w_hi, w_lo = _split_bf16(wa_ref[...])
    z = (jnp.dot(a_hi, w_hi, preferred_element_type=f32)
         + jnp.dot(a_hi, w_lo, preferred_element_type=f32)
         + jnp.dot(a_lo, w_hi, preferred_element_type=f32)) + ba_ref[...]
    la_ref[...] = (jnp.minimum(z, 0.0) - jnp.log(1.0 + jnp.exp(-jnp.abs(z)))) * (1.0 / GLA_TAU)

    c = GLA_CHUNK
    row = lax.broadcasted_iota(jnp.int32, (c, c), 0)
    col = lax.broadcasted_iota(jnp.int32, (c, c), 1)
    causal = row >= col
    tril = causal.astype(bf16)
    scale = GLA_DK ** -0.5

    def chunk(ci, carry):
        r0 = pl.multiple_of(ci * c, c)
        la = la_ref[pl.ds(r0, c), :]
        la_hi, la_lo = _split_bf16(la)
        la_lo2 = (la - la_hi.astype(f32) - la_lo.astype(f32)).astype(bf16)
        bc = (jnp.dot(tril, la_hi, preferred_element_type=f32)
              + jnp.dot(tril, la_lo, preferred_element_type=f32)
              + jnp.dot(tril, la_lo2, preferred_element_type=f32))
        bl = bc[c - 1:c, :]
        qc = q_ref[pl.ds(r0, c), :]
        kc = k_ref[pl.ds(r0, c), :]
        vc = v_ref[pl.ds(r0, c), :].astype(bf16)
        q_dec = (qc * scale * jnp.exp(bc)).astype(bf16)
        k_inv = (kc * jnp.exp(-bc)).astype(bf16)
        k_end = (kc * jnp.exp(bl - bc)).astype(bf16)
        decay = jnp.exp(bl)
        for h in range(GLA_H):
            ks = slice(h * GLA_DK, (h + 1) * GLA_DK)
            vs = slice(h * GLA_DV, (h + 1) * GLA_DV)
            att = lax.dot_general(q_dec[:, ks], k_inv[:, ks], (((1,), (1,)), ((), ())),
                                  preferred_element_type=f32)
            att = jnp.where(causal, att, 0.0).astype(bf16)
            o = jnp.dot(att, vc[:, vs], preferred_element_type=f32)
            st = st_ref[h]
            o = o + lax.dot_general(q_dec[:, ks], st.astype(bf16), (((1,), (1,)), ((), ())),
                                    preferred_element_type=f32)
            y_ref[pl.ds(r0, c), vs] = o
            upd = lax.dot_general(vc[:, vs], k_end[:, ks], (((0,), (0,)), ((), ())),
                                  preferred_element_type=f32)
            st_ref[h] = st * decay[:, ks] + upd
        return carry

    lax.fori_loop(0, tc // c, chunk, 0)

    g = gg_ref[...]
    sw = g * (1.0 / (1.0 + jnp.exp(-g)))
    for h in range(GLA_H):
        vs = slice(h * GLA_DV, (h + 1) * GLA_DV)
        o_ref[:, vs] = (_group_rms(y_ref[:, vs], gn_ref[:, vs]) * sw[:, vs]).astype(bf16)


def _gla_mixer(p, w_a2, b_a, gain, batch, seq):
    t = p.shape[0]
    tc = TC_GLA
    nst = seq // tc
    blk = lambda width, off: pl.BlockSpec((tc, width), lambda b, s, off=off, width=width: (b * nst + s, off // width))
    const = lambda shape: pl.BlockSpec(shape, lambda b, s: (0,) * len(shape))
    return pl.pallas_call(
        functools.partial(_gla_kernel, tc=tc),
        grid=(batch, nst),
        in_specs=[blk(256, OFF_GQ), blk(256, OFF_GK), blk(512, OFF_GV), blk(512, OFF_GG), blk(128, OFF_MISC),
                  const((GLA_LR, GLA_H * GLA_DK)), const((1, GLA_H * GLA_DK)), const((1, GLA_H * GLA_DV))],
        out_specs=pl.BlockSpec((tc, GLA_H * GLA_DV), lambda b, s: (b * nst + s, 0)),
        out_shape=jax.ShapeDtypeStruct((t, GLA_H * GLA_DV), bf16),
        scratch_shapes=[pltpu.VMEM((GLA_H, GLA_DV, GLA_DK), f32),
                        pltpu.VMEM((tc, GLA_H * GLA_DK), f32),
                        pltpu.VMEM((tc, GLA_H * GLA_DV), f32)],
        compiler_params=_cp(("parallel", "arbitrary")),
        name="gla_mixer",
    )(p, p, p, p, p, w_a2.astype(f32), b_a.astype(f32).reshape(1, -1), gain.reshape(1, -1))


def _compress_kernel(t_ref, pos_ref, w1_ref, w2_ref, gn_ref, o_ref, b_scr, *, n_piece, apply_norm):
    half = CMP_STRIDE
    acc_a = jnp.zeros((n_piece, NSA_D), f32)
    acc_b = jnp.zeros((n_piece, NSA_D), f32)
    for r in range(half):
        rows = t_ref[pl.ds(r, n_piece, stride=CMP_STRIDE), :]
        xa = (rows + pos_ref[r:r + 1, :]).astype(bf16)
        acc_a = acc_a + jnp.dot(xa, w1_ref[r], preferred_element_type=f32)
        xb = (rows + pos_ref[half + r:half + r + 1, :]).astype(bf16)
        acc_b = acc_b + jnp.dot(xb, w1_ref[half + r], preferred_element_type=f32)
    b_scr[0:n_piece, :] = acc_b
    b_scr[n_piece:n_piece + 8, :] = jnp.zeros((8, NSA_D), f32)
    hid = acc_a + b_scr[1:n_piece + 1, :]
    out = jnp.dot(_gelu(hid).astype(bf16), w2_ref[...], preferred_element_type=f32)
    if apply_norm:
        out = _group_rms(out, gn_ref[...])
    o_ref[0] = out.astype(bf16)


def _compress(p, off, pos, w1, w2, gain, batch, seq, apply_norm):
    n_piece = seq // CMP_STRIDE
    return pl.pallas_call(
        functools.partial(_compress_kernel, n_piece=n_piece, apply_norm=apply_norm),
        grid=(batch, NSA_G),
        in_specs=[pl.BlockSpec((seq, NSA_D), lambda b, g: (b, off // NSA_D + g)),
                  pl.BlockSpec((CMP_LEN, NSA_D), lambda b, g: (0, 0)),
                  pl.BlockSpec((CMP_LEN, NSA_D, NSA_D), lambda b, g: (0, 0, 0)),
                  pl.BlockSpec((NSA_D, NSA_D), lambda b, g: (0, 0)),
                  pl.BlockSpec((1, NSA_D), lambda b, g: (0, 0))],
        out_specs=pl.BlockSpec((1, n_piece, NSA_D), lambda b, g: (b * NSA_G + g, 0, 0)),
        out_shape=jax.ShapeDtypeStruct((batch * NSA_G, n_piece, NSA_D), bf16),
        scratch_shapes=[pltpu.VMEM((n_piece + 8, NSA_D), f32)],
        compiler_params=_cp(("parallel", "parallel")),
        name="nsa_compress",
    )(p, pos.astype(f32), w1.astype(bf16).reshape(CMP_LEN, NSA_D, NSA_D), w2.astype(bf16),
      gain.astype(f32).reshape(1, NSA_D))


def _kvprep_kernel(ks_ref, vs_ref, kw_ref, vw_ref, gs_ref, gw_ref, oks_ref, ovs_ref, okw_ref, ovw_ref):
    lead = pl.program_id(2) == 0
    oks_ref[0] = _group_rms(ks_ref[...], gs_ref[...]).astype(bf16)
    ovs_ref[0] = vs_ref[...].astype(bf16)
    okw_ref[0] = jnp.where(lead, 0.0, _group_rms(kw_ref[...], gw_ref[...])).astype(bf16)
    ovw_ref[0] = jnp.where(lead, 0.0, vw_ref[...]).astype(bf16)


def _kv_prep(p, k_norm, batch, seq):
    ts = WINDOW
    nst = seq // ts
    src = lambda s: jnp.maximum(s - 1, 0)
    blk = lambda off: pl.BlockSpec((ts, NSA_D), lambda b, g, s, off=off: (b * nst + src(s), off // NSA_D + g))
    gspec = pl.BlockSpec((1, NSA_D), lambda b, g, s: (0, 0))
    ospec = pl.BlockSpec((1, ts, NSA_D), lambda b, g, s: (b * NSA_G + g, src(s), 0))
    wspec = pl.BlockSpec((1, ts, NSA_D), lambda b, g, s: (b * NSA_G + g, s, 0))
    oshape = jax.ShapeDtypeStruct((batch * NSA_G, seq, NSA_D), bf16)
    wshape = jax.ShapeDtypeStruct((batch * NSA_G, seq + WINDOW, NSA_D), bf16)
    return pl.pallas_call(
        _kvprep_kernel,
        grid=(batch, NSA_G, nst + 1),
        in_specs=[blk(OFF_NKS), blk(OFF_NVS), blk(OFF_NKW), blk(OFF_NVW), gspec, gspec],
        out_specs=[ospec, ospec, wspec, wspec],
        out_shape=[oshape, oshape, wshape, wshape],
        compiler_params=_cp(("parallel", "parallel", "arbitrary")),
        name="nsa_kv_prep",
    )(p, p, p, p, k_norm[1].astype(f32).reshape(1, NSA_D), k_norm[2].astype(f32).reshape(1, NSA_D))


def _stack_q(q_ref, qn_ref):
    parts = []
    for r in range(NSA_R):
        qn = _group_rms(q_ref[:, r * NSA_D:(r + 1) * NSA_D], qn_ref[...])
        parts.append((qn * (NSA_D ** -0.5)).astype(bf16))
    return jnp.concatenate(parts, axis=0)


def _head_slope(g, r):
    return jnp.where(g == 0, 1.0, 2.0 ** -NSA_R).astype(f32) * (2.0 ** -(r + 1))


def _nsa_cmp_kernel(q_ref, qn_ref, kcc_ref, vcc_ref, ovt_ref, oc_ref, selt_ref, s_scr, p_scr, psum_scr,
                    *, tq, n_cmp, n_sel):
    g = pl.program_id(1)
    q0 = pl.program_id(2) * tq
    qs = _stack_q(q_ref, qn_ref)
    s_scr[...] = lax.dot_general(kcc_ref[0], qs, (((1,), (1,)), ((), ())), preferred_element_type=f32)
    rc = CMP_ROWS
    n_chunk = n_cmp // rc
    n_vis = jnp.clip((q0 + tq - CMP_LEN) // CMP_STRIDE + 1, 0, n_cmp)
    n_act = (n_vis + rc - 1) // rc
    tcol = lax.broadcasted_iota(jnp.int32, (rc, tq), 1)
    nrow = lax.broadcasted_iota(jnp.int32, (rc, tq), 0)

    def rel_end(c):
        return (c * rc + nrow) * CMP_STRIDE + (CMP_LEN - 1 - q0)

    for r in range(NSA_R):
        cs = slice(r * tq, (r + 1) * tq)
        slope = _head_slope(g, r)

        def biased(c, m):
            r0 = pl.multiple_of(c * rc, rc)
            rel = rel_end(c)
            sb = jnp.where(rel <= tcol, s_scr[pl.ds(r0, rc), cs] + slope * rel.astype(f32), NEG)
            s_scr[pl.ds(r0, rc), cs] = sb
            return jnp.maximum(m, jnp.max(sb, axis=0, keepdims=True))

        m = lax.fori_loop(0, n_act, biased, jnp.full((1, tq), NEG, f32))
        m = jnp.maximum(m, 0.1 * NEG)

        def expo(c, l):
            r0 = pl.multiple_of(c * rc, rc)
            e = jnp.exp(s_scr[pl.ds(r0, rc), cs] - m)
            s_scr[pl.ds(r0, rc), cs] = e
            return l + jnp.sum(e, axis=0, keepdims=True)

        l = lax.fori_loop(0, n_act, expo, jnp.zeros((1, tq), f32))
        inv = 1.0 / jnp.maximum(l, 1e-30)

        def normalise(c, carry):
            r0 = pl.multiple_of(c * rc, rc)
            pn = s_scr[pl.ds(r0, rc), cs] * inv
            p_scr[pl.ds(r0, rc), cs] = pn.astype(bf16)
            if r == 0:
                psum_scr[pl.ds(r0, rc), :] = pn
            else:
                psum_scr[pl.ds(r0, rc), :] += pn
            return carry

        lax.fori_loop(0, n_act, normalise, 0)

    def clear(c, carry):
        r0 = pl.multiple_of(c * rc, rc)
        p_scr[pl.ds(r0, rc), :] = jnp.zeros((rc, NSA_R * tq), bf16)
        psum_scr[pl.ds(r0, rc), :] = jnp.zeros((rc, tq), f32)
        return carry

    lax.fori_loop(n_act, n_chunk, clear, 0)

    vcc = vcc_ref[0]
    for r in range(NSA_R):
        oc_ref[:, r * NSA_D:(r + 1) * NSA_D] = lax.dot_general(
            p_scr[:, r * tq:(r + 1) * tq], vcc, (((0,), (0,)), ((), ())), preferred_element_type=f32)

    p_hi, p_lo = _split_bf16(psum_scr[...])
    imp = (jnp.dot(ovt_ref[...], p_hi, preferred_element_type=f32)
           + jnp.dot(ovt_ref[...], p_lo, preferred_element_type=f32))
    blk = lax.broadcasted_iota(jnp.int32, (n_sel, tq), 0)
    tblk = (q0 + lax.broadcasted_iota(jnp.int32, (n_sel, tq), 1)) // SEL_BLOCK
    lag = tblk - blk
    score = jnp.where(blk == 0, BIG, jnp.where(lag < SEL_LOCAL, BIG, imp))
    score = jnp.where(lag >= 0, score, -1.0)
    sel = jnp.zeros((n_sel, tq), f32)
    for _ in range(SEL_TOPK):
        m = jnp.max(score, axis=0, keepdims=True)
        first = jnp.min(jnp.where(score == m, blk, n_sel), axis=0, keepdims=True)
        hit = blk == first
        sel = jnp.where(hit, 1.0, sel)
        score = jnp.where(hit, -2.0, score)
    selt_ref[0] = sel.astype(bf16)


def _nsa_cmp(p, q_norm, kcc, vcc, batch, seq):
    tq = TQ
    nq = seq // tq
    n_cmp = seq // CMP_STRIDE
    n_sel = seq // SEL_BLOCK
    ci = np.arange(n_cmp)[None, :] * CMP_STRIDE
    sj = np.arange(n_sel)[:, None] * SEL_BLOCK
    ovt = np.clip(np.minimum(ci + CMP_LEN, sj + SEL_BLOCK) - np.maximum(ci, sj), 0, None) / CMP_LEN
    ovt = jnp.asarray(ovt, dtype=bf16)
    t = batch * seq
    hw = NSA_R * NSA_D
    return pl.pallas_call(
        functools.partial(_nsa_cmp_kernel, tq=tq, n_cmp=n_cmp, n_sel=n_sel),
        grid=(batch, NSA_G, nq),
        in_specs=[pl.BlockSpec((tq, hw), lambda b, g, i: (b * nq + i, OFF_NQ // hw + g)),
                  pl.BlockSpec((1, NSA_D), lambda b, g, i: (0, 0)),
                  pl.BlockSpec((1, n_cmp, NSA_D), lambda b, g, i: (b * NSA_G + g, 0, 0)),
                  pl.BlockSpec((1, n_cmp, NSA_D), lambda b, g, i: (b * NSA_G + g, 0, 0)),
                  pl.BlockSpec((n_sel, n_cmp), lambda b, g, i: (0, 0))],
        out_specs=[pl.BlockSpec((tq, hw), lambda b, g, i: (b * nq + i, g)),
                   pl.BlockSpec((1, n_sel, tq), lambda b, g, i: (b * NSA_G + g, 0, i))],
        out_shape=[jax.ShapeDtypeStruct((t, NSA_G * hw), f32),
                   jax.ShapeDtypeStruct((batch * NSA_G, n_sel, seq), bf16)],
        scratch_shapes=[pltpu.VMEM((n_cmp, NSA_R * tq), f32), pltpu.VMEM((n_cmp, NSA_R * tq), bf16),
                        pltpu.VMEM((n_cmp, tq), f32)],
        compiler_params=_cp(("parallel", "parallel", "parallel")),
        name="nsa_cmp_select",
    )(p, q_norm.astype(f32).reshape(1, NSA_D), kcc, vcc, ovt)


def _nsa_attn_kernel(flag_ref, q_ref, qn_ref, misc_ref, oc_ref, selt_ref, ks_ref, vs_ref, kw_ref, vw_ref, gn_ref,
                     o_ref, m_scr, l_scr, acc_scr, *, tq, tk, nkt):
    b = pl.program_id(0)
    g = pl.program_id(1)
    qi = pl.program_id(2)
    nq = pl.num_programs(2)
    q0 = qi * tq
    rows = NSA_R * tq
    qs = _stack_q(q_ref, qn_ref)
    nt_dims = (((1,), (1,)), ((), ()))
    trow = lax.broadcasted_iota(jnp.int32, (tq, 1), 0)

    bpt = tk // SEL_BLOCK
    eu = lax.broadcasted_iota(jnp.int32, (bpt, tk), 0)
    ec = lax.broadcasted_iota(jnp.int32, (bpt, tk), 1) // SEL_BLOCK
    expand = jnp.where(eu == ec, -NEG, 0.0).astype(bf16)
    koff = lax.broadcasted_iota(jnp.int32, (1, tk), 1)
    kt_last = (q0 + tq - 1) // tk

    m_scr[...] = jnp.full((rows, 1), NEG, f32)
    l_scr[...] = jnp.zeros((rows, 1), f32)
    acc_scr[...] = jnp.zeros((rows, NSA_D), f32)

    def sel_step(kt, causal):
        k0 = pl.multiple_of(kt * tk, tk)
        kb = ks_ref[0, pl.ds(k0, tk), :]
        vb = vs_ref[0, pl.ds(k0, tk), :]
        s = lax.dot_general(qs, kb, nt_dims, preferred_element_type=f32)
        picked = selt_ref[0, pl.ds(pl.multiple_of(kt * bpt, bpt), bpt), :]
        bias = lax.dot_general(picked - 1.0, expand, (((0,), (0,)), ((), ())),
                               preferred_element_type=f32)
        rel = koff + (k0 - q0)
        if causal:
            bias = bias + jnp.where(rel <= trow, 0.0, NEG)
        relf = rel.astype(f32)
        for r in range(NSA_R):
            rs = slice(r * tq, (r + 1) * tq)
            sr = s[rs, :] + (_head_slope(g, r) * relf) + bias
            m_old = m_scr[rs, :]
            m_new = jnp.maximum(m_old, jnp.max(sr, axis=-1, keepdims=True))
            alpha = jnp.exp(m_old - m_new)
            pr = jnp.exp(sr - m_new)
            l_scr[rs, :] = alpha * l_scr[rs, :] + jnp.sum(pr, axis=-1, keepdims=True)
            acc_scr[rs, :] = alpha * acc_scr[rs, :] + jnp.dot(pr.astype(bf16), vb, preferred_element_type=f32)
            m_scr[rs, :] = m_new

    fbase = ((b * NSA_G + g) * nq + qi) * nkt

    def sel_loop(kt, carry):
        @pl.when(flag_ref[fbase + kt] > 0)
        def _():
            sel_step(kt, False)
        return carry

    lax.fori_loop(0, kt_last, sel_loop, 0)
    sel_step(kt_last, True)
    o_s = acc_scr[...] * (1.0 / jnp.maximum(l_scr[...], 1e-30))

    wk = WINDOW + tq
    kwin = kw_ref[0, pl.ds(pl.multiple_of(q0, tq), wk), :]
    vwin = vw_ref[0, pl.ds(pl.multiple_of(q0, tq), wk), :]
    sw = lax.dot_general(qs, kwin, nt_dims, preferred_element_type=f32)
    jrow = lax.broadcasted_iota(jnp.int32, (1, wk), 1)
    band = jnp.where(jrow > trow, jnp.where(jrow <= trow + WINDOW, 0.0, NEG), NEG)
    before_start = jnp.where(jrow >= WINDOW - q0, 0.0, NEG)
    jf = jrow.astype(f32)
    o_w = []
    for r in range(NSA_R):
        sr = sw[r * tq:(r + 1) * tq, :] + (_head_slope(g, r) * jf + before_start) + band
        m = jnp.max(sr, axis=-1, keepdims=True)
        pr = jnp.exp(sr - m)
        den = jnp.sum(pr, axis=-1, keepdims=True)
        o_w.append(jnp.dot(pr.astype(bf16), vwin, preferred_element_type=f32) * (1.0 / den))

    for r in range(NSA_R):
        lanes = []
        for c in range(3):
            l0 = GLA_LR + r * 3 + c
            l1 = l0 + NSA_R * 3
            lanes.append(jnp.where(g == 0, misc_ref[:, l0:l0 + 1], misc_ref[:, l1:l1 + 1]))
        gc, gs, gw = [1.0 / (1.0 + jnp.exp(-x)) for x in lanes]
        hs = slice(r * NSA_D, (r + 1) * NSA_D)
        rs = slice(r * tq, (r + 1) * tq)
        y = gc * oc_ref[:, hs] + gs * o_s[rs, :] + gw * o_w[r]
        o_ref[:, hs] = _group_rms(y, gn_ref[:, hs]).astype(bf16)


def _nsa_attn(p, q_norm, o_c, selt, flags, ks, vs, kw, vw, gain, batch, seq):
    tq, tk = TQ, TK
    nq = seq // tq
    nkt = seq // tk
    n_sel = seq // SEL_BLOCK
    t = batch * seq
    hw = NSA_R * NSA_D
    slab = lambda rows: pl.BlockSpec((1, rows, NSA_D), lambda b, g, i, f: (b * NSA_G + g, 0, 0),
                                     pipeline_mode=pl.Buffered(1))
    grid_spec = pltpu.PrefetchScalarGridSpec(
        num_scalar_prefetch=1,
        grid=(batch, NSA_G, nq),
        in_specs=[pl.BlockSpec((tq, hw), lambda b, g, i, f: (b * nq + i, OFF_NQ // hw + g)),
                  pl.BlockSpec((1, NSA_D), lambda b, g, i, f: (0, 0)),
                  pl.BlockSpec((tq, 128), lambda b, g, i, f: (b * nq + i, OFF_MISC // 128)),
                  pl.BlockSpec((tq, hw), lambda b, g, i, f: (b * nq + i, g)),
                  pl.BlockSpec((1, n_sel, tq), lambda b, g, i, f: (b * NSA_G + g, 0, i)),
                  slab(seq), slab(seq), slab(seq + WINDOW), slab(seq + WINDOW),
                  pl.BlockSpec((1, hw), lambda b, g, i, f: (0, g))],
        out_specs=pl.BlockSpec((tq, hw), lambda b, g, i, f: (b * nq + i, g)),
        scratch_shapes=[pltpu.VMEM((NSA_R * tq, 1), f32), pltpu.VMEM((NSA_R * tq, 1), f32),
                        pltpu.VMEM((NSA_R * tq, NSA_D), f32)],
    )
    return pl.pallas_call(
        functools.partial(_nsa_attn_kernel, tq=tq, tk=tk, nkt=nkt),
        grid_spec=grid_spec,
        out_shape=jax.ShapeDtypeStruct((t, NSA_G * hw), bf16),
        compiler_params=_cp(("parallel", "parallel", "arbitrary")),
        name="nsa_attention",
    )(flags, p, q_norm.astype(f32).reshape(1, NSA_D), p, o_c, selt, ks, vs, kw, vw, gain.reshape(1, -1))


def _out_proj_kernel(a_ref, b_ref, c_ref, wa_ref, wb_ref, wc_ref, h_ref, o_ref):
    o_ref[...] = (h_ref[...]
                  + jnp.dot(a_ref[...], wa_ref[...], preferred_element_type=f32)
                  + jnp.dot(b_ref[...], wb_ref[...], preferred_element_type=f32)
                  + jnp.dot(c_ref[...], wc_ref[...], preferred_element_type=f32))


def _out_proj(ya, yb, yc, w_out, h):
    t, d = h.shape
    tm, tn = TM_PROJ, 1024
    w = w_out.astype(bf16)
    ka, kb, kc = ya.shape[1], yb.shape[1], yc.shape[1]
    return pl.pallas_call(
        _out_proj_kernel,
        grid=(t // tm, d // tn),
        in_specs=[pl.BlockSpec((tm, ka), lambda i, j: (i, 0)),
                  pl.BlockSpec((tm, kb), lambda i, j: (i, 0)),
                  pl.BlockSpec((tm, kc), lambda i, j: (i, 0)),
                  pl.BlockSpec((ka, tn), lambda i, j: (0, j)),
                  pl.BlockSpec((kb, tn), lambda i, j: (1, j)),
                  pl.BlockSpec((kc, tn), lambda i, j: (1, j)),
                  pl.BlockSpec((tm, tn), lambda i, j: (i, j))],
        out_specs=pl.BlockSpec((tm, tn), lambda i, j: (i, j)),
        out_shape=jax.ShapeDtypeStruct((t, d), f32),
        compiler_params=_cp(("parallel", "parallel")),
        name="out_proj",
    )(ya, yb, yc, w, w, w, h)


def _top_k_columns(v, k):
    rows = v.shape[0]
    ridx = lax.broadcasted_iota(jnp.int32, v.shape, 0)
    vals, firsts = [], []
    rank = jnp.full(v.shape, float(k), f32)
    for kk in range(k):
        m = jnp.max(v, axis=0, keepdims=True)
        first = jnp.min(jnp.where(v == m, ridx, rows), axis=0, keepdims=True)
        hit = ridx == first
        rank = jnp.where(hit, float(kk), rank)
        v = jnp.where(hit, -jnp.inf, v)
        vals.append(m)
        firsts.append(first)
    return vals, rank, firsts


def _peer_route_kernel(q_ref, key_ref, a_ref, cnt_ref, rank2_ref, e2_ref):
    tm = q_ref.shape[0]
    for h in range(PEER_H):
        sc, tops, ranks = [], [], []
        for half in range(2):
            c0 = (h * 2 + half) * PEER_NK
            qh = q_ref[:, c0:c0 + PEER_NK].astype(bf16)
            s = lax.dot_general(key_ref[h * 2 + half], qh, (((1,), (1,)), ((), ())),
                                preferred_element_type=f32)
            vals, rank, _ = _top_k_columns(s, PEER_TOPK)
            sc.append(s)
            tops.append(vals)
            ranks.append(rank)
        t2 = jnp.concatenate(tops[1], axis=0)
        cand = jnp.concatenate([tops[0][a] + t2 for a in range(PEER_TOPK)], axis=0)
        best, _, pos = _top_k_columns(cand, PEER_TOPK)
        mass = jnp.zeros_like(best[0])
        for bk in best:
            mass = mass + jnp.exp(bk - best[0])
        aidx = lax.broadcasted_iota(jnp.int32, (PEER_TOPK, tm), 0)
        per_rank = jnp.zeros((PEER_TOPK, tm), f32)
        for p in pos:
            per_rank = per_rank + jnp.where(aidx == p // PEER_TOPK, 1.0, 0.0)
        cnt = jnp.zeros((PEER_NK, tm), f32)
        for a in range(PEER_TOPK):
            cnt = cnt + jnp.where(ranks[0] == float(a), per_rank[a:a + 1, :], 0.0)
        hs = slice(h * PEER_NK, (h + 1) * PEER_NK)
        a_ref[hs, :] = (jnp.exp(sc[0] - tops[0][0]) * (0.5 / mass)).astype(bf16)
        cnt_ref[hs, :] = cnt.astype(bf16)
        rank2_ref[hs, :] = ranks[1].astype(bf16)
        e2_ref[hs, :] = jnp.exp(sc[1] - tops[1][0]).astype(bf16)


def _peer_route(qry, keys):
    t = qry.shape[0]
    tm = TM_ROUTE
    narrow = jax.ShapeDtypeStruct((PEER_H * PEER_NK, t), bf16)
    nspec = pl.BlockSpec((PEER_H * PEER_NK, tm), lambda i: (0, i))
    return pl.pallas_call(
        _peer_route_kernel,
        grid=(t // tm,),
        in_specs=[pl.BlockSpec((tm, PEER_H * 2 * PEER_NK), lambda i: (i, 0)),
                  pl.BlockSpec((PEER_H * 2, PEER_NK, PEER_NK), lambda i: (0, 0, 0))],
        out_specs=[nspec] * 4,
        out_shape=[narrow] * 4,
        compiler_params=_cp(("parallel",)),
        name="peer_route",
    )(qry, keys.astype(bf16).reshape(PEER_H * 2, PEER_NK, PEER_NK))


def _peer_expert_kernel(xn_ref, u_ref, vt_ref, a_ref, cnt_ref, rank2_ref, e2_ref, h_ref, o_ref,
                        acc_ref, act_ref, gt_ref, *, tm, te):
    j = pl.program_id(1)

    @pl.when(j == 0)
    def _():
        acc_ref[...] = jnp.zeros_like(acc_ref)

    n_first = te // PEER_NK
    i0 = pl.multiple_of(j * n_first, n_first)

    x = lax.dot_general(u_ref[...], xn_ref[...], (((1,), (1,)), ((), ())), preferred_element_type=f32)
    act_ref[...] = (x * (1.0 + jnp.tanh(x * (0.7978845608028654 + 0.035677408136300125 * (x * x))))).astype(bf16)

    zero = jnp.zeros((PEER_NK, BF16_LANES), bf16)
    for c in range(tm // BF16_LANES):
        ls = slice(c * BF16_LANES, (c + 1) * BF16_LANES)
        for u in range(n_first):
            rs = slice(u * PEER_NK, (u + 1) * PEER_NK)
            w = zero
            for h in range(PEER_H):
                hs = slice(h * PEER_NK, (h + 1) * PEER_NK)
                r0 = pl.multiple_of(h * PEER_NK + i0, n_first)
                a = a_ref[pl.ds(r0, n_first), ls][u:u + 1, :]
                cnt = cnt_ref[pl.ds(r0, n_first), ls][u:u + 1, :]
                keep = rank2_ref[hs, ls] < cnt
                w = w + jnp.where(keep, e2_ref[hs, ls], zero) * a
            gt_ref[rs, ls] = w * act_ref[rs, ls]

    acc_ref[...] += jnp.dot(vt_ref[...], gt_ref[...], preferred_element_type=f32)

    @pl.when(j == pl.num_programs(1) - 1)
    def _():
        o_ref[...] = h_ref[...] + acc_ref[...].T


def _peer_experts(xn, u, vt, a, cnt, rank2, e2, h):
    t, d = h.shape
    tm, te = TM_PEER, TE_PEER
    once = pl.Buffered(1)
    nspec = pl.BlockSpec((PEER_H * PEER_NK, tm), lambda i, j: (0, i), pipeline_mode=once)
    return pl.pallas_call(
        functools.partial(_peer_expert_kernel, tm=tm, te=te),
        grid=(t // tm, PEER_E // te),
        in_specs=[pl.BlockSpec((tm, d), lambda i, j: (i, 0), pipeline_mode=once),
                  pl.BlockSpec((te, d), lambda i, j: (j, 0)),
                  pl.BlockSpec((d, te), lambda i, j: (0, j)),
                  nspec, nspec, nspec, nspec,
                  pl.BlockSpec((tm, d), lambda i, j: (i, 0), pipeline_mode=once)],
        out_specs=pl.BlockSpec((tm, d), lambda i, j: (i, 0)),
        out_shape=jax.ShapeDtypeStruct((t, d), f32),
        scratch_shapes=[pltpu.VMEM((d, tm), f32), pltpu.VMEM((te, tm), bf16), pltpu.VMEM((te, tm), bf16)],
        compiler_params=_cp(("parallel", "arbitrary"), PEER_EXPERT_FLAGS),
        name="peer_experts",
    )(xn, u, vt, a, cnt, rank2, e2, h)


def _permute_w_in(w):
    pad = jnp.zeros((w.shape[0], N_IN - ORIG_END), w.dtype)
    return jnp.concatenate([w[:, :ORIG_GA], w[:, ORIG_NQ:ORIG_GATE], w[:, ORIG_GA:ORIG_NQ],
                            w[:, ORIG_GATE:ORIG_END], pad], axis=1).astype(bf16)


def _tile_flags(selt, seq):
    bg = selt.shape[0]
    any_sel = selt.reshape(bg, seq // TK, TK // SEL_BLOCK, seq // TQ, TQ).max(axis=(2, 4))
    return (jnp.swapaxes(any_sel, 1, 2) > 0.5).astype(jnp.int32).reshape(-1)


def _mixers(h, norm_mix, w_in, conv_w, gla_w_a2, gla_b_a, q_norm, k_norm, cmp_pos, cmp_w1, cmp_w2, out_norm,
            batch, seq):
    p = _norm_matmul(h, norm_mix.astype(f32), _permute_w_in(w_in), TN_IN)
    gain = out_norm.astype(f32)
    y_conv = _conv_mixer(p, conv_w, gain[:CONV_W], seq)
    y_gla = _gla_mixer(p, gla_w_a2, gla_b_a, gain[CONV_W:2 * CONV_W], batch, seq)
    kcc = _compress(p, OFF_NKC, cmp_pos[0], cmp_w1[0], cmp_w2[0], k_norm[0], batch, seq, True)
    vcc = _compress(p, OFF_NVC, cmp_pos[1], cmp_w1[1], cmp_w2[1], k_norm[0], batch, seq, False)
    ks, vs, kw, vw = _kv_prep(p, k_norm, batch, seq)
    o_c, selt = _nsa_cmp(p, q_norm, kcc, vcc, batch, seq)
    y_nsa = _nsa_attn(p, q_norm, o_c, selt, _tile_flags(selt, seq), ks, vs, kw, vw, gain[2 * CONV_W:], batch, seq)
    return y_conv, y_gla, y_nsa


def _layer(h, norm_mix, w_in, conv_w, gla_w_a2, gla_b_a, q_norm, k_norm, cmp_pos, cmp_w1, cmp_w2, out_norm, w_out,
           norm_ffn, peer_w_q, peer_keys, peer_u, peer_v, batch, seq):
    y_conv, y_gla, y_nsa = _mixers(h, norm_mix, w_in, conv_w, gla_w_a2, gla_b_a, q_norm, k_norm, cmp_pos, cmp_w1,
                                   cmp_w2, out_norm, batch, seq)
    h = _out_proj(y_conv, y_gla, y_nsa, w_out, h)
    qry, xn = _norm_matmul(h, norm_ffn.astype(f32), peer_w_q.astype(bf16), 1024, emit_xn=True)
    a, cnt, rank2, e2 = _peer_route(qry, peer_keys)
    return _peer_experts(xn, peer_u.astype(bf16), peer_v.astype(bf16).T, a, cnt, rank2, e2, h)


def kernel(x, norm_mix, w_in, conv_w, gla_w_a2, gla_b_a, nsa_q_norm, nsa_k_norm, nsa_cmp_pos, nsa_cmp_w1,
           nsa_cmp_w2, out_norm, w_out, norm_ffn, peer_w_q, peer_keys, peer_u, peer_v):
    batch, seq, d = x.shape
    h = x.reshape(batch * seq, d)
    for l in range(w_in.shape[0]):
        h = _layer(h, norm_mix[l], w_in[l], conv_w[l], gla_w_a2[l], gla_b_a[l], nsa_q_norm[l], nsa_k_norm[l],
                   nsa_cmp_pos[l], nsa_cmp_w1[l], nsa_cmp_w2[l], out_norm[l], w_out[l], norm_ffn[l], peer_w_q[l],
                   peer_keys[l], peer_u[l], peer_v[l], batch, seq)
    return h.reshape(batch, seq, d)
```

```python
import functools
import math

import numpy as np
import jax
import jax.numpy as jnp
from jax import lax
from jax.experimental import pallas as pl
from jax.experimental.pallas import tpu as pltpu

f32 = jnp.float32
bf16 = jnp.bfloat16

D_MODEL = 2048
EPS = 1e-6
HEAD = 128
CONV_W = 512
CONV_K = 3
GLA_H, GLA_DK, GLA_DV = 4, 64, 128
GLA_LR = 16
GLA_TAU = 16.0
GLA_CHUNK = 64
NSA_G, NSA_R, NSA_D = 2, 4, 128
CMP_LEN, CMP_STRIDE = 32, 16
SEL_BLOCK, SEL_TOPK, SEL_LOCAL = 64, 16, 2
WINDOW = 512
BIG = 1e9
PEER_H, PEER_NK, PEER_TOPK = 8, 128, 16
PEER_E = PEER_NK * PEER_NK
NEG = -1e30
BF16_LANES = 256

OFF_CX, OFF_CB, OFF_CC = 0, 512, 1024
OFF_GQ, OFF_GK, OFF_GV, OFF_GG = 1536, 1792, 2048, 2560
OFF_NQ = 3072
OFF_NKC, OFF_NVC, OFF_NKS, OFF_NVS, OFF_NKW, OFF_NVW = 4096, 4352, 4608, 4864, 5120, 5376
OFF_MISC = 5632
N_IN = 5760
ORIG_GA, ORIG_NQ, ORIG_GATE, ORIG_END = 3072, 3088, 5648, 5672

TM_PROJ = 1024
TN_IN = 1152
TS_ELT = 512
TC_GLA = 512
TQ = 128
TK = 512
CMP_ROWS = 256
TM_PEER = 512
TE_PEER = 1024
TM_ROUTE = 256
VMEM_LIMIT = 56 * 1024 * 1024
PEER_EXPERT_FLAGS = None


def _cp(sem, flags=None):
    return pltpu.CompilerParams(dimension_semantics=sem, vmem_limit_bytes=VMEM_LIMIT, flags=flags)


def _gelu(x):
    return 0.5 * x * (1.0 + jnp.tanh(0.7978845608028654 * (x + 0.044715 * (x * x * x))))


def _split_bf16(x):
    hi = x.astype(bf16)
    lo = (x - hi.astype(f32)).astype(bf16)
    return hi, lo


def _group_rms(y, gain):
    ms = jnp.mean(y * y, axis=-1, keepdims=True)
    return y * lax.rsqrt(ms + EPS) * gain


def _norm_matmul_kernel(x_ref, g_ref, w_ref, *refs, emit_xn):
    if emit_xn:
        o_ref, xo_ref, xn_ref = refs
    else:
        o_ref, xn_ref = refs

    @pl.when(pl.program_id(1) == 0)
    def _():
        x = x_ref[...]
        ms = jnp.mean(x * x, axis=-1, keepdims=True)
        xn = (x * lax.rsqrt(ms + EPS) * g_ref[...]).astype(bf16)
        xn_ref[...] = xn
        if emit_xn:
            xo_ref[...] = xn

    o_ref[...] = jnp.dot(xn_ref[...], w_ref[...], preferred_element_type=f32)


def _norm_matmul(x, gain, w, tn, emit_xn=False):
    t, d = x.shape
    n = w.shape[1]
    tm = TM_PROJ
    out_shape = [jax.ShapeDtypeStruct((t, n), f32)]
    out_specs = [pl.BlockSpec((tm, tn), lambda i, j: (i, j))]
    if emit_xn:
        out_shape.append(jax.ShapeDtypeStruct((t, d), bf16))
        out_specs.append(pl.BlockSpec((tm, d), lambda i, j: (i, 0)))
    outs = pl.pallas_call(
        functools.partial(_norm_matmul_kernel, emit_xn=emit_xn),
        grid=(t // tm, n // tn),
        in_specs=[
            pl.BlockSpec((tm, d), lambda i, j: (i, 0)),
            pl.BlockSpec((1, d), lambda i, j: (0, 0)),
            pl.BlockSpec((d, tn), lambda i, j: (0, j)),
        ],
        out_specs=out_specs,
        out_shape=out_shape,
        scratch_shapes=[pltpu.VMEM((tm, d), bf16)],
        compiler_params=_cp(("parallel", "arbitrary")),
        name="norm_matmul",
    )(x, gain.reshape(1, d), w)
    return outs if emit_xn else outs[0]


def _conv_kernel(cx_ref, cb_ref, cc_ref, pcx_ref, pcc_ref, w_ref, g_ref, o_ref, z_ref, *, ts, tiles_per_seq):
    first = (pl.program_id(0) % tiles_per_seq) == 0
    zp = pcx_ref[...] * pcc_ref[...]
    z_ref[0:8, :] = jnp.where(first, 0.0, zp)
    z_ref[8:8 + ts, :] = cx_ref[...] * cc_ref[...]
    y = (z_ref[8:8 + ts, :] * w_ref[0:1, :]
         + z_ref[7:7 + ts, :] * w_ref[1:2, :]
         + z_ref[6:6 + ts, :] * w_ref[2:3, :])
    y = cb_ref[...] * y
    for c in range(CONV_W // HEAD):
        sl = slice(c * HEAD, (c + 1) * HEAD)
        o_ref[:, sl] = _group_rms(y[:, sl], g_ref[:, sl]).astype(bf16)


def _conv_mixer(p, conv_w, gain, seq):
    t = p.shape[0]
    ts = TS_ELT
    w_t = conv_w.astype(f32).T
    cur = lambda off: pl.BlockSpec((ts, CONV_W), lambda i, off=off: (i, off // CONV_W))
    prev = lambda off: pl.BlockSpec((8, CONV_W), lambda i, off=off: (jnp.maximum(i * (ts // 8) - 1, 0), off // CONV_W))
    return pl.pallas_call(
        functools.partial(_conv_kernel, ts=ts, tiles_per_seq=seq // ts),
        grid=(t // ts,),
        in_specs=[cur(OFF_CX), cur(OFF_CB), cur(OFF_CC), prev(OFF_CX), prev(OFF_CC),
                  pl.BlockSpec((CONV_K, CONV_W), lambda i: (0, 0)),
                  pl.BlockSpec((1, CONV_W), lambda i: (0, 0))],
        out_specs=pl.BlockSpec((ts, CONV_W), lambda i: (i, 0)),
        out_shape=jax.ShapeDtypeStruct((t, CONV_W), bf16),
        scratch_shapes=[pltpu.VMEM((ts + 8, CONV_W), f32)],
        compiler_params=_cp(("parallel",)),
        name="conv_mixer",
    )(p, p, p, p, p, w_t, gain.reshape(1, CONV_W))


def _gla_kernel(q_ref, k_ref, v_ref, gg_ref, misc_ref, wa_ref, ba_ref, gn_ref, o_ref,
                st_ref, la_ref, y_ref, *, tc):
    @pl.when(pl.program_id(1) == 0)
    def _():
        st_ref[...] = jnp.zeros_like(st_ref)

    a_hi, a_lo = _split_bf16(misc_ref[:, 0:GLA_LR])
    w_hi, w_lo = _split_bf16(wa_ref[...])
    z = (jnp.dot(a_hi, w_hi, preferred_element_type=f32)
         + jnp.dot(a_hi, w_lo, preferred_element_type=f32)
         + jnp.dot(a_lo, w_hi, preferred_element_type=f32)) + ba_ref[...]
    la_ref[...] = (jnp.minimum(z, 0.0) - jnp.log(1.0 + jnp.exp(-jnp.abs(z)))) * (1.0 / GLA_TAU)

    c = GLA_CHUNK
    row = lax.broadcasted_iota(jnp.int32, (c, c), 0)
    col = lax.broadcasted_iota(jnp.int32, (c, c), 1)
    causal = row >= col
    tril = causal.astype(bf16)
    scale = GLA_DK ** -0.5

    def chunk(ci, carry):
        r0 = pl.multiple_of(ci * c, c)
        la = la_ref[pl.ds(r0, c), :]
        la_hi, la_lo = _split_bf16(la)
        la_lo2 = (la - la_hi.astype(f32) - la_lo.astype(f32)).astype(bf16)
        bc = (jnp.dot(tril, la_hi, preferred_element_type=f32)
              + jnp.dot(tril, la_lo, preferred_element_type=f32)
              + jnp.dot(tril, la_lo2, preferred_element_type=f32))
        bl = bc[c - 1:c, :]
        qc = q_ref[pl.ds(r0, c), :]
        kc = k_ref[pl.ds(r0, c), :]
        vc = v_ref[pl.ds(r0, c), :].astype(bf16)
        q_dec = (qc * scale * jnp.exp(bc)).astype(bf16)
        k_inv = (kc * jnp.exp(-bc)).astype(bf16)
        k_end = (kc * jnp.exp(bl - bc)).astype(bf16)
        decay = jnp.exp(bl)
        for h in range(GLA_H):
            ks = slice(h * GLA_DK, (h + 1) * GLA_DK)
            vs = slice(h * GLA_DV, (h + 1) * GLA_DV)
            att = lax.dot_general(q_dec[:, ks], k_inv[:, ks], (((1,), (1,)), ((), ())),
                                  preferred_element_type=f32)
            att = jnp.where(causal, att, 0.0).astype(bf16)
            o = jnp.dot(att, vc[:, vs], preferred_element_type=f32)
            st = st_ref[h]
            o = o + lax.dot_general(q_dec[:, ks], st.astype(bf16), (((1,), (1,)), ((), ())),
                                    preferred_element_type=f32)
            y_ref[pl.ds(r0, c), vs] = o
            upd = lax.dot_general(vc[:, vs], k_end[:, ks], (((0,), (0,)), ((), ())),
                                  preferred_element_type=f32)
            st_ref[h] = st * decay[:, ks] + upd
        return carry

    lax.fori_loop(0, tc // c, chunk, 0)

    g = gg_ref[...]
    sw = g * (1.0 / (1.0 + jnp.exp(-g)))
    for h in range(GLA_H):
        vs = slice(h * GLA_DV, (h + 1) * GLA_DV)
        o_ref[:, vs] = (_group_rms(y_ref[:, vs], gn_ref[:, vs]) * sw[:, vs]).astype(bf16)


def _gla_mixer(p, w_a2, b_a, gain, batch, seq):
    t = p.shape[0]
    tc = TC_GLA
    nst = seq // tc
    blk = lambda width, off: pl.BlockSpec((tc, width), lambda b, s, off=off, width=width: (b * nst + s, off // width))
    const = lambda shape: pl.BlockSpec(shape, lambda b, s: (0,) * len(shape))
    return pl.pallas_call(
        functools.partial(_gla_kernel, tc=tc),
        grid=(batch, nst),
        in_specs=[blk(256, OFF_GQ), blk(256, OFF_GK), blk(512, OFF_GV), blk(512, OFF_GG), blk(128, OFF_MISC),
                  const((GLA_LR, GLA_H * GLA_DK)), const((1, GLA_H * GLA_DK)), const((1, GLA_H * GLA_DV))],
        out_specs=pl.BlockSpec((tc, GLA_H * GLA_DV), lambda b, s: (b * nst + s, 0)),
        out_shape=jax.ShapeDtypeStruct((t, GLA_H * GLA_DV), bf16),
        scratch_shapes=[pltpu.VMEM((GLA_H, GLA_DV, GLA_DK), f32),
                        pltpu.VMEM((tc, GLA_H * GLA_DK), f32),
                        pltpu.VMEM((tc, GLA_H * GLA_DV), f32)],
        compiler_params=_cp(("parallel", "arbitrary")),
        name="gla_mixer",
    )(p, p, p, p, p, w_a2.astype(f32), b_a.astype(f32).reshape(1, -1), gain.reshape(1, -1))


def _compress_kernel(t_ref, pos_ref, w1_ref, w2_ref, gn_ref, o_ref, b_scr, *, n_piece, apply_norm):
    half = CMP_STRIDE
    acc_a = jnp.zeros((n_piece, NSA_D), f32)
    acc_b = jnp.zeros((n_piece, NSA_D), f32)
    for r in range(half):
        rows = t_ref[pl.ds(r, n_piece, stride=CMP_STRIDE), :]
        xa = (rows + pos_ref[r:r + 1, :]).astype(bf16)
        acc_a = acc_a + jnp.dot(xa, w1_ref[r], preferred_element_type=f32)
        xb = (rows + pos_ref[half + r:half + r + 1, :]).astype(bf16)
        acc_b = acc_b + jnp.dot(xb, w1_ref[half + r], preferred_element_type=f32)
    b_scr[0:n_piece, :] = acc_b
    b_scr[n_piece:n_piece + 8, :] = jnp.zeros((8, NSA_D), f32)
    hid = acc_a + b_scr[1:n_piece + 1, :]
    out = jnp.dot(_gelu(hid).astype(bf16), w2_ref[...], preferred_element_type=f32)
    if apply_norm:
        out = _group_rms(out, gn_ref[...])
    o_ref[0] = out.astype(bf16)


def _compress(p, off, pos, w1, w2, gain, batch, seq, apply_norm):
    n_piece = seq // CMP_STRIDE
    return pl.pallas_call(
        functools.partial(_compress_kernel, n_piece=n_piece, apply_norm=apply_norm),
        grid=(batch, NSA_G),
        in_specs=[pl.BlockSpec((seq, NSA_D), lambda b, g: (b, off // NSA_D + g)),
                  pl.BlockSpec((CMP_LEN, NSA_D), lambda b, g: (0, 0)),
                  pl.BlockSpec((CMP_LEN, NSA_D, NSA_D), lambda b, g: (0, 0, 0)),
                  pl.BlockSpec((NSA_D, NSA_D), lambda b, g: (0, 0)),
                  pl.BlockSpec((1, NSA_D), lambda b, g: (0, 0))],
        out_specs=pl.BlockSpec((1, n_piece, NSA_D), lambda b, g: (b * NSA_G + g, 0, 0)),
        out_shape=jax.ShapeDtypeStruct((batch * NSA_G, n_piece, NSA_D), bf16),
        scratch_shapes=[pltpu.VMEM((n_piece + 8, NSA_D), f32)],
        compiler_params=_cp(("parallel", "parallel")),
        name="nsa_compress",
    )(p, pos.astype(f32), w1.astype(bf16).reshape(CMP_LEN, NSA_D, NSA_D), w2.astype(bf16),
      gain.astype(f32).reshape(1, NSA_D))


def _kvprep_kernel(ks_ref, vs_ref, kw_ref, vw_ref, gs_ref, gw_ref, oks_ref, ovs_ref, okw_ref, ovw_ref):
    lead = pl.program_id(2) == 0
    oks_ref[0] = _group_rms(ks_ref[...], gs_ref[...]).astype(bf16)
    ovs_ref[0] = vs_ref[...].T.astype(bf16)
    okw_ref[0] = jnp.where(lead, 0.0, _group_rms(kw_ref[...], gw_ref[...])).astype(bf16)
    ovw_ref[0] = jnp.where(lead, 0.0, vw_ref[...]).T.astype(bf16)


def _kv_prep(p, k_norm, batch, seq):
    ts = WINDOW
    nst = seq // ts
    src = lambda s: jnp.maximum(s - 1, 0)
    blk = lambda off: pl.BlockSpec((ts, NSA_D), lambda b, g, s, off=off: (b * nst + src(s), off // NSA_D + g))
    gspec = pl.BlockSpec((1, NSA_D), lambda b, g, s: (0, 0))
    kspec = pl.BlockSpec((1, ts, NSA_D), lambda b, g, s: (b * NSA_G + g, src(s), 0))
    vspec = pl.BlockSpec((1, NSA_D, ts), lambda b, g, s: (b * NSA_G + g, 0, src(s)))
    kwspec = pl.BlockSpec((1, ts, NSA_D), lambda b, g, s: (b * NSA_G + g, s, 0))
    vwspec = pl.BlockSpec((1, NSA_D, ts), lambda b, g, s: (b * NSA_G + g, 0, s))
    bg = batch * NSA_G
    return pl.pallas_call(
        _kvprep_kernel,
        grid=(batch, NSA_G, nst + 1),
        in_specs=[blk(OFF_NKS), blk(OFF_NVS), blk(OFF_NKW), blk(OFF_NVW), gspec, gspec],
        out_specs=[kspec, vspec, kwspec, vwspec],
        out_shape=[jax.ShapeDtypeStruct((bg, seq, NSA_D), bf16), jax.ShapeDtypeStruct((bg, NSA_D, seq), bf16),
                   jax.ShapeDtypeStruct((bg, seq + WINDOW, NSA_D), bf16),
                   jax.ShapeDtypeStruct((bg, NSA_D, seq + WINDOW), bf16)],
        compiler_params=_cp(("parallel", "parallel", "arbitrary")),
        name="nsa_kv_prep",
    )(p, p, p, p, k_norm[1].astype(f32).reshape(1, NSA_D), k_norm[2].astype(f32).reshape(1, NSA_D))


def _stack_q(q_ref, qn_ref):
    parts = []
    for r in range(NSA_R):
        qn = _group_rms(q_ref[:, r * NSA_D:(r + 1) * NSA_D], qn_ref[...])
        parts.append((qn * (NSA_D ** -0.5)).astype(bf16))
    return jnp.concatenate(parts, axis=0)


def _head_slope(g, r):
    return jnp.where(g == 0, 1.0, 2.0 ** -NSA_R).astype(f32) * (2.0 ** -(r + 1))


def _nsa_cmp_kernel(q_ref, qn_ref, kcc_ref, vcc_ref, ovt_ref, oc_ref, selt_ref, s_scr, p_scr, psum_scr,
                    *, tq, n_cmp, n_sel):
    g = pl.program_id(1)
    q0 = pl.program_id(2) * tq
    qs = _stack_q(q_ref, qn_ref)
    s_scr[...] = lax.dot_general(kcc_ref[0], qs, (((1,), (1,)), ((), ())), preferred_element_type=f32)
    rc = CMP_ROWS
    n_chunk = n_cmp // rc
    n_vis = jnp.clip((q0 + tq - CMP_LEN) // CMP_STRIDE + 1, 0, n_cmp)
    n_act = (n_vis + rc - 1) // rc
    tcol = lax.broadcasted_iota(jnp.int32, (rc, tq), 1)
    nrow = lax.broadcasted_iota(jnp.int32, (rc, tq), 0)

    def rel_end(c):
        return (c * rc + nrow) * CMP_STRIDE + (CMP_LEN - 1 - q0)

    for r in range(NSA_R):
        cs = slice(r * tq, (r + 1) * tq)
        slope = _head_slope(g, r)

        def biased(c, m):
            r0 = pl.multiple_of(c * rc, rc)
            rel = rel_end(c)
            sb = jnp.where(rel <= tcol, s_scr[pl.ds(r0, rc), cs] + slope * rel.astype(f32), NEG)
            s_scr[pl.ds(r0, rc), cs] = sb
            return jnp.maximum(m, jnp.max(sb, axis=0, keepdims=True))

        m = lax.fori_loop(0, n_act, biased, jnp.full((1, tq), NEG, f32))
        m = jnp.maximum(m, 0.1 * NEG)

        def expo(c, l):
            r0 = pl.multiple_of(c * rc, rc)
            e = jnp.exp(s_scr[pl.ds(r0, rc), cs] - m)
            s_scr[pl.ds(r0, rc), cs] = e
            return l + jnp.sum(e, axis=0, keepdims=True)

        l = lax.fori_loop(0, n_act, expo, jnp.zeros((1, tq), f32))
        inv = 1.0 / jnp.maximum(l, 1e-30)

        def normalise(c, carry):
            r0 = pl.multiple_of(c * rc, rc)
            pn = s_scr[pl.ds(r0, rc), cs] * inv
            p_scr[pl.ds(r0, rc), cs] = pn.astype(bf16)
            if r == 0:
                psum_scr[pl.ds(r0, rc), :] = pn
            else:
                psum_scr[pl.ds(r0, rc), :] += pn
            return carry

        lax.fori_loop(0, n_act, normalise, 0)

    def clear(c, carry):
        r0 = pl.multiple_of(c * rc, rc)
        p_scr[pl.ds(r0, rc), :] = jnp.zeros((rc, NSA_R * tq), bf16)
        psum_scr[pl.ds(r0, rc), :] = jnp.zeros((rc, tq), f32)
        return carry

    lax.fori_loop(n_act, n_chunk, clear, 0)

    vcc = vcc_ref[0]
    for r in range(NSA_R):
        oc_ref[:, r * NSA_D:(r + 1) * NSA_D] = lax.dot_general(
            p_scr[:, r * tq:(r + 1) * tq], vcc, (((0,), (0,)), ((), ())), preferred_element_type=f32)

    p_hi, p_lo = _split_bf16(psum_scr[...])
    imp = (jnp.dot(ovt_ref[...], p_hi, preferred_element_type=f32)
           + jnp.dot(ovt_ref[...], p_lo, preferred_element_type=f32))
    blk = lax.broadcasted_iota(jnp.int32, (n_sel, tq), 0)
    tblk = (q0 + lax.broadcasted_iota(jnp.int32, (n_sel, tq), 1)) // SEL_BLOCK
    lag = tblk - blk
    score = jnp.where(blk == 0, BIG, jnp.where(lag < SEL_LOCAL, BIG, imp))
    score = jnp.where(lag >= 0, score, -1.0)
    sel = jnp.zeros((n_sel, tq), f32)
    for _ in range(SEL_TOPK):
        m = jnp.max(score, axis=0, keepdims=True)
        first = jnp.min(jnp.where(score == m, blk, n_sel), axis=0, keepdims=True)
        hit = blk == first
        sel = jnp.where(hit, 1.0, sel)
        score = jnp.where(hit, -2.0, score)
    selt_ref[0] = sel.astype(bf16)


def _nsa_cmp(p, q_norm, kcc, vcc, batch, seq):
    tq = TQ
    nq = seq // tq
    n_cmp = seq // CMP_STRIDE
    n_sel = seq // SEL_BLOCK
    ci = np.arange(n_cmp)[None, :] * CMP_STRIDE
    sj = np.arange(n_sel)[:, None] * SEL_BLOCK
    ovt = np.clip(np.minimum(ci + CMP_LEN, sj + SEL_BLOCK) - np.maximum(ci, sj), 0, None) / CMP_LEN
    ovt = jnp.asarray(ovt, dtype=bf16)
    t = batch * seq
    hw = NSA_R * NSA_D
    return pl.pallas_call(
        functools.partial(_nsa_cmp_kernel, tq=tq, n_cmp=n_cmp, n_sel=n_sel),
        grid=(batch, NSA_G, nq),
        in_specs=[pl.BlockSpec((tq, hw), lambda b, g, i: (b * nq + i, OFF_NQ // hw + g)),
                  pl.BlockSpec((1, NSA_D), lambda b, g, i: (0, 0)),
                  pl.BlockSpec((1, n_cmp, NSA_D), lambda b, g, i: (b * NSA_G + g, 0, 0)),
                  pl.BlockSpec((1, n_cmp, NSA_D), lambda b, g, i: (b * NSA_G + g, 0, 0)),
                  pl.BlockSpec((n_sel, n_cmp), lambda b, g, i: (0, 0))],
        out_specs=[pl.BlockSpec((tq, hw), lambda b, g, i: (b * nq + i, g)),
                   pl.BlockSpec((1, n_sel, tq), lambda b, g, i: (b * NSA_G + g, 0, i))],
        out_shape=[jax.ShapeDtypeStruct((t, NSA_G * hw), f32),
                   jax.ShapeDtypeStruct((batch * NSA_G, n_sel, seq), bf16)],
        scratch_shapes=[pltpu.VMEM((n_cmp, NSA_R * tq), f32), pltpu.VMEM((n_cmp, NSA_R * tq), bf16),
                        pltpu.VMEM((n_cmp, tq), f32)],
        compiler_params=_cp(("parallel", "parallel", "parallel")),
        name="nsa_cmp_select",
    )(p, q_norm.astype(f32).reshape(1, NSA_D), kcc, vcc, ovt)


def _nsa_attn_kernel(flag_ref, q_ref, qn_ref, misc_ref, oc_ref, selt_ref, ks_ref, vst_ref, kw_ref, vwt_ref, gn_ref,
                     o_ref, m_scr, l_scr, acc_scr, *, tq, tk, nkt):
    b = pl.program_id(0)
    g = pl.program_id(1)
    qi = pl.program_id(2)
    nq = pl.num_programs(2)
    q0 = qi * tq
    cols = NSA_R * tq
    qs = _stack_q(q_ref, qn_ref)
    nt_dims = (((1,), (1,)), ((), ()))

    bpt = tk // SEL_BLOCK
    krow = lax.broadcasted_iota(jnp.int32, (tk, tq), 0)
    tcol = lax.broadcasted_iota(jnp.int32, (tk, tq), 1)
    kt_last = (q0 + tq - 1) // tk

    m_scr[...] = jnp.full((1, cols), NEG, f32)
    l_scr[...] = jnp.zeros((1, cols), f32)
    acc_scr[...] = jnp.zeros((NSA_D, cols), f32)

    def sel_step(kt, causal):
        k0 = pl.multiple_of(kt * tk, tk)
        kb = ks_ref[0, pl.ds(k0, tk), :]
        vtb = vst_ref[0, :, pl.ds(k0, tk)]
        st = lax.dot_general(kb, qs, nt_dims, preferred_element_type=f32)
        picked = selt_ref[0, pl.ds(pl.multiple_of(kt * bpt, bpt), bpt), :]
        off = (picked.astype(f32) - 1.0) * (-NEG)
        bias = jnp.concatenate([jnp.broadcast_to(off[u:u + 1, :], (SEL_BLOCK, tq)) for u in range(bpt)], axis=0)
        rel = krow + (k0 - q0)
        if causal:
            bias = bias + jnp.where(rel <= tcol, 0.0, NEG)
        relf = rel.astype(f32)
        for r in range(NSA_R):
            cs = slice(r * tq, (r + 1) * tq)
            sr = st[:, cs] + _head_slope(g, r) * relf + bias
            m_old = m_scr[:, cs]
            m_new = jnp.maximum(m_old, jnp.max(sr, axis=0, keepdims=True))
            alpha = jnp.exp(m_old - m_new)
            pr = jnp.exp(sr - m_new)
            l_scr[:, cs] = alpha * l_scr[:, cs] + jnp.sum(pr, axis=0, keepdims=True)
            acc_scr[:, cs] = alpha * acc_scr[:, cs] + jnp.dot(vtb, pr.astype(bf16), preferred_element_type=f32)
            m_scr[:, cs] = m_new

    fbase = ((b * NSA_G + g) * nq + qi) * nkt

    def sel_loop(kt, carry):
        @pl.when(flag_ref[fbase + kt] > 0)
        def _():
            sel_step(kt, False)
        return carry

    lax.fori_loop(0, kt_last, sel_loop, 0)
    sel_step(kt_last, True)
    o_st = acc_scr[...] * (1.0 / jnp.maximum(l_scr[...], 1e-30))

    wk = WINDOW + tq
    w0 = pl.multiple_of(q0, tq)
    kwin = kw_ref[0, pl.ds(w0, wk), :]
    vwt = vwt_ref[0, :, pl.ds(w0, wk)]
    swt = lax.dot_general(kwin, qs, nt_dims, preferred_element_type=f32)
    jr = lax.broadcasted_iota(jnp.int32, (wk, tq), 0)
    tc = lax.broadcasted_iota(jnp.int32, (wk, tq), 1)
    wbias = jnp.where(jr > tc, jnp.where(jr <= tc + WINDOW, 0.0, NEG), NEG)
    wbias = wbias + jnp.where(jr >= WINDOW - q0, 0.0, NEG)
    jf = jr.astype(f32)
    o_wt = []
    for r in range(NSA_R):
        sr = swt[:, r * tq:(r + 1) * tq] + _head_slope(g, r) * jf + wbias
        m = jnp.max(sr, axis=0, keepdims=True)
        pr = jnp.exp(sr - m)
        den = jnp.sum(pr, axis=0, keepdims=True)
        o_wt.append(jnp.dot(vwt, pr.astype(bf16), preferred_element_type=f32) * (1.0 / den))

    for r in range(NSA_R):
        lanes = []
        for c in range(3):
            l0 = GLA_LR + r * 3 + c
            l1 = l0 + NSA_R * 3
            lanes.append(jnp.where(g == 0, misc_ref[:, l0:l0 + 1], misc_ref[:, l1:l1 + 1]))
        gc, gs, gw = [1.0 / (1.0 + jnp.exp(-x)) for x in lanes]
        hs = slice(r * NSA_D, (r + 1) * NSA_D)
        y = gc * oc_ref[:, hs] + gs * o_st[:, r * tq:(r + 1) * tq].T + gw * o_wt[r].T
        o_ref[:, hs] = _group_rms(y, gn_ref[:, hs]).astype(bf16)


def _nsa_attn(p, q_norm, o_c, selt, flags, ks, vst, kw, vwt, gain, batch, seq):
    tq, tk = TQ, TK
    nq = seq // tq
    nkt = seq // tk
    n_sel = seq // SEL_BLOCK
    t = batch * seq
    hw = NSA_R * NSA_D
    slab = lambda rows: pl.BlockSpec((1, rows, NSA_D), lambda b, g, i, f: (b * NSA_G + g, 0, 0),
                                     pipeline_mode=pl.Buffered(1))
    slab_t = lambda rows: pl.BlockSpec((1, NSA_D, rows), lambda b, g, i, f: (b * NSA_G + g, 0, 0),
                                       pipeline_mode=pl.Buffered(1))
    grid_spec = pltpu.PrefetchScalarGridSpec(
        num_scalar_prefetch=1,
        grid=(batch, NSA_G, nq),
        in_specs=[pl.BlockSpec((tq, hw), lambda b, g, i, f: (b * nq + i, OFF_NQ // hw + g)),
                  pl.BlockSpec((1, NSA_D), lambda b, g, i, f: (0, 0)),
                  pl.BlockSpec((tq, 128), lambda b, g, i, f: (b * nq + i, OFF_MISC // 128)),
                  pl.BlockSpec((tq, hw), lambda b, g, i, f: (b * nq + i, g)),
                  pl.BlockSpec((1, n_sel, tq), lambda b, g, i, f: (b * NSA_G + g, 0, i)),
                  slab(seq), slab_t(seq), slab(seq + WINDOW), slab_t(seq + WINDOW),
                  pl.BlockSpec((1, hw), lambda b, g, i, f: (0, g))],
        out_specs=pl.BlockSpec((tq, hw), lambda b, g, i, f: (b * nq + i, g)),
        scratch_shapes=[pltpu.VMEM((1, NSA_R * tq), f32), pltpu.VMEM((1, NSA_R * tq), f32),
                        pltpu.VMEM((NSA_D, NSA_R * tq), f32)],
    )
    return pl.pallas_call(
        functools.partial(_nsa_attn_kernel, tq=tq, tk=tk, nkt=nkt),
        grid_spec=grid_spec,
        out_shape=jax.ShapeDtypeStruct((t, NSA_G * hw), bf16),
        compiler_params=_cp(("parallel", "parallel", "arbitrary")),
        name="nsa_attention",
    )(flags, p, q_norm.astype(f32).reshape(1, NSA_D), p, o_c, selt, ks, vst, kw, vwt, gain.reshape(1, -1))


def _out_proj_kernel(a_ref, b_ref, c_ref, wa_ref, wb_ref, wc_ref, h_ref, o_ref):
    o_ref[...] = (h_ref[...]
                  + jnp.dot(a_ref[...], wa_ref[...], preferred_element_type=f32)
                  + jnp.dot(b_ref[...], wb_ref[...], preferred_element_type=f32)
                  + jnp.dot(c_ref[...], wc_ref[...], preferred_element_type=f32))


def _out_proj(ya, yb, yc, w_out, h):
    t, d = h.shape
    tm, tn = TM_PROJ, 1024
    w = w_out.astype(bf16)
    ka, kb, kc = ya.shape[1], yb.shape[1], yc.shape[1]
    return pl.pallas_call(
        _out_proj_kernel,
        grid=(t // tm, d // tn),
        in_specs=[pl.BlockSpec((tm, ka), lambda i, j: (i, 0)),
                  pl.BlockSpec((tm, kb), lambda i, j: (i, 0)),
                  pl.BlockSpec((tm, kc), lambda i, j: (i, 0)),
                  pl.BlockSpec((ka, tn), lambda i, j: (0, j)),
                  pl.BlockSpec((kb, tn), lambda i, j: (1, j)),
                  pl.BlockSpec((kc, tn), lambda i, j: (1, j)),
                  pl.BlockSpec((tm, tn), lambda i, j: (i, j))],
        out_specs=pl.BlockSpec((tm, tn), lambda i, j: (i, j)),
        out_shape=jax.ShapeDtypeStruct((t, d), f32),
        compiler_params=_cp(("parallel", "parallel")),
        name="out_proj",
    )(ya, yb, yc, w, w, w, h)


def _top_k_columns(v, k, ids=None, want_rank=True):
    if ids is None:
        ids = lax.broadcasted_iota(jnp.int32, v.shape, 0)
    big = jnp.int32(2 ** 30)
    kidx = lax.broadcasted_iota(jnp.int32, (k, v.shape[1]), 0)
    vals = jnp.zeros((k, v.shape[1]), f32)
    firsts = []
    rank = jnp.full(v.shape, float(k), f32) if want_rank else None
    for kk in range(k):
        m = jnp.max(v, axis=0, keepdims=True)
        first = jnp.min(jnp.where(v == m, ids, big), axis=0, keepdims=True)
        hit = ids == first
        if want_rank:
            rank = jnp.where(hit, float(kk), rank)
        v = jnp.where(hit, -jnp.inf, v)
        vals = jnp.where(kidx == kk, m, vals)
        firsts.append(first)
    return vals, rank, firsts


def _pair_candidates(t1, t2):
    n = t1.shape[1]
    k = PEER_TOPK
    i16 = lax.broadcasted_iota(jnp.int32, (k, n), 0)
    i8 = lax.broadcasted_iota(jnp.int32, (8, n), 0)
    vals = [t1[0:1, :] + t2, t1[1:2, :] + t2[0:8, :]]
    pos = [i16, k + i8]
    for a in range(2, 8):
        vals.append(jnp.where(i8 < k // (a + 1), t1[a:a + 1, :] + t2[0:8, :], -jnp.inf))
        pos.append(a * k + i8)
    vals.append(t1[8:16, :] + t2[0:1, :])
    pos.append((8 + i8) * k)
    return jnp.concatenate(vals, axis=0), jnp.concatenate(pos, axis=0)


def _peer_route_kernel(q_ref, key_ref, a_ref, cnt_ref, rank2_ref, e2_ref):
    tm = q_ref.shape[0]
    for h in range(PEER_H):
        sc, tops, ranks = [], [], []
        for half in range(2):
            c0 = (h * 2 + half) * PEER_NK
            qh = q_ref[:, c0:c0 + PEER_NK].astype(bf16)
            s = lax.dot_general(key_ref[h * 2 + half], qh, (((1,), (1,)), ((), ())),
                                preferred_element_type=f32)
            vals, rank, _ = _top_k_columns(s, PEER_TOPK)
            sc.append(s)
            tops.append(vals)
            ranks.append(rank)
        cand, cand_pos = _pair_candidates(tops[0], tops[1])
        best, _, pos = _top_k_columns(cand, PEER_TOPK, ids=cand_pos, want_rank=False)
        mass = jnp.sum(jnp.exp(best - best[0:1, :]), axis=0, keepdims=True)
        aidx = lax.broadcasted_iota(jnp.int32, (PEER_TOPK, tm), 0)
        per_rank = jnp.zeros((PEER_TOPK, tm), f32)
        for p in pos:
            per_rank = per_rank + jnp.where(aidx == p // PEER_TOPK, 1.0, 0.0)
        cnt = jnp.zeros((PEER_NK, tm), f32)
        for a in range(PEER_TOPK):
            cnt = cnt + jnp.where(ranks[0] == float(a), per_rank[a:a + 1, :], 0.0)
        hs = slice(h * PEER_NK, (h + 1) * PEER_NK)
        a_ref[hs, :] = (jnp.exp(sc[0] - tops[0][0:1, :]) * (0.5 / mass)).astype(bf16)
        cnt_ref[hs, :] = cnt.astype(bf16)
        rank2_ref[hs, :] = ranks[1].astype(bf16)
        e2_ref[hs, :] = jnp.exp(sc[1] - tops[1][0:1, :]).astype(bf16)


def _peer_route(qry, keys):
    t = qry.shape[0]
    tm = TM_ROUTE
    narrow = jax.ShapeDtypeStruct((PEER_H * PEER_NK, t), bf16)
    nspec = pl.BlockSpec((PEER_H * PEER_NK, tm), lambda i: (0, i))
    return pl.pallas_call(
        _peer_route_kernel,
        grid=(t // tm,),
        in_specs=[pl.BlockSpec((tm, PEER_H * 2 * PEER_NK), lambda i: (i, 0)),
                  pl.BlockSpec((PEER_H * 2, PEER_NK, PEER_NK), lambda i: (0, 0, 0))],
        out_specs=[nspec] * 4,
        out_shape=[narrow] * 4,
        compiler_params=_cp(("parallel",)),
        name="peer_route",
    )(qry, keys.astype(bf16).reshape(PEER_H * 2, PEER_NK, PEER_NK))


def _peer_expert_kernel(xn_ref, u_ref, vt_ref, a_ref, cnt_ref, rank2_ref, e2_ref, h_ref, o_ref,
                        acc_ref, act_ref, gt_ref, arow_ref, crow_ref, *, tm, te):
    j = pl.program_id(1)

    @pl.when(j == 0)
    def _():
        acc_ref[...] = jnp.zeros_like(acc_ref)

    n_first = te // PEER_NK
    i0 = pl.multiple_of(j * n_first, n_first)

    a_rows = [a_ref[pl.ds(pl.multiple_of(h * PEER_NK + i0, n_first), n_first), :] for h in range(PEER_H)]
    c_rows = [cnt_ref[pl.ds(pl.multiple_of(h * PEER_NK + i0, n_first), n_first), :] for h in range(PEER_H)]
    for h in range(PEER_H):
        arow_ref[h * n_first:(h + 1) * n_first, :] = a_rows[h]
        crow_ref[h * n_first:(h + 1) * n_first, :] = c_rows[h]

    x = lax.dot_general(u_ref[...], xn_ref[...], (((1,), (1,)), ((), ())), preferred_element_type=f32)
    act_ref[...] = (x * (1.0 + jnp.tanh(x * (0.7978845608028654 + 0.035677408136300125 * (x * x))))).astype(bf16)

    zero = jnp.zeros((PEER_NK, BF16_LANES), bf16)
    for c in range(tm // BF16_LANES):
        ls = slice(c * BF16_LANES, (c + 1) * BF16_LANES)
        for u in range(n_first):
            rs = slice(u * PEER_NK, (u + 1) * PEER_NK)
            w = zero
            for h in range(PEER_H):
                hs = slice(h * PEER_NK, (h + 1) * PEER_NK)
                a = arow_ref[h * n_first + u:h * n_first + u + 1, ls]
                cnt = crow_ref[h * n_first + u:h * n_first + u + 1, ls]
                keep = rank2_ref[hs, ls] < cnt
                w = w + jnp.where(keep, e2_ref[hs, ls], zero) * a
            gt_ref[rs, ls] = w * act_ref[rs, ls]

    acc_ref[...] += jnp.dot(vt_ref[...], gt_ref[...], preferred_element_type=f32)

    @pl.when(j == pl.num_programs(1) - 1)
    def _():
        o_ref[...] = h_ref[...] + acc_ref[...].T


def _peer_experts(xn, u, vt, a, cnt, rank2, e2, h):
    t, d = h.shape
    tm, te = TM_PEER, TE_PEER
    once = pl.Buffered(1)
    nspec = pl.BlockSpec((PEER_H * PEER_NK, tm), lambda i, j: (0, i), pipeline_mode=once)
    return pl.pallas_call(
        functools.partial(_peer_expert_kernel, tm=tm, te=te),
        grid=(t // tm, PEER_E // te),
        in_specs=[pl.BlockSpec((tm, d), lambda i, j: (i, 0), pipeline_mode=once),
                  pl.BlockSpec((te, d), lambda i, j: (j, 0)),
                  pl.BlockSpec((d, te), lambda i, j: (0, j)),
                  nspec, nspec, nspec, nspec,
                  pl.BlockSpec((tm, d), lambda i, j: (i, 0), pipeline_mode=once)],
        out_specs=pl.BlockSpec((tm, d), lambda i, j: (i, 0)),
        out_shape=jax.ShapeDtypeStruct((t, d), f32),
        scratch_shapes=[pltpu.VMEM((d, tm), f32), pltpu.VMEM((te, tm), bf16), pltpu.VMEM((te, tm), bf16),
                        pltpu.VMEM((PEER_H * te // PEER_NK, tm), bf16),
                        pltpu.VMEM((PEER_H * te // PEER_NK, tm), bf16)],
        compiler_params=_cp(("parallel", "arbitrary"), PEER_EXPERT_FLAGS),
        name="peer_experts",
    )(xn, u, vt, a, cnt, rank2, e2, h)


def _permute_w_in(w):
    pad = jnp.zeros((w.shape[0], N_IN - ORIG_END), w.dtype)
    return jnp.concatenate([w[:, :ORIG_GA], w[:, ORIG_NQ:ORIG_GATE], w[:, ORIG_GA:ORIG_NQ],
                            w[:, ORIG_GATE:ORIG_END], pad], axis=1).astype(bf16)


def _tile_flags(selt, seq):
    bg = selt.shape[0]
    any_sel = selt.reshape(bg, seq // TK, TK // SEL_BLOCK, seq // TQ, TQ).max(axis=(2, 4))
    return (jnp.swapaxes(any_sel, 1, 2) > 0.5).astype(jnp.int32).reshape(-1)


def _mixers(h, norm_mix, w_in, conv_w, gla_w_a2, gla_b_a, q_norm, k_norm, cmp_pos, cmp_w1, cmp_w2, out_norm,
            batch, seq):
    p = _norm_matmul(h, norm_mix.astype(f32), _permute_w_in(w_in), TN_IN)
    gain = out_norm.astype(f32)
    y_conv = _conv_mixer(p, conv_w, gain[:CONV_W], seq)
    y_gla = _gla_mixer(p, gla_w_a2, gla_b_a, gain[CONV_W:2 * CONV_W], batch, seq)
    kcc = _compress(p, OFF_NKC, cmp_pos[0], cmp_w1[0], cmp_w2[0], k_norm[0], batch, seq, True)
    vcc = _compress(p, OFF_NVC, cmp_pos[1], cmp_w1[1], cmp_w2[1], k_norm[0], batch, seq, False)
    ks, vs, kw, vw = _kv_prep(p, k_norm, batch, seq)
    o_c, selt = _nsa_cmp(p, q_norm, kcc, vcc, batch, seq)
    y_nsa = _nsa_attn(p, q_norm, o_c, selt, _tile_flags(selt, seq), ks, vs, kw, vw, gain[2 * CONV_W:], batch, seq)
    return y_conv, y_gla, y_nsa


def _layer(h, norm_mix, w_in, conv_w, gla_w_a2, gla_b_a, q_norm, k_norm, cmp_pos, cmp_w1, cmp_w2, out_norm, w_out,
           norm_ffn, peer_w_q, peer_keys, peer_u, peer_v, batch, seq):
    y_conv, y_gla, y_nsa = _mixers(h, norm_mix, w_in, conv_w, gla_w_a2, gla_b_a, q_norm, k_norm, cmp_pos, cmp_w1,
                                   cmp_w2, out_norm, batch, seq)
    h = _out_proj(y_conv, y_gla, y_nsa, w_out, h)
    qry, xn = _norm_matmul(h, norm_ffn.astype(f32), peer_w_q.astype(bf16), 1024, emit_xn=True)
    a, cnt, rank2, e2 = _peer_route(qry, peer_keys)
    return _peer_experts(xn, peer_u.astype(bf16), peer_v.astype(bf16).T, a, cnt, rank2, e2, h)


def kernel(x, norm_mix, w_in, conv_w, gla_w_a2, gla_b_a, nsa_q_norm, nsa_k_norm, nsa_cmp_pos, nsa_cmp_w1,
           nsa_cmp_w2, out_norm, w_out, norm_ffn, peer_w_q, peer_keys, peer_u, peer_v):
    batch, seq, d = x.shape
    h = x.reshape(batch * seq, d)
    for l in range(w_in.shape[0]):
        h = _layer(h, norm_mix[l], w_in[l], conv_w[l], gla_w_a2[l], gla_b_a[l], nsa_q_norm[l], nsa_k_norm[l],
                   nsa_cmp_pos[l], nsa_cmp_w1[l], nsa_cmp_w2[l], out_norm[l], w_out[l], norm_ffn[l], peer_w_q[l],
                   peer_keys[l], peer_u[l], peer_v[l], batch, seq)
    return h.reshape(batch, seq, d)
```

```python
import functools
import math

import numpy as np
import jax
import jax.numpy as jnp
from jax import lax
from jax.experimental import pallas as pl
from jax.experimental.pallas import tpu as pltpu

f32 = jnp.float32
bf16 = jnp.bfloat16

D_MODEL = 2048
EPS = 1e-6
HEAD = 128
CONV_W = 512
CONV_K = 3
GLA_H, GLA_DK, GLA_DV = 4, 64, 128
GLA_LR = 16
GLA_TAU = 16.0
GLA_CHUNK = 64
NSA_G, NSA_R, NSA_D = 2, 4, 128
CMP_LEN, CMP_STRIDE = 32, 16
SEL_BLOCK, SEL_TOPK, SEL_LOCAL = 64, 16, 2
WINDOW = 512
BIG = 1e9
PEER_H, PEER_NK, PEER_TOPK = 8, 128, 16
PEER_E = PEER_NK * PEER_NK
NEG = -1e30
BF16_LANES = 256

OFF_CX, OFF_CB, OFF_CC = 0, 512, 1024
OFF_GQ, OFF_GK, OFF_GV, OFF_GG = 1536, 1792, 2048, 2560
OFF_NQ = 3072
OFF_NKC, OFF_NVC, OFF_NKS, OFF_NVS, OFF_NKW, OFF_NVW = 4096, 4352, 4608, 4864, 5120, 5376
OFF_MISC = 5632
N_IN = 5760
ORIG_GA, ORIG_NQ, ORIG_GATE, ORIG_END = 3072, 3088, 5648, 5672

TM_PROJ = 1024
TN_IN = 1152
TS_ELT = 512
TC_GLA = 512
TQ = 128
TK = 512
CMP_ROWS = 256
TM_PEER = 1024
TE_PEER = 512
SUBLANES = 8
TM_ROUTE = 256
VMEM_LIMIT = 56 * 1024 * 1024
PEER_EXPERT_FLAGS = None


def _cp(sem, flags=None):
    return pltpu.CompilerParams(dimension_semantics=sem, vmem_limit_bytes=VMEM_LIMIT, flags=flags)


def _gelu(x):
    return 0.5 * x * (1.0 + jnp.tanh(0.7978845608028654 * (x + 0.044715 * (x * x * x))))


def _split_bf16(x):
    hi = x.astype(bf16)
    lo = (x - hi.astype(f32)).astype(bf16)
    return hi, lo


def _group_rms(y, gain):
    ms = jnp.mean(y * y, axis=-1, keepdims=True)
    return y * lax.rsqrt(ms + EPS) * gain


def _norm_matmul_kernel(x_ref, g_ref, w_ref, *refs, emit_xn):
    if emit_xn:
        o_ref, xo_ref, xn_ref = refs
    else:
        o_ref, xn_ref = refs

    @pl.when(pl.program_id(1) == 0)
    def _():
        x = x_ref[...]
        ms = jnp.mean(x * x, axis=-1, keepdims=True)
        xn = (x * lax.rsqrt(ms + EPS) * g_ref[...]).astype(bf16)
        xn_ref[...] = xn
        if emit_xn:
            xo_ref[...] = xn

    o_ref[...] = jnp.dot(xn_ref[...], w_ref[...], preferred_element_type=f32)


def _norm_matmul(x, gain, w, tn, emit_xn=False):
    t, d = x.shape
    n = w.shape[1]
    tm = TM_PROJ
    out_shape = [jax.ShapeDtypeStruct((t, n), f32)]
    out_specs = [pl.BlockSpec((tm, tn), lambda i, j: (i, j))]
    if emit_xn:
        out_shape.append(jax.ShapeDtypeStruct((t, d), bf16))
        out_specs.append(pl.BlockSpec((tm, d), lambda i, j: (i, 0)))
    outs = pl.pallas_call(
        functools.partial(_norm_matmul_kernel, emit_xn=emit_xn),
        grid=(t // tm, n // tn),
        in_specs=[
            pl.BlockSpec((tm, d), lambda i, j: (i, 0)),
            pl.BlockSpec((1, d), lambda i, j: (0, 0)),
            pl.BlockSpec((d, tn), lambda i, j: (0, j)),
        ],
        out_specs=out_specs,
        out_shape=out_shape,
        scratch_shapes=[pltpu.VMEM((tm, d), bf16)],
        compiler_params=_cp(("parallel", "arbitrary")),
        name="norm_matmul",
    )(x, gain.reshape(1, d), w)
    return outs if emit_xn else outs[0]


def _conv_kernel(cx_ref, cb_ref, cc_ref, pcx_ref, pcc_ref, w_ref, g_ref, o_ref, z_ref, *, ts, tiles_per_seq):
    first = (pl.program_id(0) % tiles_per_seq) == 0
    zp = pcx_ref[...] * pcc_ref[...]
    z_ref[0:8, :] = jnp.where(first, 0.0, zp)
    z_ref[8:8 + ts, :] = cx_ref[...] * cc_ref[...]
    y = (z_ref[8:8 + ts, :] * w_ref[0:1, :]
         + z_ref[7:7 + ts, :] * w_ref[1:2, :]
         + z_ref[6:6 + ts, :] * w_ref[2:3, :])
    y = cb_ref[...] * y
    for c in range(CONV_W // HEAD):
        sl = slice(c * HEAD, (c + 1) * HEAD)
        o_ref[:, sl] = _group_rms(y[:, sl], g_ref[:, sl]).astype(bf16)


def _conv_mixer(p, conv_w, gain, seq):
    t = p.shape[0]
    ts = TS_ELT
    w_t = conv_w.astype(f32).T
    cur = lambda off: pl.BlockSpec((ts, CONV_W), lambda i, off=off: (i, off // CONV_W))
    prev = lambda off: pl.BlockSpec((8, CONV_W), lambda i, off=off: (jnp.maximum(i * (ts // 8) - 1, 0), off // CONV_W))
    return pl.pallas_call(
        functools.partial(_conv_kernel, ts=ts, tiles_per_seq=seq // ts),
        grid=(t // ts,),
        in_specs=[cur(OFF_CX), cur(OFF_CB), cur(OFF_CC), prev(OFF_CX), prev(OFF_CC),
                  pl.BlockSpec((CONV_K, CONV_W), lambda i: (0, 0)),
                  pl.BlockSpec((1, CONV_W), lambda i: (0, 0))],
        out_specs=pl.BlockSpec((ts, CONV_W), lambda i: (i, 0)),
        out_shape=jax.ShapeDtypeStruct((t, CONV_W), bf16),
        scratch_shapes=[pltpu.VMEM((ts + 8, CONV_W), f32)],
        compiler_params=_cp(("parallel",)),
        name="conv_mixer",
    )(p, p, p, p, p, w_t, gain.reshape(1, CONV_W))


def _gla_kernel(q_ref, k_ref, v_ref, gg_ref, misc_ref, wa_ref, ba_ref, gn_ref, o_ref,
                st_ref, la_ref, y_ref, *, tc):
    @pl.when(pl.program_id(1) == 0)
    def _():
        st_ref[...] = jnp.zeros_like(st_ref)

    a_hi, a_lo = _split_bf16(misc_ref[:, 0:GLA_LR])
    w_hi, w_lo = _split_bf16(wa_ref[...])
    z = (jnp.dot(a_hi, w_hi, preferred_element_type=f32)
         + jnp.dot(a_hi, w_lo, preferred_element_type=f32)
         + jnp.dot(a_lo, w_hi, preferred_element_type=f32)) + ba_ref[...]
    la_ref[...] = (jnp.minimum(z, 0.0) - jnp.log(1.0 + jnp.exp(-jnp.abs(z)))) * (1.0 / GLA_TAU)

    c = GLA_CHUNK
    row = lax.broadcasted_iota(jnp.int32, (c, c), 0)
    col = lax.broadcasted_iota(jnp.int32, (c, c), 1)
    causal = row >= col
    tril = causal.astype(bf16)
    scale = GLA_DK ** -0.5

    def chunk(ci, carry):
        r0 = pl.multiple_of(ci * c, c)
        la = la_ref[pl.ds(r0, c), :]
        la_hi, la_lo = _split_bf16(la)
        la_lo2 = (la - la_hi.astype(f32) - la_lo.astype(f32)).astype(bf16)
        bc = (jnp.dot(tril, la_hi, preferred_element_type=f32)
              + jnp.dot(tril, la_lo, preferred_element_type=f32)
              + jnp.dot(tril, la_lo2, preferred_element_type=f32))
        bl = bc[c - 1:c, :]
        qc = q_ref[pl.ds(r0, c), :]
        kc = k_ref[pl.ds(r0, c), :]
        vc = v_ref[pl.ds(r0, c), :].astype(bf16)
        q_dec = (qc * scale * jnp.exp(bc)).astype(bf16)
        k_inv = (kc * jnp.exp(-bc)).astype(bf16)
        k_end = (kc * jnp.exp(bl - bc)).astype(bf16)
        decay = jnp.exp(bl)
        for h in range(GLA_H):
            ks = slice(h * GLA_DK, (h + 1) * GLA_DK)
            vs = slice(h * GLA_DV, (h + 1) * GLA_DV)
            att = lax.dot_general(q_dec[:, ks], k_inv[:, ks], (((1,), (1,)), ((), ())),
                                  preferred_element_type=f32)
            att = jnp.where(causal, att, 0.0).astype(bf16)
            o = jnp.dot(att, vc[:, vs], preferred_element_type=f32)
            st = st_ref[h]
            o = o + lax.dot_general(q_dec[:, ks], st.astype(bf16), (((1,), (1,)), ((), ())),
                                    preferred_element_type=f32)
            y_ref[pl.ds(r0, c), vs] = o
            upd = lax.dot_general(vc[:, vs], k_end[:, ks], (((0,), (0,)), ((), ())),
                                  preferred_element_type=f32)
            st_ref[h] = st * decay[:, ks] + upd
        return carry

    lax.fori_loop(0, tc // c, chunk, 0)

    g = gg_ref[...]
    sw = g * (1.0 / (1.0 + jnp.exp(-g)))
    for h in range(GLA_H):
        vs = slice(h * GLA_DV, (h + 1) * GLA_DV)
        o_ref[:, vs] = (_group_rms(y_ref[:, vs], gn_ref[:, vs]) * sw[:, vs]).astype(bf16)


def _gla_mixer(p, w_a2, b_a, gain, batch, seq):
    t = p.shape[0]
    tc = TC_GLA
    nst = seq // tc
    blk = lambda width, off: pl.BlockSpec((tc, width), lambda b, s, off=off, width=width: (b * nst + s, off // width))
    const = lambda shape: pl.BlockSpec(shape, lambda b, s: (0,) * len(shape))
    return pl.pallas_call(
        functools.partial(_gla_kernel, tc=tc),
        grid=(batch, nst),
        in_specs=[blk(256, OFF_GQ), blk(256, OFF_GK), blk(512, OFF_GV), blk(512, OFF_GG), blk(128, OFF_MISC),
                  const((GLA_LR, GLA_H * GLA_DK)), const((1, GLA_H * GLA_DK)), const((1, GLA_H * GLA_DV))],
        out_specs=pl.BlockSpec((tc, GLA_H * GLA_DV), lambda b, s: (b * nst + s, 0)),
        out_shape=jax.ShapeDtypeStruct((t, GLA_H * GLA_DV), bf16),
        scratch_shapes=[pltpu.VMEM((GLA_H, GLA_DV, GLA_DK), f32),
                        pltpu.VMEM((tc, GLA_H * GLA_DK), f32),
                        pltpu.VMEM((tc, GLA_H * GLA_DV), f32)],
        compiler_params=_cp(("parallel", "arbitrary")),
        name="gla_mixer",
    )(p, p, p, p, p, w_a2.astype(f32), b_a.astype(f32).reshape(1, -1), gain.reshape(1, -1))


def _compress_kernel(t_ref, pos_ref, w1_ref, w2_ref, gn_ref, o_ref, b_scr, *, n_piece, apply_norm):
    half = CMP_STRIDE
    acc_a = jnp.zeros((n_piece, NSA_D), f32)
    acc_b = jnp.zeros((n_piece, NSA_D), f32)
    for r in range(half):
        rows = t_ref[pl.ds(r, n_piece, stride=CMP_STRIDE), :]
        xa = (rows + pos_ref[r:r + 1, :]).astype(bf16)
        acc_a = acc_a + jnp.dot(xa, w1_ref[r], preferred_element_type=f32)
        xb = (rows + pos_ref[half + r:half + r + 1, :]).astype(bf16)
        acc_b = acc_b + jnp.dot(xb, w1_ref[half + r], preferred_element_type=f32)
    b_scr[0:n_piece, :] = acc_b
    b_scr[n_piece:n_piece + 8, :] = jnp.zeros((8, NSA_D), f32)
    hid = acc_a + b_scr[1:n_piece + 1, :]
    out = jnp.dot(_gelu(hid).astype(bf16), w2_ref[...], preferred_element_type=f32)
    if apply_norm:
        out = _group_rms(out, gn_ref[...])
    o_ref[0] = out.astype(bf16)


def _compress(p, off, pos, w1, w2, gain, batch, seq, apply_norm):
    n_piece = seq // CMP_STRIDE
    return pl.pallas_call(
        functools.partial(_compress_kernel, n_piece=n_piece, apply_norm=apply_norm),
        grid=(batch, NSA_G),
        in_specs=[pl.BlockSpec((seq, NSA_D), lambda b, g: (b, off // NSA_D + g)),
                  pl.BlockSpec((CMP_LEN, NSA_D), lambda b, g: (0, 0)),
                  pl.BlockSpec((CMP_LEN, NSA_D, NSA_D), lambda b, g: (0, 0, 0)),
                  pl.BlockSpec((NSA_D, NSA_D), lambda b, g: (0, 0)),
                  pl.BlockSpec((1, NSA_D), lambda b, g: (0, 0))],
        out_specs=pl.BlockSpec((1, n_piece, NSA_D), lambda b, g: (b * NSA_G + g, 0, 0)),
        out_shape=jax.ShapeDtypeStruct((batch * NSA_G, n_piece, NSA_D), bf16),
        scratch_shapes=[pltpu.VMEM((n_piece + 8, NSA_D), f32)],
        compiler_params=_cp(("parallel", "parallel")),
        name="nsa_compress",
    )(p, pos.astype(f32), w1.astype(bf16).reshape(CMP_LEN, NSA_D, NSA_D), w2.astype(bf16),
      gain.astype(f32).reshape(1, NSA_D))


def _kvprep_kernel(ks_ref, vs_ref, kw_ref, vw_ref, gs_ref, gw_ref, oks_ref, ovs_ref, okw_ref, ovw_ref):
    lead = pl.program_id(2) == 0
    oks_ref[0] = _group_rms(ks_ref[...], gs_ref[...]).astype(bf16)
    ovs_ref[0] = vs_ref[...].T.astype(bf16)
    okw_ref[0] = jnp.where(lead, 0.0, _group_rms(kw_ref[...], gw_ref[...])).astype(bf16)
    ovw_ref[0] = jnp.where(lead, 0.0, vw_ref[...]).T.astype(bf16)


def _kv_prep(p, k_norm, batch, seq):
    ts = WINDOW
    nst = seq // ts
    src = lambda s: jnp.maximum(s - 1, 0)
    blk = lambda off: pl.BlockSpec((ts, NSA_D), lambda b, g, s, off=off: (b * nst + src(s), off // NSA_D + g))
    gspec = pl.BlockSpec((1, NSA_D), lambda b, g, s: (0, 0))
    kspec = pl.BlockSpec((1, ts, NSA_D), lambda b, g, s: (b * NSA_G + g, src(s), 0))
    vspec = pl.BlockSpec((1, NSA_D, ts), lambda b, g, s: (b * NSA_G + g, 0, src(s)))
    kwspec = pl.BlockSpec((1, ts, NSA_D), lambda b, g, s: (b * NSA_G + g, s, 0))
    vwspec = pl.BlockSpec((1, NSA_D, ts), lambda b, g, s: (b * NSA_G + g, 0, s))
    bg = batch * NSA_G
    return pl.pallas_call(
        _kvprep_kernel,
        grid=(batch, NSA_G, nst + 1),
        in_specs=[blk(OFF_NKS), blk(OFF_NVS), blk(OFF_NKW), blk(OFF_NVW), gspec, gspec],
        out_specs=[kspec, vspec, kwspec, vwspec],
        out_shape=[jax.ShapeDtypeStruct((bg, seq, NSA_D), bf16), jax.ShapeDtypeStruct((bg, NSA_D, seq), bf16),
                   jax.ShapeDtypeStruct((bg, seq + WINDOW, NSA_D), bf16),
                   jax.ShapeDtypeStruct((bg, NSA_D, seq + WINDOW), bf16)],
        compiler_params=_cp(("parallel", "parallel", "arbitrary")),
        name="nsa_kv_prep",
    )(p, p, p, p, k_norm[1].astype(f32).reshape(1, NSA_D), k_norm[2].astype(f32).reshape(1, NSA_D))


def _stack_q(q_ref, qn_ref):
    parts = []
    for r in range(NSA_R):
        qn = _group_rms(q_ref[:, r * NSA_D:(r + 1) * NSA_D], qn_ref[...])
        parts.append((qn * (NSA_D ** -0.5)).astype(bf16))
    return jnp.concatenate(parts, axis=0)


def _head_slope(g, r):
    return jnp.where(g == 0, 1.0, 2.0 ** -NSA_R).astype(f32) * (2.0 ** -(r + 1))


def _nsa_cmp_kernel(q_ref, qn_ref, kcc_ref, vcc_ref, ovt_ref, oc_ref, selt_ref, s_scr, p_scr, psum_scr,
                    *, tq, n_cmp, n_sel):
    g = pl.program_id(1)
    q0 = pl.program_id(2) * tq
    qs = _stack_q(q_ref, qn_ref)
    s_scr[...] = lax.dot_general(kcc_ref[0], qs, (((1,), (1,)), ((), ())), preferred_element_type=f32)
    rc = CMP_ROWS
    n_chunk = n_cmp // rc
    n_vis = jnp.clip((q0 + tq - CMP_LEN) // CMP_STRIDE + 1, 0, n_cmp)
    n_act = (n_vis + rc - 1) // rc
    tcol = lax.broadcasted_iota(jnp.int32, (rc, tq), 1)
    nrow = lax.broadcasted_iota(jnp.int32, (rc, tq), 0)

    def rel_end(c):
        return (c * rc + nrow) * CMP_STRIDE + (CMP_LEN - 1 - q0)

    for r in range(NSA_R):
        cs = slice(r * tq, (r + 1) * tq)
        slope = _head_slope(g, r)

        def biased(c, m):
            r0 = pl.multiple_of(c * rc, rc)
            rel = rel_end(c)
            sb = jnp.where(rel <= tcol, s_scr[pl.ds(r0, rc), cs] + slope * rel.astype(f32), NEG)
            s_scr[pl.ds(r0, rc), cs] = sb
            return jnp.maximum(m, jnp.max(sb, axis=0, keepdims=True))

        m = lax.fori_loop(0, n_act, biased, jnp.full((1, tq), NEG, f32))
        m = jnp.maximum(m, 0.1 * NEG)

        def expo(c, l):
            r0 = pl.multiple_of(c * rc, rc)
            e = jnp.exp(s_scr[pl.ds(r0, rc), cs] - m)
            s_scr[pl.ds(r0, rc), cs] = e
            return l + jnp.sum(e, axis=0, keepdims=True)

        l = lax.fori_loop(0, n_act, expo, jnp.zeros((1, tq), f32))
        inv = 1.0 / jnp.maximum(l, 1e-30)

        def normalise(c, carry):
            r0 = pl.multiple_of(c * rc, rc)
            pn = s_scr[pl.ds(r0, rc), cs] * inv
            p_scr[pl.ds(r0, rc), cs] = pn.astype(bf16)
            if r == 0:
                psum_scr[pl.ds(r0, rc), :] = pn
            else:
                psum_scr[pl.ds(r0, rc), :] += pn
            return carry

        lax.fori_loop(0, n_act, normalise, 0)

    def clear(c, carry):
        r0 = pl.multiple_of(c * rc, rc)
        p_scr[pl.ds(r0, rc), :] = jnp.zeros((rc, NSA_R * tq), bf16)
        psum_scr[pl.ds(r0, rc), :] = jnp.zeros((rc, tq), f32)
        return carry

    lax.fori_loop(n_act, n_chunk, clear, 0)

    vcc = vcc_ref[0]
    for r in range(NSA_R):
        oc_ref[:, r * NSA_D:(r + 1) * NSA_D] = lax.dot_general(
            p_scr[:, r * tq:(r + 1) * tq], vcc, (((0,), (0,)), ((), ())), preferred_element_type=f32)

    p_hi, p_lo = _split_bf16(psum_scr[...])
    imp = (jnp.dot(ovt_ref[...], p_hi, preferred_element_type=f32)
           + jnp.dot(ovt_ref[...], p_lo, preferred_element_type=f32))
    blk = lax.broadcasted_iota(jnp.int32, (n_sel, tq), 0)
    tblk = (q0 + lax.broadcasted_iota(jnp.int32, (n_sel, tq), 1)) // SEL_BLOCK
    lag = tblk - blk
    score = jnp.where(blk == 0, BIG, jnp.where(lag < SEL_LOCAL, BIG, imp))
    score = jnp.where(lag >= 0, score, -1.0)
    sel = jnp.zeros((n_sel, tq), f32)
    for _ in range(SEL_TOPK):
        m = jnp.max(score, axis=0, keepdims=True)
        first = jnp.min(jnp.where(score == m, blk, n_sel), axis=0, keepdims=True)
        hit = blk == first
        sel = jnp.where(hit, 1.0, sel)
        score = jnp.where(hit, -2.0, score)
    selt_ref[0] = sel.astype(bf16)


def _nsa_cmp(p, q_norm, kcc, vcc, batch, seq):
    tq = TQ
    nq = seq // tq
    n_cmp = seq // CMP_STRIDE
    n_sel = seq // SEL_BLOCK
    ci = np.arange(n_cmp)[None, :] * CMP_STRIDE
    sj = np.arange(n_sel)[:, None] * SEL_BLOCK
    ovt = np.clip(np.minimum(ci + CMP_LEN, sj + SEL_BLOCK) - np.maximum(ci, sj), 0, None) / CMP_LEN
    ovt = jnp.asarray(ovt, dtype=bf16)
    t = batch * seq
    hw = NSA_R * NSA_D
    return pl.pallas_call(
        functools.partial(_nsa_cmp_kernel, tq=tq, n_cmp=n_cmp, n_sel=n_sel),
        grid=(batch, NSA_G, nq),
        in_specs=[pl.BlockSpec((tq, hw), lambda b, g, i: (b * nq + i, OFF_NQ // hw + g)),
                  pl.BlockSpec((1, NSA_D), lambda b, g, i: (0, 0)),
                  pl.BlockSpec((1, n_cmp, NSA_D), lambda b, g, i: (b * NSA_G + g, 0, 0)),
                  pl.BlockSpec((1, n_cmp, NSA_D), lambda b, g, i: (b * NSA_G + g, 0, 0)),
                  pl.BlockSpec((n_sel, n_cmp), lambda b, g, i: (0, 0))],
        out_specs=[pl.BlockSpec((tq, hw), lambda b, g, i: (b * nq + i, g)),
                   pl.BlockSpec((1, n_sel, tq), lambda b, g, i: (b * NSA_G + g, 0, i))],
        out_shape=[jax.ShapeDtypeStruct((t, NSA_G * hw), f32),
                   jax.ShapeDtypeStruct((batch * NSA_G, n_sel, seq), bf16)],
        scratch_shapes=[pltpu.VMEM((n_cmp, NSA_R * tq), f32), pltpu.VMEM((n_cmp, NSA_R * tq), bf16),
                        pltpu.VMEM((n_cmp, tq), f32)],
        compiler_params=_cp(("parallel", "parallel", "parallel")),
        name="nsa_cmp_select",
    )(p, q_norm.astype(f32).reshape(1, NSA_D), kcc, vcc, ovt)


def _nsa_attn_kernel(flag_ref, q_ref, qn_ref, misc_ref, oc_ref, selt_ref, ks_ref, vst_ref, kw_ref, vwt_ref, gn_ref,
                     o_ref, m_scr, l_scr, acc_scr, *, tq, tk, nkt):
    b = pl.program_id(0)
    g = pl.program_id(1)
    qi = pl.program_id(2)
    nq = pl.num_programs(2)
    q0 = qi * tq
    cols = NSA_R * tq
    qs = _stack_q(q_ref, qn_ref)
    nt_dims = (((1,), (1,)), ((), ()))

    bpt = tk // SEL_BLOCK
    krow = lax.broadcasted_iota(jnp.int32, (tk, tq), 0)
    tcol = lax.broadcasted_iota(jnp.int32, (tk, tq), 1)
    kt_last = (q0 + tq - 1) // tk

    m_scr[...] = jnp.full((1, cols), NEG, f32)
    l_scr[...] = jnp.zeros((1, cols), f32)
    acc_scr[...] = jnp.zeros((NSA_D, cols), f32)

    def sel_step(kt, causal):
        k0 = pl.multiple_of(kt * tk, tk)
        kb = ks_ref[0, pl.ds(k0, tk), :]
        vtb = vst_ref[0, :, pl.ds(k0, tk)]
        st = lax.dot_general(kb, qs, nt_dims, preferred_element_type=f32)
        picked = selt_ref[0, pl.ds(pl.multiple_of(kt * bpt, bpt), bpt), :]
        off = (picked.astype(f32) - 1.0) * (-NEG)
        bias = jnp.concatenate([jnp.broadcast_to(off[u:u + 1, :], (SEL_BLOCK, tq)) for u in range(bpt)], axis=0)
        rel = krow + (k0 - q0)
        if causal:
            bias = bias + jnp.where(rel <= tcol, 0.0, NEG)
        relf = rel.astype(f32)
        for r in range(NSA_R):
            cs = slice(r * tq, (r + 1) * tq)
            sr = st[:, cs] + _head_slope(g, r) * relf + bias
            m_old = m_scr[:, cs]
            m_new = jnp.maximum(m_old, jnp.max(sr, axis=0, keepdims=True))
            alpha = jnp.exp(m_old - m_new)
            pr = jnp.exp(sr - m_new)
            l_scr[:, cs] = alpha * l_scr[:, cs] + jnp.sum(pr, axis=0, keepdims=True)
            acc_scr[:, cs] = alpha * acc_scr[:, cs] + jnp.dot(vtb, pr.astype(bf16), preferred_element_type=f32)
            m_scr[:, cs] = m_new

    fbase = ((b * NSA_G + g) * nq + qi) * nkt

    def sel_loop(kt, carry):
        @pl.when(flag_ref[fbase + kt] > 0)
        def _():
            sel_step(kt, False)
        return carry

    lax.fori_loop(0, kt_last, sel_loop, 0)
    sel_step(kt_last, True)
    o_st = acc_scr[...] * (1.0 / jnp.maximum(l_scr[...], 1e-30))

    wk = WINDOW + tq
    w0 = pl.multiple_of(q0, tq)
    kwin = kw_ref[0, pl.ds(w0, wk), :]
    vwt = vwt_ref[0, :, pl.ds(w0, wk)]
    swt = lax.dot_general(kwin, qs, nt_dims, preferred_element_type=f32)
    jr = lax.broadcasted_iota(jnp.int32, (wk, tq), 0)
    tc = lax.broadcasted_iota(jnp.int32, (wk, tq), 1)
    wbias = jnp.where(jr > tc, jnp.where(jr <= tc + WINDOW, 0.0, NEG), NEG)
    wbias = wbias + jnp.where(jr >= WINDOW - q0, 0.0, NEG)
    jf = jr.astype(f32)
    o_wt = []
    for r in range(NSA_R):
        sr = swt[:, r * tq:(r + 1) * tq] + _head_slope(g, r) * jf + wbias
        m = jnp.max(sr, axis=0, keepdims=True)
        pr = jnp.exp(sr - m)
        den = jnp.sum(pr, axis=0, keepdims=True)
        o_wt.append(jnp.dot(vwt, pr.astype(bf16), preferred_element_type=f32) * (1.0 / den))

    for r in range(NSA_R):
        lanes = []
        for c in range(3):
            l0 = GLA_LR + r * 3 + c
            l1 = l0 + NSA_R * 3
            lanes.append(jnp.where(g == 0, misc_ref[:, l0:l0 + 1], misc_ref[:, l1:l1 + 1]))
        gc, gs, gw = [1.0 / (1.0 + jnp.exp(-x)) for x in lanes]
        hs = slice(r * NSA_D, (r + 1) * NSA_D)
        y = gc * oc_ref[:, hs] + gs * o_st[:, r * tq:(r + 1) * tq].T + gw * o_wt[r].T
        o_ref[:, hs] = _group_rms(y, gn_ref[:, hs]).astype(bf16)


def _nsa_attn(p, q_norm, o_c, selt, flags, ks, vst, kw, vwt, gain, batch, seq):
    tq, tk = TQ, TK
    nq = seq // tq
    nkt = seq // tk
    n_sel = seq // SEL_BLOCK
    t = batch * seq
    hw = NSA_R * NSA_D
    slab = lambda rows: pl.BlockSpec((1, rows, NSA_D), lambda b, g, i, f: (b * NSA_G + g, 0, 0),
                                     pipeline_mode=pl.Buffered(1))
    slab_t = lambda rows: pl.BlockSpec((1, NSA_D, rows), lambda b, g, i, f: (b * NSA_G + g, 0, 0),
                                       pipeline_mode=pl.Buffered(1))
    grid_spec = pltpu.PrefetchScalarGridSpec(
        num_scalar_prefetch=1,
        grid=(batch, NSA_G, nq),
        in_specs=[pl.BlockSpec((tq, hw), lambda b, g, i, f: (b * nq + i, OFF_NQ // hw + g)),
                  pl.BlockSpec((1, NSA_D), lambda b, g, i, f: (0, 0)),
                  pl.BlockSpec((tq, 128), lambda b, g, i, f: (b * nq + i, OFF_MISC // 128)),
                  pl.BlockSpec((tq, hw), lambda b, g, i, f: (b * nq + i, g)),
                  pl.BlockSpec((1, n_sel, tq), lambda b, g, i, f: (b * NSA_G + g, 0, i)),
                  slab(seq), slab_t(seq), slab(seq + WINDOW), slab_t(seq + WINDOW),
                  pl.BlockSpec((1, hw), lambda b, g, i, f: (0, g))],
        out_specs=pl.BlockSpec((tq, hw), lambda b, g, i, f: (b * nq + i, g)),
        scratch_shapes=[pltpu.VMEM((1, NSA_R * tq), f32), pltpu.VMEM((1, NSA_R * tq), f32),
                        pltpu.VMEM((NSA_D, NSA_R * tq), f32)],
    )
    return pl.pallas_call(
        functools.partial(_nsa_attn_kernel, tq=tq, tk=tk, nkt=nkt),
        grid_spec=grid_spec,
        out_shape=jax.ShapeDtypeStruct((t, NSA_G * hw), bf16),
        compiler_params=_cp(("parallel", "parallel", "arbitrary")),
        name="nsa_attention",
    )(flags, p, q_norm.astype(f32).reshape(1, NSA_D), p, o_c, selt, ks, vst, kw, vwt, gain.reshape(1, -1))


def _out_proj_kernel(a_ref, b_ref, c_ref, wa_ref, wb_ref, wc_ref, h_ref, o_ref):
    o_ref[...] = (h_ref[...]
                  + jnp.dot(a_ref[...], wa_ref[...], preferred_element_type=f32)
                  + jnp.dot(b_ref[...], wb_ref[...], preferred_element_type=f32)
                  + jnp.dot(c_ref[...], wc_ref[...], preferred_element_type=f32))


def _out_proj(ya, yb, yc, w_out, h):
    t, d = h.shape
    tm, tn = TM_PROJ, 1024
    w = w_out.astype(bf16)
    ka, kb, kc = ya.shape[1], yb.shape[1], yc.shape[1]
    return pl.pallas_call(
        _out_proj_kernel,
        grid=(t // tm, d // tn),
        in_specs=[pl.BlockSpec((tm, ka), lambda i, j: (i, 0)),
                  pl.BlockSpec((tm, kb), lambda i, j: (i, 0)),
                  pl.BlockSpec((tm, kc), lambda i, j: (i, 0)),
                  pl.BlockSpec((ka, tn), lambda i, j: (0, j)),
                  pl.BlockSpec((kb, tn), lambda i, j: (1, j)),
                  pl.BlockSpec((kc, tn), lambda i, j: (1, j)),
                  pl.BlockSpec((tm, tn), lambda i, j: (i, j))],
        out_specs=pl.BlockSpec((tm, tn), lambda i, j: (i, j)),
        out_shape=jax.ShapeDtypeStruct((t, d), f32),
        compiler_params=_cp(("parallel", "parallel")),
        name="out_proj",
    )(ya, yb, yc, w, w, w, h)


def _top_k_columns(v, k, ids=None, want_rank=True):
    if ids is None:
        ids = lax.broadcasted_iota(jnp.int32, v.shape, 0)
    big = jnp.int32(2 ** 30)
    kidx = lax.broadcasted_iota(jnp.int32, (k, v.shape[1]), 0)
    vals = jnp.zeros((k, v.shape[1]), f32)
    firsts = []
    rank = jnp.full(v.shape, float(k), f32) if want_rank else None
    for kk in range(k):
        m = jnp.max(v, axis=0, keepdims=True)
        first = jnp.min(jnp.where(v == m, ids, big), axis=0, keepdims=True)
        hit = ids == first
        if want_rank:
            rank = jnp.where(hit, float(kk), rank)
        v = jnp.where(hit, -jnp.inf, v)
        vals = jnp.where(kidx == kk, m, vals)
        firsts.append(first)
    return vals, rank, firsts


def _pair_candidates(t1, t2):
    n = t1.shape[1]
    k = PEER_TOPK
    i16 = lax.broadcasted_iota(jnp.int32, (k, n), 0)
    i8 = lax.broadcasted_iota(jnp.int32, (8, n), 0)
    vals = [t1[0:1, :] + t2, t1[1:2, :] + t2[0:8, :]]
    pos = [i16, k + i8]
    for a in range(2, 8):
        vals.append(jnp.where(i8 < k // (a + 1), t1[a:a + 1, :] + t2[0:8, :], -jnp.inf))
        pos.append(a * k + i8)
    vals.append(t1[8:16, :] + t2[0:1, :])
    pos.append((8 + i8) * k)
    return jnp.concatenate(vals, axis=0), jnp.concatenate(pos, axis=0)


def _peer_route_kernel(q_ref, key_ref, a_ref, cnt_ref, rank2_ref, e2_ref):
    tm = q_ref.shape[0]
    for h in range(PEER_H):
        sc, tops, ranks = [], [], []
        for half in range(2):
            c0 = (h * 2 + half) * PEER_NK
            qh = q_ref[:, c0:c0 + PEER_NK].astype(bf16)
            s = lax.dot_general(key_ref[h * 2 + half], qh, (((1,), (1,)), ((), ())),
                                preferred_element_type=f32)
            vals, rank, _ = _top_k_columns(s, PEER_TOPK)
            sc.append(s)
            tops.append(vals)
            ranks.append(rank)
        cand, cand_pos = _pair_candidates(tops[0], tops[1])
        best, _, pos = _top_k_columns(cand, PEER_TOPK, ids=cand_pos, want_rank=False)
        mass = jnp.sum(jnp.exp(best - best[0:1, :]), axis=0, keepdims=True)
        aidx = lax.broadcasted_iota(jnp.int32, (PEER_TOPK, tm), 0)
        per_rank = jnp.zeros((PEER_TOPK, tm), f32)
        for p in pos:
            per_rank = per_rank + jnp.where(aidx == p // PEER_TOPK, 1.0, 0.0)
        cnt = jnp.zeros((PEER_NK, tm), f32)
        for a in range(PEER_TOPK):
            cnt = cnt + jnp.where(ranks[0] == float(a), per_rank[a:a + 1, :], 0.0)
        hs = slice(h * PEER_NK, (h + 1) * PEER_NK)
        a_ref[hs, :] = (jnp.exp(sc[0] - tops[0][0:1, :]) * (0.5 / mass)).astype(bf16)
        cnt_ref[hs, :] = cnt.astype(bf16)
        rank2_ref[hs, :] = ranks[1].astype(bf16)
        e2_ref[hs, :] = jnp.exp(sc[1] - tops[1][0:1, :]).astype(bf16)


def _peer_route(qry, keys):
    t = qry.shape[0]
    tm = TM_ROUTE
    narrow = jax.ShapeDtypeStruct((PEER_H * PEER_NK, t), bf16)
    nspec = pl.BlockSpec((PEER_H * PEER_NK, tm), lambda i: (0, i))
    return pl.pallas_call(
        _peer_route_kernel,
        grid=(t // tm,),
        in_specs=[pl.BlockSpec((tm, PEER_H * 2 * PEER_NK), lambda i: (i, 0)),
                  pl.BlockSpec((PEER_H * 2, PEER_NK, PEER_NK), lambda i: (0, 0, 0))],
        out_specs=[nspec] * 4,
        out_shape=[narrow] * 4,
        compiler_params=_cp(("parallel",)),
        name="peer_route",
    )(qry, keys.astype(bf16).reshape(PEER_H * 2, PEER_NK, PEER_NK))


def _peer_expert_kernel(xn_ref, u_ref, v_ref, a_ref, cnt_ref, rank2_ref, e2_ref, h_ref, o_ref,
                        act_ref, gt_ref, arow_ref, crow_ref, *, tm, te):
    j = pl.program_id(1)

    @pl.when(j == 0)
    def _():
        o_ref[...] = h_ref[...]

    n_first = te // PEER_NK
    assert SUBLANES % n_first == 0
    per_group = SUBLANES // n_first
    grp = pl.multiple_of((j // per_group) * SUBLANES, SUBLANES)
    part = j % per_group

    def tile_rows(ref, h):
        blk = ref[pl.ds(pl.multiple_of(h * PEER_NK + grp, SUBLANES), SUBLANES), :]
        rows = blk[0:n_first, :]
        for q in range(1, per_group):
            rows = jnp.where(part == q, blk[q * n_first:(q + 1) * n_first, :], rows)
        return rows

    a_rows = [tile_rows(a_ref, h) for h in range(PEER_H)]
    c_rows = [tile_rows(cnt_ref, h) for h in range(PEER_H)]
    for h in range(PEER_H):
        arow_ref[h * n_first:(h + 1) * n_first, :] = a_rows[h]
        crow_ref[h * n_first:(h + 1) * n_first, :] = c_rows[h]

    x = lax.dot_general(u_ref[...], xn_ref[...], (((1,), (1,)), ((), ())), preferred_element_type=f32)
    act_ref[...] = (x * (1.0 + jnp.tanh(x * (0.7978845608028654 + 0.035677408136300125 * (x * x))))).astype(bf16)

    zero = jnp.zeros((PEER_NK, BF16_LANES), bf16)
    for c in range(tm // BF16_LANES):
        ls = slice(c * BF16_LANES, (c + 1) * BF16_LANES)
        for u in range(n_first):
            rs = slice(u * PEER_NK, (u + 1) * PEER_NK)
            w = zero
            for h in range(PEER_H):
                hs = slice(h * PEER_NK, (h + 1) * PEER_NK)
                a = arow_ref[h * n_first + u:h * n_first + u + 1, ls]
                cnt = crow_ref[h * n_first + u:h * n_first + u + 1, ls]
                keep = rank2_ref[hs, ls] < cnt
                w = w + jnp.where(keep, e2_ref[hs, ls], zero) * a
            gt_ref[rs, ls] = w * act_ref[rs, ls]

    o_ref[...] += lax.dot_general(gt_ref[...], v_ref[...], (((0,), (0,)), ((), ())), preferred_element_type=f32)


def _peer_experts(xn, u, v, a, cnt, rank2, e2, h):
    t, d = h.shape
    tm, te = TM_PEER, TE_PEER
    once = pl.Buffered(1)
    nspec = pl.BlockSpec((PEER_H * PEER_NK, tm), lambda i, j: (0, i))
    return pl.pallas_call(
        functools.partial(_peer_expert_kernel, tm=tm, te=te),
        grid=(t // tm, PEER_E // te),
        in_specs=[pl.BlockSpec((tm, d), lambda i, j: (i, 0), pipeline_mode=once),
                  pl.BlockSpec((te, d), lambda i, j: (j, 0)),
                  pl.BlockSpec((te, d), lambda i, j: (j, 0)),
                  nspec, nspec, nspec, nspec,
                  pl.BlockSpec((tm, d), lambda i, j: (i, 0), pipeline_mode=once)],
        out_specs=pl.BlockSpec((tm, d), lambda i, j: (i, 0)),
        out_shape=jax.ShapeDtypeStruct((t, d), f32),
        scratch_shapes=[pltpu.VMEM((te, tm), bf16), pltpu.VMEM((te, tm), bf16),
                        pltpu.VMEM((PEER_H * te // PEER_NK, tm), bf16),
                        pltpu.VMEM((PEER_H * te // PEER_NK, tm), bf16)],
        compiler_params=_cp(("parallel", "arbitrary"), PEER_EXPERT_FLAGS),
        name="peer_experts",
    )(xn, u, v, a, cnt, rank2, e2, h)


def _permute_w_in(w):
    pad = jnp.zeros((w.shape[0], N_IN - ORIG_END), w.dtype)
    return jnp.concatenate([w[:, :ORIG_GA], w[:, ORIG_NQ:ORIG_GATE], w[:, ORIG_GA:ORIG_NQ],
                            w[:, ORIG_GATE:ORIG_END], pad], axis=1).astype(bf16)


def _tile_flags(selt, seq):
    bg = selt.shape[0]
    any_sel = selt.reshape(bg, seq // TK, TK // SEL_BLOCK, seq // TQ, TQ).max(axis=(2, 4))
    return (jnp.swapaxes(any_sel, 1, 2) > 0.5).astype(jnp.int32).reshape(-1)


def _mixers(h, norm_mix, w_in, conv_w, gla_w_a2, gla_b_a, q_norm, k_norm, cmp_pos, cmp_w1, cmp_w2, out_norm,
            batch, seq):
    p = _norm_matmul(h, norm_mix.astype(f32), _permute_w_in(w_in), TN_IN)
    gain = out_norm.astype(f32)
    y_conv = _conv_mixer(p, conv_w, gain[:CONV_W], seq)
    y_gla = _gla_mixer(p, gla_w_a2, gla_b_a, gain[CONV_W:2 * CONV_W], batch, seq)
    kcc = _compress(p, OFF_NKC, cmp_pos[0], cmp_w1[0], cmp_w2[0], k_norm[0], batch, seq, True)
    vcc = _compress(p, OFF_NVC, cmp_pos[1], cmp_w1[1], cmp_w2[1], k_norm[0], batch, seq, False)
    ks, vs, kw, vw = _kv_prep(p, k_norm, batch, seq)
    o_c, selt = _nsa_cmp(p, q_norm, kcc, vcc, batch, seq)
    y_nsa = _nsa_attn(p, q_norm, o_c, selt, _tile_flags(selt, seq), ks, vs, kw, vw, gain[2 * CONV_W:], batch, seq)
    return y_conv, y_gla, y_nsa


def _layer(h, norm_mix, w_in, conv_w, gla_w_a2, gla_b_a, q_norm, k_norm, cmp_pos, cmp_w1, cmp_w2, out_norm, w_out,
           norm_ffn, peer_w_q, peer_keys, peer_u, peer_v, batch, seq):
    y_conv, y_gla, y_nsa = _mixers(h, norm_mix, w_in, conv_w, gla_w_a2, gla_b_a, q_norm, k_norm, cmp_pos, cmp_w1,
                                   cmp_w2, out_norm, batch, seq)
    h = _out_proj(y_conv, y_gla, y_nsa, w_out, h)
    qry, xn = _norm_matmul(h, norm_ffn.astype(f32), peer_w_q.astype(bf16), 1024, emit_xn=True)
    a, cnt, rank2, e2 = _peer_route(qry, peer_keys)
    return _peer_experts(xn, peer_u.astype(bf16), peer_v.astype(bf16), a, cnt, rank2, e2, h)


def kernel(x, norm_mix, w_in, conv_w, gla_w_a2, gla_b_a, nsa_q_norm, nsa_k_norm, nsa_cmp_pos, nsa_cmp_w1,
           nsa_cmp_w2, out_norm, w_out, norm_ffn, peer_w_q, peer_keys, peer_u, peer_v):
    batch, seq, d = x.shape
    h = x.reshape(batch * seq, d)
    for l in range(w_in.shape[0]):
        h = _layer(h, norm_mix[l], w_in[l], conv_w[l], gla_w_a2[l], gla_b_a[l], nsa_q_norm[l], nsa_k_norm[l],
                   nsa_cmp_pos[l], nsa_cmp_w1[l], nsa_cmp_w2[l], out_norm[l], w_out[l], norm_ffn[l], peer_w_q[l],
                   peer_keys[l], peer_u[l], peer_v[l], batch, seq)
    return h.reshape(batch, seq, d)
```

```python
import functools
import math

import numpy as np
import jax
import jax.numpy as jnp
from jax import lax
from jax.experimental import pallas as pl
from jax.experimental.pallas import tpu as pltpu

f32 = jnp.float32
bf16 = jnp.bfloat16

D_MODEL = 2048
EPS = 1e-6
HEAD = 128
CONV_W = 512
CONV_K = 3
GLA_H, GLA_DK, GLA_DV = 4, 64, 128
GLA_LR = 16
GLA_TAU = 16.0
GLA_CHUNK = 64
NSA_G, NSA_R, NSA_D = 2, 4, 128
CMP_LEN, CMP_STRIDE = 32, 16
SEL_BLOCK, SEL_TOPK, SEL_LOCAL = 64, 16, 2
WINDOW = 512
BIG = 1e9
PEER_H, PEER_NK, PEER_TOPK = 8, 128, 16
PEER_E = PEER_NK * PEER_NK
NEG = -1e30
BF16_LANES = 256

OFF_CX, OFF_CB, OFF_CC = 0, 512, 1024
OFF_GQ, OFF_GK, OFF_GV, OFF_GG = 1536, 1792, 2048, 2560
OFF_NQ = 3072
OFF_NKC, OFF_NVC, OFF_NKS, OFF_NVS, OFF_NKW, OFF_NVW = 4096, 4352, 4608, 4864, 5120, 5376
OFF_MISC = 5632
N_IN = 5760
ORIG_GA, ORIG_NQ, ORIG_GATE, ORIG_END = 3072, 3088, 5648, 5672

TM_PROJ = 1024
TN_IN = 1152
TS_ELT = 512
TC_GLA = 512
TQ = 128
TK = 512
CMP_ROWS = 256
TM_PEER = 512
TE_PEER = 1024
TM_ROUTE = 256
VMEM_LIMIT = 56 * 1024 * 1024
PEER_EXPERT_FLAGS = None


def _cp(sem, flags=None):
    return pltpu.CompilerParams(dimension_semantics=sem, vmem_limit_bytes=VMEM_LIMIT, flags=flags)


def _gelu(x):
    return 0.5 * x * (1.0 + jnp.tanh(0.7978845608028654 * (x + 0.044715 * (x * x * x))))


def _split_bf16(x):
    hi = x.astype(bf16)
    lo = (x - hi.astype(f32)).astype(bf16)
    return hi, lo


def _group_rms(y, gain):
    ms = jnp.mean(y * y, axis=-1, keepdims=True)
    return y * lax.rsqrt(ms + EPS) * gain


def _norm_matmul_kernel(x_ref, g_ref, w_ref, *refs, emit_xn):
    if emit_xn:
        o_ref, xo_ref, xn_ref = refs
    else:
        o_ref, xn_ref = refs

    @pl.when(pl.program_id(1) == 0)
    def _():
        x = x_ref[...]
        ms = jnp.mean(x * x, axis=-1, keepdims=True)
        xn = (x * lax.rsqrt(ms + EPS) * g_ref[...]).astype(bf16)
        xn_ref[...] = xn
        if emit_xn:
            xo_ref[...] = xn

    o_ref[...] = jnp.dot(xn_ref[...], w_ref[...], preferred_element_type=f32)


def _norm_matmul(x, gain, w, tn, emit_xn=False):
    t, d = x.shape
    n = w.shape[1]
    tm = TM_PROJ
    out_shape = [jax.ShapeDtypeStruct((t, n), f32)]
    out_specs = [pl.BlockSpec((tm, tn), lambda i, j: (i, j))]
    if emit_xn:
        out_shape.append(jax.ShapeDtypeStruct((t, d), bf16))
        out_specs.append(pl.BlockSpec((tm, d), lambda i, j: (i, 0)))
    outs = pl.pallas_call(
        functools.partial(_norm_matmul_kernel, emit_xn=emit_xn),
        grid=(t // tm, n // tn),
        in_specs=[
            pl.BlockSpec((tm, d), lambda i, j: (i, 0)),
            pl.BlockSpec((1, d), lambda i, j: (0, 0)),
            pl.BlockSpec((d, tn), lambda i, j: (0, j)),
        ],
        out_specs=out_specs,
        out_shape=out_shape,
        scratch_shapes=[pltpu.VMEM((tm, d), bf16)],
        compiler_params=_cp(("parallel", "arbitrary")),
        name="norm_matmul",
    )(x, gain.reshape(1, d), w)
    return outs if emit_xn else outs[0]


def _conv_kernel(cx_ref, cb_ref, cc_ref, pcx_ref, pcc_ref, w_ref, g_ref, o_ref, z_ref, *, ts, tiles_per_seq):
    first = (pl.program_id(0) % tiles_per_seq) == 0
    zp = pcx_ref[...] * pcc_ref[...]
    z_ref[0:8, :] = jnp.where(first, 0.0, zp)
    z_ref[8:8 + ts, :] = cx_ref[...] * cc_ref[...]
    y = (z_ref[8:8 + ts, :] * w_ref[0:1, :]
         + z_ref[7:7 + ts, :] * w_ref[1:2, :]
         + z_ref[6:6 + ts, :] * w_ref[2:3, :])
    y = cb_ref[...] * y
    for c in range(CONV_W // HEAD):
        sl = slice(c * HEAD, (c + 1) * HEAD)
        o_ref[:, sl] = _group_rms(y[:, sl], g_ref[:, sl]).astype(bf16)


def _conv_mixer(p, conv_w, gain, seq):
    t = p.shape[0]
    ts = TS_ELT
    w_t = conv_w.astype(f32).T
    cur = lambda off: pl.BlockSpec((ts, CONV_W), lambda i, off=off: (i, off // CONV_W))
    prev = lambda off: pl.BlockSpec((8, CONV_W), lambda i, off=off: (jnp.maximum(i * (ts // 8) - 1, 0), off // CONV_W))
    return pl.pallas_call(
        functools.partial(_conv_kernel, ts=ts, tiles_per_seq=seq // ts),
        grid=(t // ts,),
        in_specs=[cur(OFF_CX), cur(OFF_CB), cur(OFF_CC), prev(OFF_CX), prev(OFF_CC),
                  pl.BlockSpec((CONV_K, CONV_W), lambda i: (0, 0)),
                  pl.BlockSpec((1, CONV_W), lambda i: (0, 0))],
        out_specs=pl.BlockSpec((ts, CONV_W), lambda i: (i, 0)),
        out_shape=jax.ShapeDtypeStruct((t, CONV_W), bf16),
        scratch_shapes=[pltpu.VMEM((ts + 8, CONV_W), f32)],
        compiler_params=_cp(("parallel",)),
        name="conv_mixer",
    )(p, p, p, p, p, w_t, gain.reshape(1, CONV_W))


def _gla_kernel(q_ref, k_ref, v_ref, gg_ref, misc_ref, wa_ref, ba_ref, gn_ref, o_ref,
                st_ref, la_ref, y_ref, *, tc):
    @pl.when(pl.program_id(1) == 0)
    def _():
        st_ref[...] = jnp.zeros_like(st_ref)

    a_hi, a_lo = _split_bf16(misc_ref[:, 0:GLA_LR])
    w_hi, w_lo = _split_bf16(wa_ref[...])
    z = (jnp.dot(a_hi, w_hi, preferred_element_type=f32)
         + jnp.dot(a_hi, w_lo, preferred_element_type=f32)
         + jnp.dot(a_lo, w_hi, preferred_element_type=f32)) + ba_ref[...]
    la_ref[...] = (jnp.minimum(z, 0.0) - jnp.log(1.0 + jnp.exp(-jnp.abs(z)))) * (1.0 / GLA_TAU)

    c = GLA_CHUNK
    row = lax.broadcasted_iota(jnp.int32, (c, c), 0)
    col = lax.broadcasted_iota(jnp.int32, (c, c), 1)
    causal = row >= col
    tril = causal.astype(bf16)
    scale = GLA_DK ** -0.5

    def chunk(ci, carry):
        r0 = pl.multiple_of(ci * c, c)
        la = la_ref[pl.ds(r0, c), :]
        la_hi, la_lo = _split_bf16(la)
        la_lo2 = (la - la_hi.astype(f32) - la_lo.astype(f32)).astype(bf16)
        bc = (jnp.dot(tril, la_hi, preferred_element_type=f32)
              + jnp.dot(tril, la_lo, preferred_element_type=f32)
              + jnp.dot(tril, la_lo2, preferred_element_type=f32))
        bl = bc[c - 1:c, :]
        qc = q_ref[pl.ds(r0, c), :]
        kc = k_ref[pl.ds(r0, c), :]
        vc = v_ref[pl.ds(r0, c), :].astype(bf16)
        q_dec = (qc * scale * jnp.exp(bc)).astype(bf16)
        k_inv = (kc * jnp.exp(-bc)).astype(bf16)
        k_end = (kc * jnp.exp(bl - bc)).astype(bf16)
        decay = jnp.exp(bl)
        for h in range(GLA_H):
            ks = slice(h * GLA_DK, (h + 1) * GLA_DK)
            vs = slice(h * GLA_DV, (h + 1) * GLA_DV)
            att = lax.dot_general(q_dec[:, ks], k_inv[:, ks], (((1,), (1,)), ((), ())),
                                  preferred_element_type=f32)
            att = jnp.where(causal, att, 0.0).astype(bf16)
            o = jnp.dot(att, vc[:, vs], preferred_element_type=f32)
            st = st_ref[h]
            o = o + lax.dot_general(q_dec[:, ks], st.astype(bf16), (((1,), (1,)), ((), ())),
                                    preferred_element_type=f32)
            y_ref[pl.ds(r0, c), vs] = o
            upd = lax.dot_general(vc[:, vs], k_end[:, ks], (((0,), (0,)), ((), ())),
                                  preferred_element_type=f32)
            st_ref[h] = st * decay[:, ks] + upd
        return carry

    lax.fori_loop(0, tc // c, chunk, 0)

    g = gg_ref[...]
    sw = g * (1.0 / (1.0 + jnp.exp(-g)))
    for h in range(GLA_H):
        vs = slice(h * GLA_DV, (h + 1) * GLA_DV)
        o_ref[:, vs] = (_group_rms(y_ref[:, vs], gn_ref[:, vs]) * sw[:, vs]).astype(bf16)


def _gla_mixer(p, w_a2, b_a, gain, batch, seq):
    t = p.shape[0]
    tc = TC_GLA
    nst = seq // tc
    blk = lambda width, off: pl.BlockSpec((tc, width), lambda b, s, off=off, width=width: (b * nst + s, off // width))
    const = lambda shape: pl.BlockSpec(shape, lambda b, s: (0,) * len(shape))
    return pl.pallas_call(
        functools.partial(_gla_kernel, tc=tc),
        grid=(batch, nst),
        in_specs=[blk(256, OFF_GQ), blk(256, OFF_GK), blk(512, OFF_GV), blk(512, OFF_GG), blk(128, OFF_MISC),
                  const((GLA_LR, GLA_H * GLA_DK)), const((1, GLA_H * GLA_DK)), const((1, GLA_H * GLA_DV))],
        out_specs=pl.BlockSpec((tc, GLA_H * GLA_DV), lambda b, s: (b * nst + s, 0)),
        out_shape=jax.ShapeDtypeStruct((t, GLA_H * GLA_DV), bf16),
        scratch_shapes=[pltpu.VMEM((GLA_H, GLA_DV, GLA_DK), f32),
                        pltpu.VMEM((tc, GLA_H * GLA_DK), f32),
                        pltpu.VMEM((tc, GLA_H * GLA_DV), f32)],
        compiler_params=_cp(("parallel", "arbitrary")),
        name="gla_mixer",
    )(p, p, p, p, p, w_a2.astype(f32), b_a.astype(f32).reshape(1, -1), gain.reshape(1, -1))


def _compress_kernel(t_ref, pos_ref, w1_ref, w2_ref, gn_ref, o_ref, b_scr, *, n_piece, apply_norm):
    half = CMP_STRIDE
    acc_a = jnp.zeros((n_piece, NSA_D), f32)
    acc_b = jnp.zeros((n_piece, NSA_D), f32)
    for r in range(half):
        rows = t_ref[pl.ds(r, n_piece, stride=CMP_STRIDE), :]
        xa = (rows + pos_ref[r:r + 1, :]).astype(bf16)
        acc_a = acc_a + jnp.dot(xa, w1_ref[r], preferred_element_type=f32)
        xb = (rows + pos_ref[half + r:half + r + 1, :]).astype(bf16)
        acc_b = acc_b + jnp.dot(xb, w1_ref[half + r], preferred_element_type=f32)
    b_scr[0:n_piece, :] = acc_b
    b_scr[n_piece:n_piece + 8, :] = jnp.zeros((8, NSA_D), f32)
    hid = acc_a + b_scr[1:n_piece + 1, :]
    out = jnp.dot(_gelu(hid).astype(bf16), w2_ref[...], preferred_element_type=f32)
    if apply_norm:
        out = _group_rms(out, gn_ref[...])
    o_ref[0] = out.astype(bf16)


def _compress(p, off, pos, w1, w2, gain, batch, seq, apply_norm):
    n_piece = seq // CMP_STRIDE
    return pl.pallas_call(
        functools.partial(_compress_kernel, n_piece=n_piece, apply_norm=apply_norm),
        grid=(batch, NSA_G),
        in_specs=[pl.BlockSpec((seq, NSA_D), lambda b, g: (b, off // NSA_D + g)),
                  pl.BlockSpec((CMP_LEN, NSA_D), lambda b, g: (0, 0)),
                  pl.BlockSpec((CMP_LEN, NSA_D, NSA_D), lambda b, g: (0, 0, 0)),
                  pl.BlockSpec((NSA_D, NSA_D), lambda b, g: (0, 0)),
                  pl.BlockSpec((1, NSA_D), lambda b, g: (0, 0))],
        out_specs=pl.BlockSpec((1, n_piece, NSA_D), lambda b, g: (b * NSA_G + g, 0, 0)),
        out_shape=jax.ShapeDtypeStruct((batch * NSA_G, n_piece, NSA_D), bf16),
        scratch_shapes=[pltpu.VMEM((n_piece + 8, NSA_D), f32)],
        compiler_params=_cp(("parallel", "parallel")),
        name="nsa_compress",
    )(p, pos.astype(f32), w1.astype(bf16).reshape(CMP_LEN, NSA_D, NSA_D), w2.astype(bf16),
      gain.astype(f32).reshape(1, NSA_D))


def _kvprep_kernel(ks_ref, vs_ref, kw_ref, vw_ref, gs_ref, gw_ref, oks_ref, ovs_ref, okw_ref, ovw_ref):
    lead = pl.program_id(2) == 0
    oks_ref[0] = _group_rms(ks_ref[...], gs_ref[...]).astype(bf16)
    ovs_ref[0] = vs_ref[...].T.astype(bf16)
    okw_ref[0] = jnp.where(lead, 0.0, _group_rms(kw_ref[...], gw_ref[...])).astype(bf16)
    ovw_ref[0] = jnp.where(lead, 0.0, vw_ref[...]).T.astype(bf16)


def _kv_prep(p, k_norm, batch, seq):
    ts = WINDOW
    nst = seq // ts
    src = lambda s: jnp.maximum(s - 1, 0)
    blk = lambda off: pl.BlockSpec((ts, NSA_D), lambda b, g, s, off=off: (b * nst + src(s), off // NSA_D + g))
    gspec = pl.BlockSpec((1, NSA_D), lambda b, g, s: (0, 0))
    kspec = pl.BlockSpec((1, ts, NSA_D), lambda b, g, s: (b * NSA_G + g, src(s), 0))
    vspec = pl.BlockSpec((1, NSA_D, ts), lambda b, g, s: (b * NSA_G + g, 0, src(s)))
    kwspec = pl.BlockSpec((1, ts, NSA_D), lambda b, g, s: (b * NSA_G + g, s, 0))
    vwspec = pl.BlockSpec((1, NSA_D, ts), lambda b, g, s: (b * NSA_G + g, 0, s))
    bg = batch * NSA_G
    return pl.pallas_call(
        _kvprep_kernel,
        grid=(batch, NSA_G, nst + 1),
        in_specs=[blk(OFF_NKS), blk(OFF_NVS), blk(OFF_NKW), blk(OFF_NVW), gspec, gspec],
        out_specs=[kspec, vspec, kwspec, vwspec],
        out_shape=[jax.ShapeDtypeStruct((bg, seq, NSA_D), bf16), jax.ShapeDtypeStruct((bg, NSA_D, seq), bf16),
                   jax.ShapeDtypeStruct((bg, seq + WINDOW, NSA_D), bf16),
                   jax.ShapeDtypeStruct((bg, NSA_D, seq + WINDOW), bf16)],
        compiler_params=_cp(("parallel", "parallel", "arbitrary")),
        name="nsa_kv_prep",
    )(p, p, p, p, k_norm[1].astype(f32).reshape(1, NSA_D), k_norm[2].astype(f32).reshape(1, NSA_D))


def _stack_q(q_ref, qn_ref):
    parts = []
    for r in range(NSA_R):
        qn = _group_rms(q_ref[:, r * NSA_D:(r + 1) * NSA_D], qn_ref[...])
        parts.append((qn * (NSA_D ** -0.5)).astype(bf16))
    return jnp.concatenate(parts, axis=0)


def _head_slope(g, r):
    return jnp.where(g == 0, 1.0, 2.0 ** -NSA_R).astype(f32) * (2.0 ** -(r + 1))


def _nsa_cmp_kernel(q_ref, qn_ref, kcc_ref, vcc_ref, ovt_ref, oc_ref, selt_ref, s_scr, p_scr, psum_scr,
                    *, tq, n_cmp, n_sel):
    g = pl.program_id(1)
    q0 = pl.program_id(2) * tq
    qs = _stack_q(q_ref, qn_ref)
    s_scr[...] = lax.dot_general(kcc_ref[0], qs, (((1,), (1,)), ((), ())), preferred_element_type=f32)
    rc = CMP_ROWS
    n_chunk = n_cmp // rc
    n_vis = jnp.clip((q0 + tq - CMP_LEN) // CMP_STRIDE + 1, 0, n_cmp)
    n_act = (n_vis + rc - 1) // rc
    tcol = lax.broadcasted_iota(jnp.int32, (rc, tq), 1)
    nrow = lax.broadcasted_iota(jnp.int32, (rc, tq), 0)

    def rel_end(c):
        return (c * rc + nrow) * CMP_STRIDE + (CMP_LEN - 1 - q0)

    for r in range(NSA_R):
        cs = slice(r * tq, (r + 1) * tq)
        slope = _head_slope(g, r)

        def biased(c, m):
            r0 = pl.multiple_of(c * rc, rc)
            rel = rel_end(c)
            sb = jnp.where(rel <= tcol, s_scr[pl.ds(r0, rc), cs] + slope * rel.astype(f32), NEG)
            s_scr[pl.ds(r0, rc), cs] = sb
            return jnp.maximum(m, jnp.max(sb, axis=0, keepdims=True))

        m = lax.fori_loop(0, n_act, biased, jnp.full((1, tq), NEG, f32))
        m = jnp.maximum(m, 0.1 * NEG)

        def expo(c, l):
            r0 = pl.multiple_of(c * rc, rc)
            e = jnp.exp(s_scr[pl.ds(r0, rc), cs] - m)
            s_scr[pl.ds(r0, rc), cs] = e
            return l + jnp.sum(e, axis=0, keepdims=True)

        l = lax.fori_loop(0, n_act, expo, jnp.zeros((1, tq), f32))
        inv = 1.0 / jnp.maximum(l, 1e-30)

        def normalise(c, carry):
            r0 = pl.multiple_of(c * rc, rc)
            pn = s_scr[pl.ds(r0, rc), cs] * inv
            p_scr[pl.ds(r0, rc), cs] = pn.astype(bf16)
            if r == 0:
                psum_scr[pl.ds(r0, rc), :] = pn
            else:
                psum_scr[pl.ds(r0, rc), :] += pn
            return carry

        lax.fori_loop(0, n_act, normalise, 0)

    def clear(c, carry):
        r0 = pl.multiple_of(c * rc, rc)
        p_scr[pl.ds(r0, rc), :] = jnp.zeros((rc, NSA_R * tq), bf16)
        psum_scr[pl.ds(r0, rc), :] = jnp.zeros((rc, tq), f32)
        return carry

    lax.fori_loop(n_act, n_chunk, clear, 0)

    vcc = vcc_ref[0]
    for r in range(NSA_R):
        oc_ref[:, r * NSA_D:(r + 1) * NSA_D] = lax.dot_general(
            p_scr[:, r * tq:(r + 1) * tq], vcc, (((0,), (0,)), ((), ())), preferred_element_type=f32)

    p_hi, p_lo = _split_bf16(psum_scr[...])
    imp = (jnp.dot(ovt_ref[...], p_hi, preferred_element_type=f32)
           + jnp.dot(ovt_ref[...], p_lo, preferred_element_type=f32))
    blk = lax.broadcasted_iota(jnp.int32, (n_sel, tq), 0)
    tblk = (q0 + lax.broadcasted_iota(jnp.int32, (n_sel, tq), 1)) // SEL_BLOCK
    lag = tblk - blk
    score = jnp.where(blk == 0, BIG, jnp.where(lag < SEL_LOCAL, BIG, imp))
    score = jnp.where(lag >= 0, score, -1.0)
    sel = jnp.zeros((n_sel, tq), f32)
    for _ in range(SEL_TOPK):
        m = jnp.max(score, axis=0, keepdims=True)
        first = jnp.min(jnp.where(score == m, blk, n_sel), axis=0, keepdims=True)
        hit = blk == first
        sel = jnp.where(hit, 1.0, sel)
        score = jnp.where(hit, -2.0, score)
    selt_ref[0] = sel.astype(bf16)


def _nsa_cmp(p, q_norm, kcc, vcc, batch, seq):
    tq = TQ
    nq = seq // tq
    n_cmp = seq // CMP_STRIDE
    n_sel = seq // SEL_BLOCK
    ci = np.arange(n_cmp)[None, :] * CMP_STRIDE
    sj = np.arange(n_sel)[:, None] * SEL_BLOCK
    ovt = np.clip(np.minimum(ci + CMP_LEN, sj + SEL_BLOCK) - np.maximum(ci, sj), 0, None) / CMP_LEN
    ovt = jnp.asarray(ovt, dtype=bf16)
    t = batch * seq
    hw = NSA_R * NSA_D
    return pl.pallas_call(
        functools.partial(_nsa_cmp_kernel, tq=tq, n_cmp=n_cmp, n_sel=n_sel),
        grid=(batch, NSA_G, nq),
        in_specs=[pl.BlockSpec((tq, hw), lambda b, g, i: (b * nq + i, OFF_NQ // hw + g)),
                  pl.BlockSpec((1, NSA_D), lambda b, g, i: (0, 0)),
                  pl.BlockSpec((1, n_cmp, NSA_D), lambda b, g, i: (b * NSA_G + g, 0, 0)),
                  pl.BlockSpec((1, n_cmp, NSA_D), lambda b, g, i: (b * NSA_G + g, 0, 0)),
                  pl.BlockSpec((n_sel, n_cmp), lambda b, g, i: (0, 0))],
        out_specs=[pl.BlockSpec((tq, hw), lambda b, g, i: (b * nq + i, g)),
                   pl.BlockSpec((1, n_sel, tq), lambda b, g, i: (b * NSA_G + g, 0, i))],
        out_shape=[jax.ShapeDtypeStruct((t, NSA_G * hw), f32),
                   jax.ShapeDtypeStruct((batch * NSA_G, n_sel, seq), bf16)],
        scratch_shapes=[pltpu.VMEM((n_cmp, NSA_R * tq), f32), pltpu.VMEM((n_cmp, NSA_R * tq), bf16),
                        pltpu.VMEM((n_cmp, tq), f32)],
        compiler_params=_cp(("parallel", "parallel", "parallel")),
        name="nsa_cmp_select",
    )(p, q_norm.astype(f32).reshape(1, NSA_D), kcc, vcc, ovt)


def _nsa_attn_kernel(flag_ref, q_ref, qn_ref, misc_ref, oc_ref, selt_ref, ks_ref, vst_ref, kw_ref, vwt_ref, gn_ref,
                     o_ref, m_scr, l_scr, acc_scr, *, tq, tk, nkt):
    b = pl.program_id(0)
    g = pl.program_id(1)
    qi = pl.program_id(2)
    nq = pl.num_programs(2)
    q0 = qi * tq
    cols = NSA_R * tq
    qs = _stack_q(q_ref, qn_ref)
    nt_dims = (((1,), (1,)), ((), ()))

    bpt = tk // SEL_BLOCK
    krow = lax.broadcasted_iota(jnp.int32, (tk, tq), 0)
    tcol = lax.broadcasted_iota(jnp.int32, (tk, tq), 1)
    kt_last = (q0 + tq - 1) // tk
    slopes = jnp.concatenate([jnp.full((1, tq), 1.0, f32) * _head_slope(g, r) for r in range(NSA_R)], axis=1)

    m_scr[...] = jnp.full((1, cols), NEG, f32)
    l_scr[...] = jnp.zeros((1, cols), f32)
    acc_scr[...] = jnp.zeros((NSA_D, cols), f32)

    def sel_step(kt, causal):
        k0 = pl.multiple_of(kt * tk, tk)
        kb = ks_ref[0, pl.ds(k0, tk), :]
        vtb = vst_ref[0, :, pl.ds(k0, tk)]
        st = lax.dot_general(kb, qs, nt_dims, preferred_element_type=f32)
        picked = selt_ref[0, pl.ds(pl.multiple_of(kt * bpt, bpt), bpt), :]
        off = (picked.astype(f32) - 1.0) * (-NEG)
        bias = jnp.concatenate([jnp.broadcast_to(off[u:u + 1, :], (SEL_BLOCK, tq)) for u in range(bpt)], axis=0)
        rel = krow + (k0 - q0)
        if causal:
            bias = bias + jnp.where(rel <= tcol, 0.0, NEG)
        relf = rel.astype(f32)
        sr = st + jnp.concatenate([relf] * NSA_R, axis=1) * slopes + jnp.concatenate([bias] * NSA_R, axis=1)
        m_old = m_scr[...]
        m_new = jnp.maximum(m_old, jnp.max(sr, axis=0, keepdims=True))
        alpha = jnp.exp(m_old - m_new)
        pr = jnp.exp(sr - m_new)
        l_scr[...] = alpha * l_scr[...] + jnp.sum(pr, axis=0, keepdims=True)
        acc_scr[...] = alpha * acc_scr[...] + jnp.dot(vtb, pr.astype(bf16), preferred_element_type=f32)
        m_scr[...] = m_new

    fbase = ((b * NSA_G + g) * nq + qi) * nkt

    def sel_loop(kt, carry):
        @pl.when(flag_ref[fbase + kt] > 0)
        def _():
            sel_step(kt, False)
        return carry

    lax.fori_loop(0, kt_last, sel_loop, 0)
    sel_step(kt_last, True)
    o_st = acc_scr[...] * (1.0 / jnp.maximum(l_scr[...], 1e-30))

    wk = WINDOW + tq
    w0 = pl.multiple_of(q0, tq)
    kwin = kw_ref[0, pl.ds(w0, wk), :]
    vwt = vwt_ref[0, :, pl.ds(w0, wk)]
    swt = lax.dot_general(kwin, qs, nt_dims, preferred_element_type=f32)
    jr = lax.broadcasted_iota(jnp.int32, (wk, tq), 0)
    tc = lax.broadcasted_iota(jnp.int32, (wk, tq), 1)
    wbias = jnp.where(jr > tc, jnp.where(jr <= tc + WINDOW, 0.0, NEG), NEG)
    wbias = wbias + jnp.where(jr >= WINDOW - q0, 0.0, NEG)
    jf = jr.astype(f32)
    srw = swt + jnp.concatenate([jf] * NSA_R, axis=1) * slopes + jnp.concatenate([wbias] * NSA_R, axis=1)
    mw = jnp.max(srw, axis=0, keepdims=True)
    prw = jnp.exp(srw - mw)
    denw = jnp.sum(prw, axis=0, keepdims=True)
    o_wt = jnp.dot(vwt, prw.astype(bf16), preferred_element_type=f32) * (1.0 / denw)

    for r in range(NSA_R):
        lanes = []
        for c in range(3):
            l0 = GLA_LR + r * 3 + c
            l1 = l0 + NSA_R * 3
            lanes.append(jnp.where(g == 0, misc_ref[:, l0:l0 + 1], misc_ref[:, l1:l1 + 1]))
        gc, gs, gw = [1.0 / (1.0 + jnp.exp(-x)) for x in lanes]
        hs = slice(r * NSA_D, (r + 1) * NSA_D)
        cs = slice(r * tq, (r + 1) * tq)
        y = gc * oc_ref[:, hs] + gs * o_st[:, cs].T + gw * o_wt[:, cs].T
        o_ref[:, hs] = _group_rms(y, gn_ref[:, hs]).astype(bf16)


def _nsa_attn(p, q_norm, o_c, selt, flags, ks, vst, kw, vwt, gain, batch, seq):
    tq, tk = TQ, TK
    nq = seq // tq
    nkt = seq // tk
    n_sel = seq // SEL_BLOCK
    t = batch * seq
    hw = NSA_R * NSA_D
    slab = lambda rows: pl.BlockSpec((1, rows, NSA_D), lambda b, g, i, f: (b * NSA_G + g, 0, 0),
                                     pipeline_mode=pl.Buffered(1))
    slab_t = lambda rows: pl.BlockSpec((1, NSA_D, rows), lambda b, g, i, f: (b * NSA_G + g, 0, 0),
                                       pipeline_mode=pl.Buffered(1))
    grid_spec = pltpu.PrefetchScalarGridSpec(
        num_scalar_prefetch=1,
        grid=(batch, NSA_G, nq),
        in_specs=[pl.BlockSpec((tq, hw), lambda b, g, i, f: (b * nq + i, OFF_NQ // hw + g)),
                  pl.BlockSpec((1, NSA_D), lambda b, g, i, f: (0, 0)),
                  pl.BlockSpec((tq, 128), lambda b, g, i, f: (b * nq + i, OFF_MISC // 128)),
                  pl.BlockSpec((tq, hw), lambda b, g, i, f: (b * nq + i, g)),
                  pl.BlockSpec((1, n_sel, tq), lambda b, g, i, f: (b * NSA_G + g, 0, i)),
                  slab(seq), slab_t(seq), slab(seq + WINDOW), slab_t(seq + WINDOW),
                  pl.BlockSpec((1, hw), lambda b, g, i, f: (0, g))],
        out_specs=pl.BlockSpec((tq, hw), lambda b, g, i, f: (b * nq + i, g)),
        scratch_shapes=[pltpu.VMEM((1, NSA_R * tq), f32), pltpu.VMEM((1, NSA_R * tq), f32),
                        pltpu.VMEM((NSA_D, NSA_R * tq), f32)],
    )
    return pl.pallas_call(
        functools.partial(_nsa_attn_kernel, tq=tq, tk=tk, nkt=nkt),
        grid_spec=grid_spec,
        out_shape=jax.ShapeDtypeStruct((t, NSA_G * hw), bf16),
        compiler_params=_cp(("parallel", "parallel", "arbitrary")),
        name="nsa_attention",
    )(flags, p, q_norm.astype(f32).reshape(1, NSA_D), p, o_c, selt, ks, vst, kw, vwt, gain.reshape(1, -1))


def _out_proj_kernel(a_ref, b_ref, c_ref, wa_ref, wb_ref, wc_ref, h_ref, o_ref):
    o_ref[...] = (h_ref[...]
                  + jnp.dot(a_ref[...], wa_ref[...], preferred_element_type=f32)
                  + jnp.dot(b_ref[...], wb_ref[...], preferred_element_type=f32)
                  + jnp.dot(c_ref[...], wc_ref[...], preferred_element_type=f32))


def _out_proj(ya, yb, yc, w_out, h):
    t, d = h.shape
    tm, tn = TM_PROJ, 1024
    w = w_out.astype(bf16)
    ka, kb, kc = ya.shape[1], yb.shape[1], yc.shape[1]
    return pl.pallas_call(
        _out_proj_kernel,
        grid=(t // tm, d // tn),
        in_specs=[pl.BlockSpec((tm, ka), lambda i, j: (i, 0)),
                  pl.BlockSpec((tm, kb), lambda i, j: (i, 0)),
                  pl.BlockSpec((tm, kc), lambda i, j: (i, 0)),
                  pl.BlockSpec((ka, tn), lambda i, j: (0, j)),
                  pl.BlockSpec((kb, tn), lambda i, j: (1, j)),
                  pl.BlockSpec((kc, tn), lambda i, j: (1, j)),
                  pl.BlockSpec((tm, tn), lambda i, j: (i, j))],
        out_specs=pl.BlockSpec((tm, tn), lambda i, j: (i, j)),
        out_shape=jax.ShapeDtypeStruct((t, d), f32),
        compiler_params=_cp(("parallel", "parallel")),
        name="out_proj",
    )(ya, yb, yc, w, w, w, h)


def _top_k_columns(v, k, ids=None, want_rank=True):
    if ids is None:
        ids = lax.broadcasted_iota(jnp.int32, v.shape, 0)
    big = jnp.int32(2 ** 30)
    kidx = lax.broadcasted_iota(jnp.int32, (k, v.shape[1]), 0)
    vals = jnp.zeros((k, v.shape[1]), f32)
    firsts = []
    rank = jnp.full(v.shape, float(k), f32) if want_rank else None
    for kk in range(k):
        m = jnp.max(v, axis=0, keepdims=True)
        first = jnp.min(jnp.where(v == m, ids, big), axis=0, keepdims=True)
        hit = ids == first
        if want_rank:
            rank = jnp.where(hit, float(kk), rank)
        v = jnp.where(hit, -jnp.inf, v)
        vals = jnp.where(kidx == kk, m, vals)
        firsts.append(first)
    return vals, rank, firsts


def _pair_candidates(t1, t2):
    n = t1.shape[1]
    k = PEER_TOPK
    i16 = lax.broadcasted_iota(jnp.int32, (k, n), 0)
    i8 = lax.broadcasted_iota(jnp.int32, (8, n), 0)
    vals = [t1[0:1, :] + t2, t1[1:2, :] + t2[0:8, :]]
    pos = [i16, k + i8]
    for a in range(2, 8):
        vals.append(jnp.where(i8 < k // (a + 1), t1[a:a + 1, :] + t2[0:8, :], -jnp.inf))
        pos.append(a * k + i8)
    vals.append(t1[8:16, :] + t2[0:1, :])
    pos.append((8 + i8) * k)
    return jnp.concatenate(vals, axis=0), jnp.concatenate(pos, axis=0)


def _peer_route_kernel(q_ref, key_ref, a_ref, cnt_ref, rank2_ref, e2_ref):
    tm = q_ref.shape[0]
    for h in range(PEER_H):
        sc, tops, ranks = [], [], []
        for half in range(2):
            c0 = (h * 2 + half) * PEER_NK
            qh = q_ref[:, c0:c0 + PEER_NK].astype(bf16)
            s = lax.dot_general(key_ref[h * 2 + half], qh, (((1,), (1,)), ((), ())),
                                preferred_element_type=f32)
            vals, rank, _ = _top_k_columns(s, PEER_TOPK)
            sc.append(s)
            tops.append(vals)
            ranks.append(rank)
        cand, cand_pos = _pair_candidates(tops[0], tops[1])
        best, _, pos = _top_k_columns(cand, PEER_TOPK, ids=cand_pos, want_rank=False)
        mass = jnp.sum(jnp.exp(best - best[0:1, :]), axis=0, keepdims=True)
        aidx = lax.broadcasted_iota(jnp.int32, (PEER_TOPK, tm), 0)
        per_rank = jnp.zeros((PEER_TOPK, tm), f32)
        for p in pos:
            per_rank = per_rank + jnp.where(aidx == p // PEER_TOPK, 1.0, 0.0)
        cnt = jnp.zeros((PEER_NK, tm), f32)
        for a in range(PEER_TOPK):
            cnt = cnt + jnp.where(ranks[0] == float(a), per_rank[a:a + 1, :], 0.0)
        hs = slice(h * PEER_NK, (h + 1) * PEER_NK)
        a_ref[hs, :] = (jnp.exp(sc[0] - tops[0][0:1, :]) * (0.5 / mass)).astype(bf16)
        cnt_ref[hs, :] = cnt.astype(bf16)
        rank2_ref[hs, :] = ranks[1].astype(bf16)
        e2_ref[hs, :] = jnp.exp(sc[1] - tops[1][0:1, :]).astype(bf16)


def _peer_route(qry, keys):
    t = qry.shape[0]
    tm = TM_ROUTE
    narrow = jax.ShapeDtypeStruct((PEER_H * PEER_NK, t), bf16)
    nspec = pl.BlockSpec((PEER_H * PEER_NK, tm), lambda i: (0, i))
    return pl.pallas_call(
        _peer_route_kernel,
        grid=(t // tm,),
        in_specs=[pl.BlockSpec((tm, PEER_H * 2 * PEER_NK), lambda i: (i, 0)),
                  pl.BlockSpec((PEER_H * 2, PEER_NK, PEER_NK), lambda i: (0, 0, 0))],
        out_specs=[nspec] * 4,
        out_shape=[narrow] * 4,
        compiler_params=_cp(("parallel",)),
        name="peer_route",
    )(qry, keys.astype(bf16).reshape(PEER_H * 2, PEER_NK, PEER_NK))


def _peer_expert_kernel(xn_ref, u_ref, vt_ref, a_ref, cnt_ref, rank2_ref, e2_ref, h_ref, o_ref,
                        acc_ref, act_ref, gt_ref, arow_ref, crow_ref, *, tm, te):
    j = pl.program_id(1)

    @pl.when(j == 0)
    def _():
        acc_ref[...] = jnp.zeros_like(acc_ref)

    n_first = te // PEER_NK
    i0 = pl.multiple_of(j * n_first, n_first)

    a_rows = [a_ref[pl.ds(pl.multiple_of(h * PEER_NK + i0, n_first), n_first), :] for h in range(PEER_H)]
    c_rows = [cnt_ref[pl.ds(pl.multiple_of(h * PEER_NK + i0, n_first), n_first), :] for h in range(PEER_H)]
    for h in range(PEER_H):
        arow_ref[h * n_first:(h + 1) * n_first, :] = a_rows[h]
        crow_ref[h * n_first:(h + 1) * n_first, :] = c_rows[h]

    x = lax.dot_general(u_ref[...], xn_ref[...], (((1,), (1,)), ((), ())), preferred_element_type=f32)
    act_ref[...] = (x * (1.0 + jnp.tanh(x * (0.7978845608028654 + 0.035677408136300125 * (x * x))))).astype(bf16)

    zero = jnp.zeros((PEER_NK, BF16_LANES), bf16)
    for c in range(tm // BF16_LANES):
        ls = slice(c * BF16_LANES, (c + 1) * BF16_LANES)
        for u in range(n_first):
            rs = slice(u * PEER_NK, (u + 1) * PEER_NK)
            w = zero
            for h in range(PEER_H):
                hs = slice(h * PEER_NK, (h + 1) * PEER_NK)
                a = arow_ref[h * n_first + u:h * n_first + u + 1, ls]
                cnt = crow_ref[h * n_first + u:h * n_first + u + 1, ls]
                keep = rank2_ref[hs, ls] < cnt
                w = w + jnp.where(keep, e2_ref[hs, ls], zero) * a
            gt_ref[rs, ls] = w * act_ref[rs, ls]

    acc_ref[...] += jnp.dot(vt_ref[...], gt_ref[...], preferred_element_type=f32)

    @pl.when(j == pl.num_programs(1) - 1)
    def _():
        o_ref[...] = h_ref[...] + acc_ref[...].T


def _peer_experts(xn, u, vt, a, cnt, rank2, e2, h):
    t, d = h.shape
    tm, te = TM_PEER, TE_PEER
    nspec = pl.BlockSpec((PEER_H * PEER_NK, tm), lambda i, j: (0, i))
    return pl.pallas_call(
        functools.partial(_peer_expert_kernel, tm=tm, te=te),
        grid=(t // tm, PEER_E // te),
        in_specs=[pl.BlockSpec((tm, d), lambda i, j: (i, 0)),
                  pl.BlockSpec((te, d), lambda i, j: (j, 0)),
                  pl.BlockSpec((d, te), lambda i, j: (0, j)),
                  nspec, nspec, nspec, nspec,
                  pl.BlockSpec((tm, d), lambda i, j: (i, 0))],
        out_specs=pl.BlockSpec((tm, d), lambda i, j: (i, 0)),
        out_shape=jax.ShapeDtypeStruct((t, d), f32),
        scratch_shapes=[pltpu.VMEM((d, tm), f32), pltpu.VMEM((te, tm), bf16), pltpu.VMEM((te, tm), bf16),
                        pltpu.VMEM((PEER_H * te // PEER_NK, tm), bf16),
                        pltpu.VMEM((PEER_H * te // PEER_NK, tm), bf16)],
        compiler_params=_cp(("parallel", "arbitrary"), PEER_EXPERT_FLAGS),
        name="peer_experts",
    )(xn, u, vt, a, cnt, rank2, e2, h)


def _permute_w_in(w):
    pad = jnp.zeros((w.shape[0], N_IN - ORIG_END), w.dtype)
    return jnp.concatenate([w[:, :ORIG_GA], w[:, ORIG_NQ:ORIG_GATE], w[:, ORIG_GA:ORIG_NQ],
                            w[:, ORIG_GATE:ORIG_END], pad], axis=1).astype(bf16)


def _tile_flags(selt, seq):
    bg = selt.shape[0]
    any_sel = selt.reshape(bg, seq // TK, TK // SEL_BLOCK, seq // TQ, TQ).max(axis=(2, 4))
    return (jnp.swapaxes(any_sel, 1, 2) > 0.5).astype(jnp.int32).reshape(-1)


def _mixers(h, norm_mix, w_in, conv_w, gla_w_a2, gla_b_a, q_norm, k_norm, cmp_pos, cmp_w1, cmp_w2, out_norm,
            batch, seq):
    p = _norm_matmul(h, norm_mix.astype(f32), _permute_w_in(w_in), TN_IN)
    gain = out_norm.astype(f32)
    y_conv = _conv_mixer(p, conv_w, gain[:CONV_W], seq)
    y_gla = _gla_mixer(p, gla_w_a2, gla_b_a, gain[CONV_W:2 * CONV_W], batch, seq)
    kcc = _compress(p, OFF_NKC, cmp_pos[0], cmp_w1[0], cmp_w2[0], k_norm[0], batch, seq, True)
    vcc = _compress(p, OFF_NVC, cmp_pos[1], cmp_w1[1], cmp_w2[1], k_norm[0], batch, seq, False)
    ks, vs, kw, vw = _kv_prep(p, k_norm, batch, seq)
    o_c, selt = _nsa_cmp(p, q_norm, kcc, vcc, batch, seq)
    y_nsa = _nsa_attn(p, q_norm, o_c, selt, _tile_flags(selt, seq), ks, vs, kw, vw, gain[2 * CONV_W:], batch, seq)
    return y_conv, y_gla, y_nsa


def _layer(h, norm_mix, w_in, conv_w, gla_w_a2, gla_b_a, q_norm, k_norm, cmp_pos, cmp_w1, cmp_w2, out_norm, w_out,
           norm_ffn, peer_w_q, peer_keys, peer_u, peer_v, batch, seq):
    y_conv, y_gla, y_nsa = _mixers(h, norm_mix, w_in, conv_w, gla_w_a2, gla_b_a, q_norm, k_norm, cmp_pos, cmp_w1,
                                   cmp_w2, out_norm, batch, seq)
    h = _out_proj(y_conv, y_gla, y_nsa, w_out, h)
    qry, xn = _norm_matmul(h, norm_ffn.astype(f32), peer_w_q.astype(bf16), 1024, emit_xn=True)
    a, cnt, rank2, e2 = _peer_route(qry, peer_keys)
    return _peer_experts(xn, peer_u.astype(bf16), peer_v.astype(bf16).T, a, cnt, rank2, e2, h)


def kernel(x, norm_mix, w_in, conv_w, gla_w_a2, gla_b_a, nsa_q_norm, nsa_k_norm, nsa_cmp_pos, nsa_cmp_w1,
           nsa_cmp_w2, out_norm, w_out, norm_ffn, peer_w_q, peer_keys, peer_u, peer_v):
    batch, seq, d = x.shape
    h = x.reshape(batch * seq, d)
    for l in range(w_in.shape[0]):
        h = _layer(h, norm_mix[l], w_in[l], conv_w[l], gla_w_a2[l], gla_b_a[l], nsa_q_norm[l], nsa_k_norm[l],
                   nsa_cmp_pos[l], nsa_cmp_w1[l], nsa_cmp_w2[l], out_norm[l], w_out[l], norm_ffn[l], peer_w_q[l],
                   peer_keys[l], peer_u[l], peer_v[l], batch, seq)
    return h.reshape(batch, seq, d)
```

```python
import functools
import math

import numpy as np
import jax
import jax.numpy as jnp
from jax import lax
from jax.experimental import pallas as pl
from jax.experimental.pallas import tpu as pltpu

f32 = jnp.float32
bf16 = jnp.bfloat16

D_MODEL = 2048
EPS = 1e-6
HEAD = 128
CONV_W = 512
CONV_K = 3
GLA_H, GLA_DK, GLA_DV = 4, 64, 128
GLA_LR = 16
GLA_TAU = 16.0
GLA_CHUNK = 64
NSA_G, NSA_R, NSA_D = 2, 4, 128
CMP_LEN, CMP_STRIDE = 32, 16
SEL_BLOCK, SEL_TOPK, SEL_LOCAL = 64, 16, 2
WINDOW = 512
BIG = 1e9
PEER_H, PEER_NK, PEER_TOPK = 8, 128, 16
PEER_E = PEER_NK * PEER_NK
NEG = -1e30
BF16_LANES = 256

OFF_CX, OFF_CB, OFF_CC = 0, 512, 1024
OFF_GQ, OFF_GK, OFF_GV, OFF_GG = 1536, 1792, 2048, 2560
OFF_NQ = 3072
OFF_NKC, OFF_NVC, OFF_NKS, OFF_NVS, OFF_NKW, OFF_NVW = 4096, 4352, 4608, 4864, 5120, 5376
OFF_MISC = 5632
N_IN = 5760
ORIG_GA, ORIG_NQ, ORIG_GATE, ORIG_END = 3072, 3088, 5648, 5672

TM_PROJ = 1024
TN_IN = 1152
TS_ELT = 512
TC_GLA = 512
TQ = 128
TK = 512
CMP_ROWS = 256
TM_PEER = 512
TE_PEER = 1024
TM_ROUTE = 256
PEER_JB = 128
VMEM_LIMIT = 56 * 1024 * 1024
GELU_C = math.sqrt(2.0 / math.pi)
GELU_A = 0.044715


def _cp(sem):
    return pltpu.CompilerParams(dimension_semantics=sem, vmem_limit_bytes=VMEM_LIMIT)


def _gelu(x):
    return 0.5 * x * (1.0 + jnp.tanh(GELU_C * (x + GELU_A * (x * x * x))))


def _split_bf16(x):
    hi = x.astype(bf16)
    lo = (x - hi.astype(f32)).astype(bf16)
    return hi, lo


def _group_rms(y, gain):
    ms = jnp.mean(y * y, axis=-1, keepdims=True)
    return y * lax.rsqrt(ms + EPS) * gain


def _norm_matmul_kernel(x_ref, g_ref, w_ref, *refs, emit_xn):
    if emit_xn:
        o_ref, xo_ref, xn_ref = refs
    else:
        o_ref, xn_ref = refs

    @pl.when(pl.program_id(1) == 0)
    def _():
        x = x_ref[...]
        ms = jnp.mean(x * x, axis=-1, keepdims=True)
        xn = (x * lax.rsqrt(ms + EPS) * g_ref[...]).astype(bf16)
        xn_ref[...] = xn
        if emit_xn:
            xo_ref[...] = xn

    o_ref[...] = jnp.dot(xn_ref[...], w_ref[...], preferred_element_type=f32)


def _norm_matmul(x, gain, w, tn, emit_xn=False):
    t, d = x.shape
    n = w.shape[1]
    tm = TM_PROJ
    out_shape = [jax.ShapeDtypeStruct((t, n), f32)]
    out_specs = [pl.BlockSpec((tm, tn), lambda i, j: (i, j))]
    if emit_xn:
        out_shape.append(jax.ShapeDtypeStruct((t, d), bf16))
        out_specs.append(pl.BlockSpec((tm, d), lambda i, j: (i, 0)))
    outs = pl.pallas_call(
        functools.partial(_norm_matmul_kernel, emit_xn=emit_xn),
        grid=(t // tm, n // tn),
        in_specs=[
            pl.BlockSpec((tm, d), lambda i, j: (i, 0)),
            pl.BlockSpec((1, d), lambda i, j: (0, 0)),
            pl.BlockSpec((d, tn), lambda i, j: (0, j)),
        ],
        out_specs=out_specs,
        out_shape=out_shape,
        scratch_shapes=[pltpu.VMEM((tm, d), bf16)],
        compiler_params=_cp(("parallel", "arbitrary")),
        name="norm_matmul",
    )(x, gain.reshape(1, d), w)
    return outs if emit_xn else outs[0]


def _conv_kernel(cx_ref, cb_ref, cc_ref, pcx_ref, pcc_ref, w_ref, g_ref, o_ref, z_ref, *, ts, tiles_per_seq):
    first = (pl.program_id(0) % tiles_per_seq) == 0
    zp = pcx_ref[...] * pcc_ref[...]
    z_ref[0:8, :] = jnp.where(first, 0.0, zp)
    z_ref[8:8 + ts, :] = cx_ref[...] * cc_ref[...]
    y = (z_ref[8:8 + ts, :] * w_ref[0:1, :]
         + z_ref[7:7 + ts, :] * w_ref[1:2, :]
         + z_ref[6:6 + ts, :] * w_ref[2:3, :])
    y = cb_ref[...] * y
    for c in range(CONV_W // HEAD):
        sl = slice(c * HEAD, (c + 1) * HEAD)
        o_ref[:, sl] = _group_rms(y[:, sl], g_ref[:, sl]).astype(bf16)


def _conv_mixer(p, conv_w, gain, seq):
    t = p.shape[0]
    ts = TS_ELT
    w_t = conv_w.astype(f32).T
    cur = lambda off: pl.BlockSpec((ts, CONV_W), lambda i, off=off: (i, off // CONV_W))
    prev = lambda off: pl.BlockSpec((8, CONV_W), lambda i, off=off: (jnp.maximum(i * (ts // 8) - 1, 0), off // CONV_W))
    return pl.pallas_call(
        functools.partial(_conv_kernel, ts=ts, tiles_per_seq=seq // ts),
        grid=(t // ts,),
        in_specs=[cur(OFF_CX), cur(OFF_CB), cur(OFF_CC), prev(OFF_CX), prev(OFF_CC),
                  pl.BlockSpec((CONV_K, CONV_W), lambda i: (0, 0)),
                  pl.BlockSpec((1, CONV_W), lambda i: (0, 0))],
        out_specs=pl.BlockSpec((ts, CONV_W), lambda i: (i, 0)),
        out_shape=jax.ShapeDtypeStruct((t, CONV_W), bf16),
        scratch_shapes=[pltpu.VMEM((ts + 8, CONV_W), f32)],
        compiler_params=_cp(("parallel",)),
        name="conv_mixer",
    )(p, p, p, p, p, w_t, gain.reshape(1, CONV_W))


def _gla_kernel(q_ref, k_ref, v_ref, gg_ref, misc_ref, wa_ref, ba_ref, gn_ref, o_ref,
                st_ref, la_ref, y_ref, *, tc):
    @pl.when(pl.program_id(1) == 0)
    def _():
        st_ref[...] = jnp.zeros_like(st_ref)

    a_hi, a_lo = _split_bf16(misc_ref[:, 0:GLA_LR])
    w_hi, w_lo = _split_bf16(wa_ref[...])
    z = (jnp.dot(a_hi, w_hi, preferred_element_type=f32)
         + jnp.dot(a_hi, w_lo, preferred_element_type=f32)
         + jnp.dot(a_lo, w_hi, preferred_element_type=f32)) + ba_ref[...]
    la_ref[...] = (jnp.minimum(z, 0.0) - jnp.log(1.0 + jnp.exp(-jnp.abs(z)))) * (1.0 / GLA_TAU)

    c = GLA_CHUNK
    row = lax.broadcasted_iota(jnp.int32, (c, c), 0)
    col = lax.broadcasted_iota(jnp.int32, (c, c), 1)
    causal = row >= col
    tril = causal.astype(bf16)
    scale = GLA_DK ** -0.5

    def chunk(ci, carry):
        r0 = pl.multiple_of(ci * c, c)
        la = la_ref[pl.ds(r0, c), :]
        la_hi, la_lo = _split_bf16(la)
        la_lo2 = (la - la_hi.astype(f32) - la_lo.astype(f32)).astype(bf16)
        bc = (jnp.dot(tril, la_hi, preferred_element_type=f32)
              + jnp.dot(tril, la_lo, preferred_element_type=f32)
              + jnp.dot(tril, la_lo2, preferred_element_type=f32))
        bl = bc[c - 1:c, :]
        qc = q_ref[pl.ds(r0, c), :]
        kc = k_ref[pl.ds(r0, c), :]
        vc = v_ref[pl.ds(r0, c), :].astype(bf16)
        q_dec = (qc * scale * jnp.exp(bc)).astype(bf16)
        k_inv = (kc * jnp.exp(-bc)).astype(bf16)
        k_end = (kc * jnp.exp(bl - bc)).astype(bf16)
        decay = jnp.exp(bl)
        for h in range(GLA_H):
            ks = slice(h * GLA_DK, (h + 1) * GLA_DK)
            vs = slice(h * GLA_DV, (h + 1) * GLA_DV)
            att = lax.dot_general(q_dec[:, ks], k_inv[:, ks], (((1,), (1,)), ((), ())),
                                  preferred_element_type=f32)
            att = jnp.where(causal, att, 0.0).astype(bf16)
            o = jnp.dot(att, vc[:, vs], preferred_element_type=f32)
            st = st_ref[h]
            o = o + lax.dot_general(q_dec[:, ks], st.astype(bf16), (((1,), (1,)), ((), ())),
                                    preferred_element_type=f32)
            y_ref[pl.ds(r0, c), vs] = o
            upd = lax.dot_general(vc[:, vs], k_end[:, ks], (((0,), (0,)), ((), ())),
                                  preferred_element_type=f32)
            st_ref[h] = st * decay[:, ks] + upd
        return carry

    lax.fori_loop(0, tc // c, chunk, 0)

    g = gg_ref[...]
    sw = g * (1.0 / (1.0 + jnp.exp(-g)))
    for h in range(GLA_H):
        vs = slice(h * GLA_DV, (h + 1) * GLA_DV)
        o_ref[:, vs] = (_group_rms(y_ref[:, vs], gn_ref[:, vs]) * sw[:, vs]).astype(bf16)


def _gla_mixer(p, w_a2, b_a, gain, batch, seq):
    t = p.shape[0]
    tc = TC_GLA
    nst = seq // tc
    blk = lambda width, off: pl.BlockSpec((tc, width), lambda b, s, off=off, width=width: (b * nst + s, off // width))
    const = lambda shape: pl.BlockSpec(shape, lambda b, s: (0,) * len(shape))
    return pl.pallas_call(
        functools.partial(_gla_kernel, tc=tc),
        grid=(batch, nst),
        in_specs=[blk(256, OFF_GQ), blk(256, OFF_GK), blk(512, OFF_GV), blk(512, OFF_GG), blk(128, OFF_MISC),
                  const((GLA_LR, GLA_H * GLA_DK)), const((1, GLA_H * GLA_DK)), const((1, GLA_H * GLA_DV))],
        out_specs=pl.BlockSpec((tc, GLA_H * GLA_DV), lambda b, s: (b * nst + s, 0)),
        out_shape=jax.ShapeDtypeStruct((t, GLA_H * GLA_DV), bf16),
        scratch_shapes=[pltpu.VMEM((GLA_H, GLA_DV, GLA_DK), f32),
                        pltpu.VMEM((tc, GLA_H * GLA_DK), f32),
                        pltpu.VMEM((tc, GLA_H * GLA_DV), f32)],
        compiler_params=_cp(("parallel", "arbitrary")),
        name="gla_mixer",
    )(p, p, p, p, p, w_a2.astype(f32), b_a.astype(f32).reshape(1, -1), gain.reshape(1, -1))


def _compress_kernel(t_ref, pos_ref, w1_ref, w2_ref, gn_ref, o_ref, b_scr, *, n_piece, apply_norm):
    half = CMP_STRIDE
    acc_a = jnp.zeros((n_piece, NSA_D), f32)
    acc_b = jnp.zeros((n_piece, NSA_D), f32)
    for r in range(half):
        rows = t_ref[pl.ds(r, n_piece, stride=CMP_STRIDE), :]
        xa = (rows + pos_ref[r:r + 1, :]).astype(bf16)
        acc_a = acc_a + jnp.dot(xa, w1_ref[r], preferred_element_type=f32)
        xb = (rows + pos_ref[half + r:half + r + 1, :]).astype(bf16)
        acc_b = acc_b + jnp.dot(xb, w1_ref[half + r], preferred_element_type=f32)
    b_scr[0:n_piece, :] = acc_b
    b_scr[n_piece:n_piece + 8, :] = jnp.zeros((8, NSA_D), f32)
    hid = acc_a + b_scr[1:n_piece + 1, :]
    out = jnp.dot(_gelu(hid).astype(bf16), w2_ref[...], preferred_element_type=f32)
    if apply_norm:
        out = _group_rms(out, gn_ref[...])
    o_ref[0] = out.astype(bf16)


def _compress(p, off, pos, w1, w2, gain, batch, seq, apply_norm):
    n_piece = seq // CMP_STRIDE
    return pl.pallas_call(
        functools.partial(_compress_kernel, n_piece=n_piece, apply_norm=apply_norm),
        grid=(batch, NSA_G),
        in_specs=[pl.BlockSpec((seq, NSA_D), lambda b, g: (b, off // NSA_D + g)),
                  pl.BlockSpec((CMP_LEN, NSA_D), lambda b, g: (0, 0)),
                  pl.BlockSpec((CMP_LEN, NSA_D, NSA_D), lambda b, g: (0, 0, 0)),
                  pl.BlockSpec((NSA_D, NSA_D), lambda b, g: (0, 0)),
                  pl.BlockSpec((1, NSA_D), lambda b, g: (0, 0))],
        out_specs=pl.BlockSpec((1, n_piece, NSA_D), lambda b, g: (b * NSA_G + g, 0, 0)),
        out_shape=jax.ShapeDtypeStruct((batch * NSA_G, n_piece, NSA_D), bf16),
        scratch_shapes=[pltpu.VMEM((n_piece + 8, NSA_D), f32)],
        compiler_params=_cp(("parallel", "parallel")),
        name="nsa_compress",
    )(p, pos.astype(f32), w1.astype(bf16).reshape(CMP_LEN, NSA_D, NSA_D), w2.astype(bf16),
      gain.astype(f32).reshape(1, NSA_D))


def _kvprep_kernel(ks_ref, vs_ref, kw_ref, vw_ref, gs_ref, gw_ref, oks_ref, ovs_ref, okw_ref, ovw_ref):
    lead = pl.program_id(2) == 0
    oks_ref[0] = _group_rms(ks_ref[...], gs_ref[...]).astype(bf16)
    ovs_ref[0] = vs_ref[...].T.astype(bf16)
    okw_ref[0] = jnp.where(lead, 0.0, _group_rms(kw_ref[...], gw_ref[...])).astype(bf16)
    ovw_ref[0] = jnp.where(lead, 0.0, vw_ref[...]).T.astype(bf16)


def _kv_prep(p, k_norm, batch, seq):
    ts = WINDOW
    nst = seq // ts
    src = lambda s: jnp.maximum(s - 1, 0)
    blk = lambda off: pl.BlockSpec((ts, NSA_D), lambda b, g, s, off=off: (b * nst + src(s), off // NSA_D + g))
    gspec = pl.BlockSpec((1, NSA_D), lambda b, g, s: (0, 0))
    kspec = pl.BlockSpec((1, ts, NSA_D), lambda b, g, s: (b * NSA_G + g, src(s), 0))
    vspec = pl.BlockSpec((1, NSA_D, ts), lambda b, g, s: (b * NSA_G + g, 0, src(s)))
    kwspec = pl.BlockSpec((1, ts, NSA_D), lambda b, g, s: (b * NSA_G + g, s, 0))
    vwspec = pl.BlockSpec((1, NSA_D, ts), lambda b, g, s: (b * NSA_G + g, 0, s))
    bg = batch * NSA_G
    return pl.pallas_call(
        _kvprep_kernel,
        grid=(batch, NSA_G, nst + 1),
        in_specs=[blk(OFF_NKS), blk(OFF_NVS), blk(OFF_NKW), blk(OFF_NVW), gspec, gspec],
        out_specs=[kspec, vspec, kwspec, vwspec],
        out_shape=[jax.ShapeDtypeStruct((bg, seq, NSA_D), bf16), jax.ShapeDtypeStruct((bg, NSA_D, seq), bf16),
                   jax.ShapeDtypeStruct((bg, seq + WINDOW, NSA_D), bf16),
                   jax.ShapeDtypeStruct((bg, NSA_D, seq + WINDOW), bf16)],
        compiler_params=_cp(("parallel", "parallel", "arbitrary")),
        name="nsa_kv_prep",
    )(p, p, p, p, k_norm[1].astype(f32).reshape(1, NSA_D), k_norm[2].astype(f32).reshape(1, NSA_D))


def _stack_q(q_ref, qn_ref):
    parts = []
    for r in range(NSA_R):
        qn = _group_rms(q_ref[:, r * NSA_D:(r + 1) * NSA_D], qn_ref[...])
        parts.append((qn * (NSA_D ** -0.5)).astype(bf16))
    return jnp.concatenate(parts, axis=0)


def _head_slope(g, r):
    return jnp.where(g == 0, 1.0, 2.0 ** -NSA_R).astype(f32) * (2.0 ** -(r + 1))


def _nsa_cmp_kernel(q_ref, qn_ref, kcc_ref, vcc_ref, ovt_ref, oc_ref, selt_ref, s_scr, p_scr, psum_scr,
                    *, tq, n_cmp, n_sel):
    g = pl.program_id(1)
    q0 = pl.program_id(2) * tq
    qs = _stack_q(q_ref, qn_ref)
    s_scr[...] = lax.dot_general(kcc_ref[0], qs, (((1,), (1,)), ((), ())), preferred_element_type=f32)
    rc = CMP_ROWS
    n_chunk = n_cmp // rc
    n_vis = jnp.clip((q0 + tq - CMP_LEN) // CMP_STRIDE + 1, 0, n_cmp)
    n_act = (n_vis + rc - 1) // rc
    tcol = lax.broadcasted_iota(jnp.int32, (rc, tq), 1)
    nrow = lax.broadcasted_iota(jnp.int32, (rc, tq), 0)

    def rel_end(c):
        return (c * rc + nrow) * CMP_STRIDE + (CMP_LEN - 1 - q0)

    for r in range(NSA_R):
        cs = slice(r * tq, (r + 1) * tq)
        slope = _head_slope(g, r)

        def biased(c, m):
            r0 = pl.multiple_of(c * rc, rc)
            rel = rel_end(c)
            sb = jnp.where(rel <= tcol, s_scr[pl.ds(r0, rc), cs] + slope * rel.astype(f32), NEG)
            s_scr[pl.ds(r0, rc), cs] = sb
            return jnp.maximum(m, jnp.max(sb, axis=0, keepdims=True))

        m = lax.fori_loop(0, n_act, biased, jnp.full((1, tq), NEG, f32))
        m = jnp.maximum(m, 0.1 * NEG)

        def expo(c, l):
            r0 = pl.multiple_of(c * rc, rc)
            e = jnp.exp(s_scr[pl.ds(r0, rc), cs] - m)
            s_scr[pl.ds(r0, rc), cs] = e
            return l + jnp.sum(e, axis=0, keepdims=True)

        l = lax.fori_loop(0, n_act, expo, jnp.zeros((1, tq), f32))
        inv = 1.0 / jnp.maximum(l, 1e-30)

        def normalise(c, carry):
            r0 = pl.multiple_of(c * rc, rc)
            pn = s_scr[pl.ds(r0, rc), cs] * inv
            p_scr[pl.ds(r0, rc), cs] = pn.astype(bf16)
            if r == 0:
                psum_scr[pl.ds(r0, rc), :] = pn
            else:
                psum_scr[pl.ds(r0, rc), :] += pn
            return carry

        lax.fori_loop(0, n_act, normalise, 0)

    def clear(c, carry):
        r0 = pl.multiple_of(c * rc, rc)
        p_scr[pl.ds(r0, rc), :] = jnp.zeros((rc, NSA_R * tq), bf16)
        psum_scr[pl.ds(r0, rc), :] = jnp.zeros((rc, tq), f32)
        return carry

    lax.fori_loop(n_act, n_chunk, clear, 0)

    vcc = vcc_ref[0]
    for r in range(NSA_R):
        oc_ref[:, r * NSA_D:(r + 1) * NSA_D] = lax.dot_general(
            p_scr[:, r * tq:(r + 1) * tq], vcc, (((0,), (0,)), ((), ())), preferred_element_type=f32)

    p_hi, p_lo = _split_bf16(psum_scr[...])
    imp = (jnp.dot(ovt_ref[...], p_hi, preferred_element_type=f32)
           + jnp.dot(ovt_ref[...], p_lo, preferred_element_type=f32))
    blk = lax.broadcasted_iota(jnp.int32, (n_sel, tq), 0)
    tblk = (q0 + lax.broadcasted_iota(jnp.int32, (n_sel, tq), 1)) // SEL_BLOCK
    lag = tblk - blk
    score = jnp.where(blk == 0, BIG, jnp.where(lag < SEL_LOCAL, BIG, imp))
    score = jnp.where(lag >= 0, score, -1.0)
    sel = jnp.zeros((n_sel, tq), f32)
    for _ in range(SEL_TOPK):
        m = jnp.max(score, axis=0, keepdims=True)
        first = jnp.min(jnp.where(score == m, blk, n_sel), axis=0, keepdims=True)
        hit = blk == first
        sel = jnp.where(hit, 1.0, sel)
        score = jnp.where(hit, -2.0, score)
    selt_ref[0] = sel.astype(bf16)


def _nsa_cmp(p, q_norm, kcc, vcc, batch, seq):
    tq = TQ
    nq = seq // tq
    n_cmp = seq // CMP_STRIDE
    n_sel = seq // SEL_BLOCK
    ci = np.arange(n_cmp)[None, :] * CMP_STRIDE
    sj = np.arange(n_sel)[:, None] * SEL_BLOCK
    ovt = np.clip(np.minimum(ci + CMP_LEN, sj + SEL_BLOCK) - np.maximum(ci, sj), 0, None) / CMP_LEN
    ovt = jnp.asarray(ovt, dtype=bf16)
    t = batch * seq
    hw = NSA_R * NSA_D
    return pl.pallas_call(
        functools.partial(_nsa_cmp_kernel, tq=tq, n_cmp=n_cmp, n_sel=n_sel),
        grid=(batch, NSA_G, nq),
        in_specs=[pl.BlockSpec((tq, hw), lambda b, g, i: (b * nq + i, OFF_NQ // hw + g)),
                  pl.BlockSpec((1, NSA_D), lambda b, g, i: (0, 0)),
                  pl.BlockSpec((1, n_cmp, NSA_D), lambda b, g, i: (b * NSA_G + g, 0, 0)),
                  pl.BlockSpec((1, n_cmp, NSA_D), lambda b, g, i: (b * NSA_G + g, 0, 0)),
                  pl.BlockSpec((n_sel, n_cmp), lambda b, g, i: (0, 0))],
        out_specs=[pl.BlockSpec((tq, hw), lambda b, g, i: (b * nq + i, g)),
                   pl.BlockSpec((1, n_sel, tq), lambda b, g, i: (b * NSA_G + g, 0, i))],
        out_shape=[jax.ShapeDtypeStruct((t, NSA_G * hw), f32),
                   jax.ShapeDtypeStruct((batch * NSA_G, n_sel, seq), bf16)],
        scratch_shapes=[pltpu.VMEM((n_cmp, NSA_R * tq), f32), pltpu.VMEM((n_cmp, NSA_R * tq), bf16),
                        pltpu.VMEM((n_cmp, tq), f32)],
        compiler_params=_cp(("parallel", "parallel", "parallel")),
        name="nsa_cmp_select",
    )(p, q_norm.astype(f32).reshape(1, NSA_D), kcc, vcc, ovt)


def _nsa_attn_kernel(flag_ref, q_ref, qn_ref, misc_ref, oc_ref, selt_ref, ks_ref, vst_ref, kw_ref, vwt_ref, gn_ref,
                     o_ref, m_scr, l_scr, acc_scr, *, tq, tk, nkt):
    b = pl.program_id(0)
    g = pl.program_id(1)
    qi = pl.program_id(2)
    nq = pl.num_programs(2)
    q0 = qi * tq
    cols = NSA_R * tq
    qs = _stack_q(q_ref, qn_ref)
    nt_dims = (((1,), (1,)), ((), ()))

    bpt = tk // SEL_BLOCK
    krow = lax.broadcasted_iota(jnp.int32, (tk, tq), 0)
    tcol = lax.broadcasted_iota(jnp.int32, (tk, tq), 1)
    kt_last = (q0 + tq - 1) // tk
    slopes = jnp.concatenate([jnp.full((1, tq), 1.0, f32) * _head_slope(g, r) for r in range(NSA_R)], axis=1)

    m_scr[...] = jnp.full((1, cols), NEG, f32)
    l_scr[...] = jnp.zeros((1, cols), f32)
    acc_scr[...] = jnp.zeros((NSA_D, cols), f32)

    def sel_step(kt, causal):
        k0 = pl.multiple_of(kt * tk, tk)
        kb = ks_ref[0, pl.ds(k0, tk), :]
        vtb = vst_ref[0, :, pl.ds(k0, tk)]
        st = lax.dot_general(kb, qs, nt_dims, preferred_element_type=f32)
        picked = selt_ref[0, pl.ds(pl.multiple_of(kt * bpt, bpt), bpt), :]
        off = (picked.astype(f32) - 1.0) * (-NEG)
        bias = jnp.concatenate([jnp.broadcast_to(off[u:u + 1, :], (SEL_BLOCK, tq)) for u in range(bpt)], axis=0)
        rel = krow + (k0 - q0)
        if causal:
            bias = bias + jnp.where(rel <= tcol, 0.0, NEG)
        relf = rel.astype(f32)
        sr = st + jnp.concatenate([relf] * NSA_R, axis=1) * slopes + jnp.concatenate([bias] * NSA_R, axis=1)
        m_old = m_scr[...]
        m_new = jnp.maximum(m_old, jnp.max(sr, axis=0, keepdims=True))
        alpha = jnp.exp(m_old - m_new)
        pr = jnp.exp(sr - m_new)
        l_scr[...] = alpha * l_scr[...] + jnp.sum(pr, axis=0, keepdims=True)
        acc_scr[...] = alpha * acc_scr[...] + jnp.dot(vtb, pr.astype(bf16), preferred_element_type=f32)
        m_scr[...] = m_new

    fbase = ((b * NSA_G + g) * nq + qi) * nkt

    def sel_loop(kt, carry):
        @pl.when(flag_ref[fbase + kt] > 0)
        def _():
            sel_step(kt, False)
        return carry

    lax.fori_loop(0, kt_last, sel_loop, 0)
    sel_step(kt_last, True)
    o_st = acc_scr[...] * (1.0 / jnp.maximum(l_scr[...], 1e-30))

    wk = WINDOW + tq
    w0 = pl.multiple_of(q0, tq)
    kwin = kw_ref[0, pl.ds(w0, wk), :]
    vwt = vwt_ref[0, :, pl.ds(w0, wk)]
    swt = lax.dot_general(kwin, qs, nt_dims, preferred_element_type=f32)
    jr = lax.broadcasted_iota(jnp.int32, (wk, tq), 0)
    tc = lax.broadcasted_iota(jnp.int32, (wk, tq), 1)
    wbias = jnp.where(jr > tc, jnp.where(jr <= tc + WINDOW, 0.0, NEG), NEG)
    wbias = wbias + jnp.where(jr >= WINDOW - q0, 0.0, NEG)
    jf = jr.astype(f32)
    srw = swt + jnp.concatenate([jf] * NSA_R, axis=1) * slopes + jnp.concatenate([wbias] * NSA_R, axis=1)
    mw = jnp.max(srw, axis=0, keepdims=True)
    prw = jnp.exp(srw - mw)
    denw = jnp.sum(prw, axis=0, keepdims=True)
    o_wt = jnp.dot(vwt, prw.astype(bf16), preferred_element_type=f32) * (1.0 / denw)

    for r in range(NSA_R):
        lanes = []
        for c in range(3):
            l0 = GLA_LR + r * 3 + c
            l1 = l0 + NSA_R * 3
            lanes.append(jnp.where(g == 0, misc_ref[:, l0:l0 + 1], misc_ref[:, l1:l1 + 1]))
        gc, gs, gw = [1.0 / (1.0 + jnp.exp(-x)) for x in lanes]
        hs = slice(r * NSA_D, (r + 1) * NSA_D)
        cs = slice(r * tq, (r + 1) * tq)
        y = gc * oc_ref[:, hs] + gs * o_st[:, cs].T + gw * o_wt[:, cs].T
        o_ref[:, hs] = _group_rms(y, gn_ref[:, hs]).astype(bf16)


def _nsa_attn(p, q_norm, o_c, selt, flags, ks, vst, kw, vwt, gain, batch, seq):
    tq, tk = TQ, TK
    nq = seq // tq
    nkt = seq // tk
    n_sel = seq // SEL_BLOCK
    t = batch * seq
    hw = NSA_R * NSA_D
    slab = lambda rows: pl.BlockSpec((1, rows, NSA_D), lambda b, g, i, f: (b * NSA_G + g, 0, 0),
                                     pipeline_mode=pl.Buffered(1))
    slab_t = lambda rows: pl.BlockSpec((1, NSA_D, rows), lambda b, g, i, f: (b * NSA_G + g, 0, 0),
                                       pipeline_mode=pl.Buffered(1))
    grid_spec = pltpu.PrefetchScalarGridSpec(
        num_scalar_prefetch=1,
        grid=(batch, NSA_G, nq),
        in_specs=[pl.BlockSpec((tq, hw), lambda b, g, i, f: (b * nq + i, OFF_NQ // hw + g)),
                  pl.BlockSpec((1, NSA_D), lambda b, g, i, f: (0, 0)),
                  pl.BlockSpec((tq, 128), lambda b, g, i, f: (b * nq + i, OFF_MISC // 128)),
                  pl.BlockSpec((tq, hw), lambda b, g, i, f: (b * nq + i, g)),
                  pl.BlockSpec((1, n_sel, tq), lambda b, g, i, f: (b * NSA_G + g, 0, i)),
                  slab(seq), slab_t(seq), slab(seq + WINDOW), slab_t(seq + WINDOW),
                  pl.BlockSpec((1, hw), lambda b, g, i, f: (0, g))],
        out_specs=pl.BlockSpec((tq, hw), lambda b, g, i, f: (b * nq + i, g)),
        scratch_shapes=[pltpu.VMEM((1, NSA_R * tq), f32), pltpu.VMEM((1, NSA_R * tq), f32),
                        pltpu.VMEM((NSA_D, NSA_R * tq), f32)],
    )
    return pl.pallas_call(
        functools.partial(_nsa_attn_kernel, tq=tq, tk=tk, nkt=nkt),
        grid_spec=grid_spec,
        out_shape=jax.ShapeDtypeStruct((t, NSA_G * hw), bf16),
        compiler_params=_cp(("parallel", "parallel", "arbitrary")),
        name="nsa_attention",
    )(flags, p, q_norm.astype(f32).reshape(1, NSA_D), p, o_c, selt, ks, vst, kw, vwt, gain.reshape(1, -1))


def _out_proj_kernel(a_ref, b_ref, c_ref, wa_ref, wb_ref, wc_ref, h_ref, o_ref):
    o_ref[...] = (h_ref[...]
                  + jnp.dot(a_ref[...], wa_ref[...], preferred_element_type=f32)
                  + jnp.dot(b_ref[...], wb_ref[...], preferred_element_type=f32)
                  + jnp.dot(c_ref[...], wc_ref[...], preferred_element_type=f32))


def _out_proj(ya, yb, yc, w_out, h):
    t, d = h.shape
    tm, tn = TM_PROJ, 1024
    w = w_out.astype(bf16)
    ka, kb, kc = ya.shape[1], yb.shape[1], yc.shape[1]
    return pl.pallas_call(
        _out_proj_kernel,
        grid=(t // tm, d // tn),
        in_specs=[pl.BlockSpec((tm, ka), lambda i, j: (i, 0)),
                  pl.BlockSpec((tm, kb), lambda i, j: (i, 0)),
                  pl.BlockSpec((tm, kc), lambda i, j: (i, 0)),
                  pl.BlockSpec((ka, tn), lambda i, j: (0, j)),
                  pl.BlockSpec((kb, tn), lambda i, j: (1, j)),
                  pl.BlockSpec((kc, tn), lambda i, j: (1, j)),
                  pl.BlockSpec((tm, tn), lambda i, j: (i, j))],
        out_specs=pl.BlockSpec((tm, tn), lambda i, j: (i, j)),
        out_shape=jax.ShapeDtypeStruct((t, d), f32),
        compiler_params=_cp(("parallel", "parallel")),
        name="out_proj",
    )(ya, yb, yc, w, w, w, h)


def _top_k_columns(v, k, ids=None, want_rank=True):
    if ids is None:
        ids = lax.broadcasted_iota(jnp.int32, v.shape, 0)
    big = jnp.int32(2 ** 30)
    kidx = lax.broadcasted_iota(jnp.int32, (k, v.shape[1]), 0)
    vals = jnp.zeros((k, v.shape[1]), f32)
    firsts = []
    rank = jnp.full(v.shape, float(k), f32) if want_rank else None
    for kk in range(k):
        m = jnp.max(v, axis=0, keepdims=True)
        first = jnp.min(jnp.where(v == m, ids, big), axis=0, keepdims=True)
        hit = ids == first
        if want_rank:
            rank = jnp.where(hit, float(kk), rank)
        v = jnp.where(hit, -jnp.inf, v)
        vals = jnp.where(kidx == kk, m, vals)
        firsts.append(first)
    return vals, rank, firsts


def _pair_candidates(t1, t2):
    n = t1.shape[1]
    k = PEER_TOPK
    i16 = lax.broadcasted_iota(jnp.int32, (k, n), 0)
    i8 = lax.broadcasted_iota(jnp.int32, (8, n), 0)
    vals = [t1[0:1, :] + t2, t1[1:2, :] + t2[0:8, :]]
    pos = [i16, k + i8]
    for a in range(2, 8):
        vals.append(jnp.where(i8 < k // (a + 1), t1[a:a + 1, :] + t2[0:8, :], -jnp.inf))
        pos.append(a * k + i8)
    vals.append(t1[8:16, :] + t2[0:1, :])
    pos.append((8 + i8) * k)
    return jnp.concatenate(vals, axis=0), jnp.concatenate(pos, axis=0)


def _peer_route_kernel(q_ref, key_ref, a_ref, cnt_ref, rank2_ref, e2_ref):
    tm = q_ref.shape[0]
    for h in range(PEER_H):
        sc, tops, ranks = [], [], []
        for half in range(2):
            c0 = (h * 2 + half) * PEER_NK
            qh = q_ref[:, c0:c0 + PEER_NK].astype(bf16)
            s = lax.dot_general(key_ref[h * 2 + half], qh, (((1,), (1,)), ((), ())),
                                preferred_element_type=f32)
            vals, rank, _ = _top_k_columns(s, PEER_TOPK)
            sc.append(s)
            tops.append(vals)
            ranks.append(rank)
        cand, cand_pos = _pair_candidates(tops[0], tops[1])
        best, _, pos = _top_k_columns(cand, PEER_TOPK, ids=cand_pos, want_rank=False)
        mass = jnp.sum(jnp.exp(best - best[0:1, :]), axis=0, keepdims=True)
        aidx = lax.broadcasted_iota(jnp.int32, (PEER_TOPK, tm), 0)
        per_rank = jnp.zeros((PEER_TOPK, tm), f32)
        for p in pos:
            per_rank = per_rank + jnp.where(aidx == p // PEER_TOPK, 1.0, 0.0)
        cnt = jnp.zeros((PEER_NK, tm), f32)
        for a in range(PEER_TOPK):
            cnt = cnt + jnp.where(ranks[0] == float(a), per_rank[a:a + 1, :], 0.0)
        hs = slice(h * PEER_NK, (h + 1) * PEER_NK)
        a_ref[hs, :] = (jnp.exp(sc[0] - tops[0][0:1, :]) * (0.5 / mass)).astype(bf16)
        cnt_ref[hs, :] = cnt.astype(bf16)
        rank2_ref[hs, :] = ranks[1].astype(bf16)
        e2_ref[hs, :] = jnp.exp(sc[1] - tops[1][0:1, :]).astype(bf16)


def _peer_route(qry, keys):
    t = qry.shape[0]
    tm = TM_ROUTE
    narrow = jax.ShapeDtypeStruct((PEER_H * PEER_NK, t), bf16)
    nspec = pl.BlockSpec((PEER_H * PEER_NK, tm), lambda i: (0, i))
    return pl.pallas_call(
        _peer_route_kernel,
        grid=(t // tm,),
        in_specs=[pl.BlockSpec((tm, PEER_H * 2 * PEER_NK), lambda i: (i, 0)),
                  pl.BlockSpec((PEER_H * 2, PEER_NK, PEER_NK), lambda i: (0, 0, 0))],
        out_specs=[nspec] * 4,
        out_shape=[narrow] * 4,
        compiler_params=_cp(("parallel",)),
        name="peer_route",
    )(qry, keys.astype(bf16).reshape(PEER_H * 2, PEER_NK, PEER_NK))


def _peer_expert_kernel(xn_ref, u_ref, vt_ref, a_ref, cnt_ref, rank2_ref, e2_ref, h_ref, o_ref,
                        acc_ref, act_ref, gt_ref, arow_ref, crow_ref, *, tm, te):
    j = pl.program_id(1)

    @pl.when(j == 0)
    def _():
        acc_ref[...] = jnp.zeros_like(acc_ref)

    n_first = te // PEER_NK
    i0 = pl.multiple_of(j * n_first, n_first)

    a_rows = [a_ref[pl.ds(pl.multiple_of(h * PEER_NK + i0, n_first), n_first), :] for h in range(PEER_H)]
    c_rows = [cnt_ref[pl.ds(pl.multiple_of(h * PEER_NK + i0, n_first), n_first), :] for h in range(PEER_H)]
    for h in range(PEER_H):
        arow_ref[h * n_first:(h + 1) * n_first, :] = a_rows[h]
        crow_ref[h * n_first:(h + 1) * n_first, :] = c_rows[h]

    x = lax.dot_general(u_ref[...], xn_ref[...], (((1,), (1,)), ((), ())), preferred_element_type=f32)
    act_ref[...] = (x * (1.0 + jnp.tanh(x * (GELU_C + (GELU_C * GELU_A) * (x * x))))).astype(bf16)

    jb = PEER_JB
    zero = jnp.zeros((jb, BF16_LANES), bf16)
    for c in range(tm // BF16_LANES):
        ls = slice(c * BF16_LANES, (c + 1) * BF16_LANES)
        for u in range(n_first):
            for q in range(PEER_NK // jb):
                rs = slice(u * PEER_NK + q * jb, u * PEER_NK + (q + 1) * jb)
                w = zero
                for h in range(PEER_H):
                    hs = slice(h * PEER_NK + q * jb, h * PEER_NK + (q + 1) * jb)
                    a = arow_ref[h * n_first + u:h * n_first + u + 1, ls]
                    cnt = crow_ref[h * n_first + u:h * n_first + u + 1, ls]
                    keep = rank2_ref[hs, ls] < cnt
                    w = w + jnp.where(keep, e2_ref[hs, ls], zero) * a
                gt_ref[rs, ls] = w * act_ref[rs, ls]

    acc_ref[...] += jnp.dot(vt_ref[...], gt_ref[...], preferred_element_type=f32)

    @pl.when(j == pl.num_programs(1) - 1)
    def _():
        o_ref[...] = h_ref[...] + acc_ref[...].T


def _peer_experts(xn, u, vt, a, cnt, rank2, e2, h):
    t, d = h.shape
    tm, te = TM_PEER, TE_PEER
    nspec = pl.BlockSpec((PEER_H * PEER_NK, tm), lambda i, j: (0, i))
    return pl.pallas_call(
        functools.partial(_peer_expert_kernel, tm=tm, te=te),
        grid=(t // tm, PEER_E // te),
        in_specs=[pl.BlockSpec((tm, d), lambda i, j: (i, 0)),
                  pl.BlockSpec((te, d), lambda i, j: (j, 0)),
                  pl.BlockSpec((d, te), lambda i, j: (0, j)),
                  nspec, nspec, nspec, nspec,
                  pl.BlockSpec((tm, d), lambda i, j: (i, 0))],
        out_specs=pl.BlockSpec((tm, d), lambda i, j: (i, 0)),
        out_shape=jax.ShapeDtypeStruct((t, d), f32),
        scratch_shapes=[pltpu.VMEM((d, tm), f32), pltpu.VMEM((te, tm), bf16), pltpu.VMEM((te, tm), bf16),
                        pltpu.VMEM((PEER_H * te // PEER_NK, tm), bf16),
                        pltpu.VMEM((PEER_H * te // PEER_NK, tm), bf16)],
        compiler_params=_cp(("parallel", "arbitrary")),
        name="peer_experts",
    )(xn, u, vt, a, cnt, rank2, e2, h)


def _transpose_cast_kernel(x_ref, o_ref):
    o_ref[...] = x_ref[...].T.astype(bf16)


def _transpose_cast(x):
    r, c = x.shape
    tr = TE_PEER
    return pl.pallas_call(
        _transpose_cast_kernel,
        grid=(r // tr,),
        in_specs=[pl.BlockSpec((tr, c), lambda i: (i, 0))],
        out_specs=pl.BlockSpec((c, tr), lambda i: (0, i)),
        out_shape=jax.ShapeDtypeStruct((c, r), bf16),
        compiler_params=_cp(("parallel",)),
        name="transpose_cast",
    )(x)


def _permute_w_in(w):
    pad = jnp.zeros((w.shape[0], N_IN - ORIG_END), w.dtype)
    return jnp.concatenate([w[:, :ORIG_GA], w[:, ORIG_NQ:ORIG_GATE], w[:, ORIG_GA:ORIG_NQ],
                            w[:, ORIG_GATE:ORIG_END], pad], axis=1).astype(bf16)


def _tile_flags(selt, seq):
    bg = selt.shape[0]
    any_sel = selt.reshape(bg, seq // TK, TK // SEL_BLOCK, seq // TQ, TQ).max(axis=(2, 4))
    return (jnp.swapaxes(any_sel, 1, 2) > 0.5).astype(jnp.int32).reshape(-1)


def _mixers(h, norm_mix, w_in, conv_w, gla_w_a2, gla_b_a, q_norm, k_norm, cmp_pos, cmp_w1, cmp_w2, out_norm,
            batch, seq):
    p = _norm_matmul(h, norm_mix.astype(f32), _permute_w_in(w_in), TN_IN)
    gain = out_norm.astype(f32)
    y_conv = _conv_mixer(p, conv_w, gain[:CONV_W], seq)
    y_gla = _gla_mixer(p, gla_w_a2, gla_b_a, gain[CONV_W:2 * CONV_W], batch, seq)
    kcc = _compress(p, OFF_NKC, cmp_pos[0], cmp_w1[0], cmp_w2[0], k_norm[0], batch, seq, True)
    vcc = _compress(p, OFF_NVC, cmp_pos[1], cmp_w1[1], cmp_w2[1], k_norm[0], batch, seq, False)
    ks, vs, kw, vw = _kv_prep(p, k_norm, batch, seq)
    o_c, selt = _nsa_cmp(p, q_norm, kcc, vcc, batch, seq)
    y_nsa = _nsa_attn(p, q_norm, o_c, selt, _tile_flags(selt, seq), ks, vs, kw, vw, gain[2 * CONV_W:], batch, seq)
    return y_conv, y_gla, y_nsa


def _layer(h, norm_mix, w_in, conv_w, gla_w_a2, gla_b_a, q_norm, k_norm, cmp_pos, cmp_w1, cmp_w2, out_norm, w_out,
           norm_ffn, peer_w_q, peer_keys, peer_u, peer_v, batch, seq):
    y_conv, y_gla, y_nsa = _mixers(h, norm_mix, w_in, conv_w, gla_w_a2, gla_b_a, q_norm, k_norm, cmp_pos, cmp_w1,
                                   cmp_w2, out_norm, batch, seq)
    h = _out_proj(y_conv, y_gla, y_nsa, w_out, h)
    qry, xn = _norm_matmul(h, norm_ffn.astype(f32), peer_w_q.astype(bf16), 1024, emit_xn=True)
    a, cnt, rank2, e2 = _peer_route(qry, peer_keys)
    return _peer_experts(xn, peer_u.astype(bf16), _transpose_cast(peer_v.astype(f32)), a, cnt, rank2, e2, h)


def kernel(x, norm_mix, w_in, conv_w, gla_w_a2, gla_b_a, nsa_q_norm, nsa_k_norm, nsa_cmp_pos, nsa_cmp_w1,
           nsa_cmp_w2, out_norm, w_out, norm_ffn, peer_w_q, peer_keys, peer_u, peer_v):
    batch, seq, d = x.shape
    h = x.reshape(batch * seq, d)
    for l in range(w_in.shape[0]):
        h = _layer(h, norm_mix[l], w_in[l], conv_w[l], gla_w_a2[l], gla_b_a[l], nsa_q_norm[l], nsa_k_norm[l],
                   nsa_cmp_pos[l], nsa_cmp_w1[l], nsa_cmp_w2[l], out_norm[l], w_out[l], norm_ffn[l], peer_w_q[l],
                   peer_keys[l], peer_u[l], peer_v[l], batch, seq)
    return h.reshape(batch, seq, d)
```

```python
import functools
import math

import numpy as np
import jax
import jax.numpy as jnp
from jax import lax
from jax.experimental import pallas as pl
from jax.experimental.pallas import tpu as pltpu

f32 = jnp.float32
bf16 = jnp.bfloat16

D_MODEL = 2048
EPS = 1e-6
HEAD = 128
CONV_W = 512
CONV_K = 3
GLA_H, GLA_DK, GLA_DV = 4, 64, 128
GLA_LR = 16
GLA_TAU = 16.0
GLA_CHUNK = 64
NSA_G, NSA_R, NSA_D = 2, 4, 128
CMP_LEN, CMP_STRIDE = 32, 16
SEL_BLOCK, SEL_TOPK, SEL_LOCAL = 64, 16, 2
WINDOW = 512
BIG = 1e9
PEER_H, PEER_NK, PEER_TOPK = 8, 128, 16
PEER_E = PEER_NK * PEER_NK
NEG = -1e30
BF16_LANES = 256

OFF_CX, OFF_CB, OFF_CC = 0, 512, 1024
OFF_GQ, OFF_GK, OFF_GV, OFF_GG = 1536, 1792, 2048, 2560
OFF_NQ = 3072
OFF_NKC, OFF_NVC, OFF_NKS, OFF_NVS, OFF_NKW, OFF_NVW = 4096, 4352, 4608, 4864, 5120, 5376
OFF_MISC = 5632
N_IN = 5760
ORIG_GA, ORIG_NQ, ORIG_GATE, ORIG_END = 3072, 3088, 5648, 5672

TM_PROJ = 1024
TN_IN = 1152
TS_ELT = 512
TC_GLA = 512
TQ = 128
TK = 512
CMP_ROWS = 256
TM_PEER = 512
TE_PEER = 1024
TM_ROUTE = 256
PEER_JB = 128
VMEM_LIMIT = 56 * 1024 * 1024
GELU_C = math.sqrt(2.0 / math.pi)
GELU_A = 0.044715


def _cp(sem):
    return pltpu.CompilerParams(dimension_semantics=sem, vmem_limit_bytes=VMEM_LIMIT)


def _gelu(x):
    return 0.5 * x * (1.0 + jnp.tanh(GELU_C * (x + GELU_A * (x * x * x))))


def _split_bf16(x):
    hi = x.astype(bf16)
    lo = (x - hi.astype(f32)).astype(bf16)
    return hi, lo


def _group_rms(y, gain):
    ms = jnp.mean(y * y, axis=-1, keepdims=True)
    return y * lax.rsqrt(ms + EPS) * gain


def _norm_matmul_kernel(x_ref, g_ref, w_ref, *refs, emit_xn):
    if emit_xn:
        o_ref, xo_ref, xn_ref = refs
    else:
        o_ref, xn_ref = refs

    @pl.when(pl.program_id(1) == 0)
    def _():
        x = x_ref[...]
        ms = jnp.mean(x * x, axis=-1, keepdims=True)
        xn = (x * lax.rsqrt(ms + EPS) * g_ref[...]).astype(bf16)
        xn_ref[...] = xn
        if emit_xn:
            xo_ref[...] = xn

    o_ref[...] = jnp.dot(xn_ref[...], w_ref[...], preferred_element_type=f32)


def _norm_matmul(x, gain, w, tn, emit_xn=False):
    t, d = x.shape
    n = w.shape[1]
    tm = TM_PROJ
    out_shape = [jax.ShapeDtypeStruct((t, n), f32)]
    out_specs = [pl.BlockSpec((tm, tn), lambda i, j: (i, j))]
    if emit_xn:
        out_shape.append(jax.ShapeDtypeStruct((t, d), bf16))
        out_specs.append(pl.BlockSpec((tm, d), lambda i, j: (i, 0)))
    outs = pl.pallas_call(
        functools.partial(_norm_matmul_kernel, emit_xn=emit_xn),
        grid=(t // tm, n // tn),
        in_specs=[
            pl.BlockSpec((tm, d), lambda i, j: (i, 0)),
            pl.BlockSpec((1, d), lambda i, j: (0, 0)),
            pl.BlockSpec((d, tn), lambda i, j: (0, j)),
        ],
        out_specs=out_specs,
        out_shape=out_shape,
        scratch_shapes=[pltpu.VMEM((tm, d), bf16)],
        compiler_params=_cp(("parallel", "arbitrary")),
        name="norm_matmul",
    )(x, gain.reshape(1, d), w)
    return outs if emit_xn else outs[0]


def _conv_kernel(cx_ref, cb_ref, cc_ref, pcx_ref, pcc_ref, w_ref, g_ref, o_ref, z_ref, *, ts, tiles_per_seq):
    first = (pl.program_id(0) % tiles_per_seq) == 0
    zp = pcx_ref[...] * pcc_ref[...]
    z_ref[0:8, :] = jnp.where(first, 0.0, zp)
    z_ref[8:8 + ts, :] = cx_ref[...] * cc_ref[...]
    y = (z_ref[8:8 + ts, :] * w_ref[0:1, :]
         + z_ref[7:7 + ts, :] * w_ref[1:2, :]
         + z_ref[6:6 + ts, :] * w_ref[2:3, :])
    y = cb_ref[...] * y
    for c in range(CONV_W // HEAD):
        sl = slice(c * HEAD, (c + 1) * HEAD)
        o_ref[:, sl] = _group_rms(y[:, sl], g_ref[:, sl]).astype(bf16)


def _conv_mixer(p, conv_w, gain, seq):
    t = p.shape[0]
    ts = TS_ELT
    w_t = conv_w.astype(f32).T
    cur = lambda off: pl.BlockSpec((ts, CONV_W), lambda i, off=off: (i, off // CONV_W))
    prev = lambda off: pl.BlockSpec((8, CONV_W), lambda i, off=off: (jnp.maximum(i * (ts // 8) - 1, 0), off // CONV_W))
    return pl.pallas_call(
        functools.partial(_conv_kernel, ts=ts, tiles_per_seq=seq // ts),
        grid=(t // ts,),
        in_specs=[cur(OFF_CX), cur(OFF_CB), cur(OFF_CC), prev(OFF_CX), prev(OFF_CC),
                  pl.BlockSpec((CONV_K, CONV_W), lambda i: (0, 0)),
                  pl.BlockSpec((1, CONV_W), lambda i: (0, 0))],
        out_specs=pl.BlockSpec((ts, CONV_W), lambda i: (i, 0)),
        out_shape=jax.ShapeDtypeStruct((t, CONV_W), bf16),
        scratch_shapes=[pltpu.VMEM((ts + 8, CONV_W), f32)],
        compiler_params=_cp(("parallel",)),
        name="conv_mixer",
    )(p, p, p, p, p, w_t, gain.reshape(1, CONV_W))


def _gla_kernel(q_ref, k_ref, v_ref, gg_ref, misc_ref, wa_ref, ba_ref, gn_ref, o_ref,
                st_ref, la_ref, y_ref, *, tc):
    @pl.when(pl.program_id(1) == 0)
    def _():
        st_ref[...] = jnp.zeros_like(st_ref)

    a_hi, a_lo = _split_bf16(misc_ref[:, 0:GLA_LR])
    w_hi, w_lo = _split_bf16(wa_ref[...])
    z = (jnp.dot(a_hi, w_hi, preferred_element_type=f32)
         + jnp.dot(a_hi, w_lo, preferred_element_type=f32)
         + jnp.dot(a_lo, w_hi, preferred_element_type=f32)) + ba_ref[...]
    la_ref[...] = (jnp.minimum(z, 0.0) - jnp.log(1.0 + jnp.exp(-jnp.abs(z)))) * (1.0 / GLA_TAU)

    c = GLA_CHUNK
    row = lax.broadcasted_iota(jnp.int32, (c, c), 0)
    col = lax.broadcasted_iota(jnp.int32, (c, c), 1)
    causal = row >= col
    tril = causal.astype(bf16)
    scale = GLA_DK ** -0.5

    def chunk(ci, carry):
        r0 = pl.multiple_of(ci * c, c)
        la = la_ref[pl.ds(r0, c), :]
        la_hi, la_lo = _split_bf16(la)
        la_lo2 = (la - la_hi.astype(f32) - la_lo.astype(f32)).astype(bf16)
        bc = (jnp.dot(tril, la_hi, preferred_element_type=f32)
              + jnp.dot(tril, la_lo, preferred_element_type=f32)
              + jnp.dot(tril, la_lo2, preferred_element_type=f32))
        bl = bc[c - 1:c, :]
        qc = q_ref[pl.ds(r0, c), :]
        kc = k_ref[pl.ds(r0, c), :]
        vc = v_ref[pl.ds(r0, c), :].astype(bf16)
        q_dec = (qc * scale * jnp.exp(bc)).astype(bf16)
        k_inv = (kc * jnp.exp(-bc)).astype(bf16)
        k_end = (kc * jnp.exp(bl - bc)).astype(bf16)
        decay = jnp.exp(bl)
        for h in range(GLA_H):
            ks = slice(h * GLA_DK, (h + 1) * GLA_DK)
            vs = slice(h * GLA_DV, (h + 1) * GLA_DV)
            att = lax.dot_general(q_dec[:, ks], k_inv[:, ks], (((1,), (1,)), ((), ())),
                                  preferred_element_type=f32)
            att = jnp.where(causal, att, 0.0).astype(bf16)
            o = jnp.dot(att, vc[:, vs], preferred_element_type=f32)
            st = st_ref[h]
            o = o + lax.dot_general(q_dec[:, ks], st.astype(bf16), (((1,), (1,)), ((), ())),
                                    preferred_element_type=f32)
            y_ref[pl.ds(r0, c), vs] = o
            upd = lax.dot_general(vc[:, vs], k_end[:, ks], (((0,), (0,)), ((), ())),
                                  preferred_element_type=f32)
            st_ref[h] = st * decay[:, ks] + upd
        return carry

    lax.fori_loop(0, tc // c, chunk, 0)

    g = gg_ref[...]
    sw = g * (1.0 / (1.0 + jnp.exp(-g)))
    for h in range(GLA_H):
        vs = slice(h * GLA_DV, (h + 1) * GLA_DV)
        o_ref[:, vs] = (_group_rms(y_ref[:, vs], gn_ref[:, vs]) * sw[:, vs]).astype(bf16)


def _gla_mixer(p, w_a2, b_a, gain, batch, seq):
    t = p.shape[0]
    tc = TC_GLA
    nst = seq // tc
    blk = lambda width, off: pl.BlockSpec((tc, width), lambda b, s, off=off, width=width: (b * nst + s, off // width))
    const = lambda shape: pl.BlockSpec(shape, lambda b, s: (0,) * len(shape))
    return pl.pallas_call(
        functools.partial(_gla_kernel, tc=tc),
        grid=(batch, nst),
        in_specs=[blk(256, OFF_GQ), blk(256, OFF_GK), blk(512, OFF_GV), blk(512, OFF_GG), blk(128, OFF_MISC),
                  const((GLA_LR, GLA_H * GLA_DK)), const((1, GLA_H * GLA_DK)), const((1, GLA_H * GLA_DV))],
        out_specs=pl.BlockSpec((tc, GLA_H * GLA_DV), lambda b, s: (b * nst + s, 0)),
        out_shape=jax.ShapeDtypeStruct((t, GLA_H * GLA_DV), bf16),
        scratch_shapes=[pltpu.VMEM((GLA_H, GLA_DV, GLA_DK), f32),
                        pltpu.VMEM((tc, GLA_H * GLA_DK), f32),
                        pltpu.VMEM((tc, GLA_H * GLA_DV), f32)],
        compiler_params=_cp(("parallel", "arbitrary")),
        name="gla_mixer",
    )(p, p, p, p, p, w_a2.astype(f32), b_a.astype(f32).reshape(1, -1), gain.reshape(1, -1))


def _compress_kernel(t_ref, pos_ref, w1_ref, w2_ref, gn_ref, o_ref, b_scr, *, n_piece, apply_norm):
    half = CMP_STRIDE
    acc_a = jnp.zeros((n_piece, NSA_D), f32)
    acc_b = jnp.zeros((n_piece, NSA_D), f32)
    for r in range(half):
        rows = t_ref[pl.ds(r, n_piece, stride=CMP_STRIDE), :]
        xa = (rows + pos_ref[r:r + 1, :]).astype(bf16)
        acc_a = acc_a + jnp.dot(xa, w1_ref[r], preferred_element_type=f32)
        xb = (rows + pos_ref[half + r:half + r + 1, :]).astype(bf16)
        acc_b = acc_b + jnp.dot(xb, w1_ref[half + r], preferred_element_type=f32)
    b_scr[0:n_piece, :] = acc_b
    b_scr[n_piece:n_piece + 8, :] = jnp.zeros((8, NSA_D), f32)
    hid = acc_a + b_scr[1:n_piece + 1, :]
    out = jnp.dot(_gelu(hid).astype(bf16), w2_ref[...], preferred_element_type=f32)
    if apply_norm:
        out = _group_rms(out, gn_ref[...])
    o_ref[0] = out.astype(bf16)


def _compress(p, off, pos, w1, w2, gain, batch, seq, apply_norm):
    n_piece = seq // CMP_STRIDE
    return pl.pallas_call(
        functools.partial(_compress_kernel, n_piece=n_piece, apply_norm=apply_norm),
        grid=(batch, NSA_G),
        in_specs=[pl.BlockSpec((seq, NSA_D), lambda b, g: (b, off // NSA_D + g)),
                  pl.BlockSpec((CMP_LEN, NSA_D), lambda b, g: (0, 0)),
                  pl.BlockSpec((CMP_LEN, NSA_D, NSA_D), lambda b, g: (0, 0, 0)),
                  pl.BlockSpec((NSA_D, NSA_D), lambda b, g: (0, 0)),
                  pl.BlockSpec((1, NSA_D), lambda b, g: (0, 0))],
        out_specs=pl.BlockSpec((1, n_piece, NSA_D), lambda b, g: (b * NSA_G + g, 0, 0)),
        out_shape=jax.ShapeDtypeStruct((batch * NSA_G, n_piece, NSA_D), bf16),
        scratch_shapes=[pltpu.VMEM((n_piece + 8, NSA_D), f32)],
        compiler_params=_cp(("parallel", "parallel")),
        name="nsa_compress",
    )(p, pos.astype(f32), w1.astype(bf16).reshape(CMP_LEN, NSA_D, NSA_D), w2.astype(bf16),
      gain.astype(f32).reshape(1, NSA_D))


def _kvprep_kernel(ks_ref, vs_ref, kw_ref, vw_ref, gs_ref, gw_ref, oks_ref, ovs_ref, okw_ref, ovw_ref):
    lead = pl.program_id(2) == 0
    oks_ref[0] = _group_rms(ks_ref[...], gs_ref[...]).astype(bf16)
    ovs_ref[0] = vs_ref[...].T.astype(bf16)
    okw_ref[0] = jnp.where(lead, 0.0, _group_rms(kw_ref[...], gw_ref[...])).astype(bf16)
    ovw_ref[0] = jnp.where(lead, 0.0, vw_ref[...]).T.astype(bf16)


def _kv_prep(p, k_norm, batch, seq):
    ts = WINDOW
    nst = seq // ts
    src = lambda s: jnp.maximum(s - 1, 0)
    blk = lambda off: pl.BlockSpec((ts, NSA_D), lambda b, g, s, off=off: (b * nst + src(s), off // NSA_D + g))
    gspec = pl.BlockSpec((1, NSA_D), lambda b, g, s: (0, 0))
    kspec = pl.BlockSpec((1, ts, NSA_D), lambda b, g, s: (b * NSA_G + g, src(s), 0))
    vspec = pl.BlockSpec((1, NSA_D, ts), lambda b, g, s: (b * NSA_G + g, 0, src(s)))
    kwspec = pl.BlockSpec((1, ts, NSA_D), lambda b, g, s: (b * NSA_G + g, s, 0))
    vwspec = pl.BlockSpec((1, NSA_D, ts), lambda b, g, s: (b * NSA_G + g, 0, s))
    bg = batch * NSA_G
    return pl.pallas_call(
        _kvprep_kernel,
        grid=(batch, NSA_G, nst + 1),
        in_specs=[blk(OFF_NKS), blk(OFF_NVS), blk(OFF_NKW), blk(OFF_NVW), gspec, gspec],
        out_specs=[kspec, vspec, kwspec, vwspec],
        out_shape=[jax.ShapeDtypeStruct((bg, seq, NSA_D), bf16), jax.ShapeDtypeStruct((bg, NSA_D, seq), bf16),
                   jax.ShapeDtypeStruct((bg, seq + WINDOW, NSA_D), bf16),
                   jax.ShapeDtypeStruct((bg, NSA_D, seq + WINDOW), bf16)],
        compiler_params=_cp(("parallel", "parallel", "arbitrary")),
        name="nsa_kv_prep",
    )(p, p, p, p, k_norm[1].astype(f32).reshape(1, NSA_D), k_norm[2].astype(f32).reshape(1, NSA_D))


def _stack_q(q_ref, qn_ref):
    parts = []
    for r in range(NSA_R):
        qn = _group_rms(q_ref[:, r * NSA_D:(r + 1) * NSA_D], qn_ref[...])
        parts.append((qn * (NSA_D ** -0.5)).astype(bf16))
    return jnp.concatenate(parts, axis=0)


def _head_slope(g, r):
    return jnp.where(g == 0, 1.0, 2.0 ** -NSA_R).astype(f32) * (2.0 ** -(r + 1))


def _nsa_cmp_kernel(q_ref, qn_ref, kcc_ref, vcc_ref, ovt_ref, oc_ref, selt_ref, any_ref, s_scr, p_scr, psum_scr,
                    *, tq, n_cmp, n_sel):
    g = pl.program_id(1)
    q0 = pl.program_id(2) * tq
    qs = _stack_q(q_ref, qn_ref)
    s_scr[...] = lax.dot_general(kcc_ref[0], qs, (((1,), (1,)), ((), ())), preferred_element_type=f32)
    rc = CMP_ROWS
    n_chunk = n_cmp // rc
    n_vis = jnp.clip((q0 + tq - CMP_LEN) // CMP_STRIDE + 1, 0, n_cmp)
    n_act = (n_vis + rc - 1) // rc
    tcol = lax.broadcasted_iota(jnp.int32, (rc, tq), 1)
    nrow = lax.broadcasted_iota(jnp.int32, (rc, tq), 0)

    def rel_end(c):
        return (c * rc + nrow) * CMP_STRIDE + (CMP_LEN - 1 - q0)

    for r in range(NSA_R):
        cs = slice(r * tq, (r + 1) * tq)
        slope = _head_slope(g, r)

        def biased(c, m):
            r0 = pl.multiple_of(c * rc, rc)
            rel = rel_end(c)
            sb = jnp.where(rel <= tcol, s_scr[pl.ds(r0, rc), cs] + slope * rel.astype(f32), NEG)
            s_scr[pl.ds(r0, rc), cs] = sb
            return jnp.maximum(m, jnp.max(sb, axis=0, keepdims=True))

        m = lax.fori_loop(0, n_act, biased, jnp.full((1, tq), NEG, f32))
        m = jnp.maximum(m, 0.1 * NEG)

        def expo(c, l):
            r0 = pl.multiple_of(c * rc, rc)
            e = jnp.exp(s_scr[pl.ds(r0, rc), cs] - m)
            s_scr[pl.ds(r0, rc), cs] = e
            return l + jnp.sum(e, axis=0, keepdims=True)

        l = lax.fori_loop(0, n_act, expo, jnp.zeros((1, tq), f32))
        inv = 1.0 / jnp.maximum(l, 1e-30)

        def normalise(c, carry):
            r0 = pl.multiple_of(c * rc, rc)
            pn = s_scr[pl.ds(r0, rc), cs] * inv
            p_scr[pl.ds(r0, rc), cs] = pn.astype(bf16)
            if r == 0:
                psum_scr[pl.ds(r0, rc), :] = pn
            else:
                psum_scr[pl.ds(r0, rc), :] += pn
            return carry

        lax.fori_loop(0, n_act, normalise, 0)

    def clear(c, carry):
        r0 = pl.multiple_of(c * rc, rc)
        p_scr[pl.ds(r0, rc), :] = jnp.zeros((rc, NSA_R * tq), bf16)
        psum_scr[pl.ds(r0, rc), :] = jnp.zeros((rc, tq), f32)
        return carry

    lax.fori_loop(n_act, n_chunk, clear, 0)

    vcc = vcc_ref[0]
    for r in range(NSA_R):
        oc_ref[:, r * NSA_D:(r + 1) * NSA_D] = lax.dot_general(
            p_scr[:, r * tq:(r + 1) * tq], vcc, (((0,), (0,)), ((), ())), preferred_element_type=f32)

    p_hi, p_lo = _split_bf16(psum_scr[...])
    imp = (jnp.dot(ovt_ref[...], p_hi, preferred_element_type=f32)
           + jnp.dot(ovt_ref[...], p_lo, preferred_element_type=f32))
    blk = lax.broadcasted_iota(jnp.int32, (n_sel, tq), 0)
    tblk = (q0 + lax.broadcasted_iota(jnp.int32, (n_sel, tq), 1)) // SEL_BLOCK
    lag = tblk - blk
    score = jnp.where(blk == 0, BIG, jnp.where(lag < SEL_LOCAL, BIG, imp))
    score = jnp.where(lag >= 0, score, -1.0)
    sel = jnp.zeros((n_sel, tq), f32)
    for _ in range(SEL_TOPK):
        m = jnp.max(score, axis=0, keepdims=True)
        first = jnp.min(jnp.where(score == m, blk, n_sel), axis=0, keepdims=True)
        hit = blk == first
        sel = jnp.where(hit, 1.0, sel)
        score = jnp.where(hit, -2.0, score)
    selt_ref[0] = sel.astype(bf16)
    any_ref[0] = jnp.max(sel, axis=1, keepdims=True)


def _nsa_cmp(p, q_norm, kcc, vcc, batch, seq):
    tq = TQ
    nq = seq // tq
    n_cmp = seq // CMP_STRIDE
    n_sel = seq // SEL_BLOCK
    ci = np.arange(n_cmp)[None, :] * CMP_STRIDE
    sj = np.arange(n_sel)[:, None] * SEL_BLOCK
    ovt = np.clip(np.minimum(ci + CMP_LEN, sj + SEL_BLOCK) - np.maximum(ci, sj), 0, None) / CMP_LEN
    ovt = jnp.asarray(ovt, dtype=bf16)
    t = batch * seq
    hw = NSA_R * NSA_D
    return pl.pallas_call(
        functools.partial(_nsa_cmp_kernel, tq=tq, n_cmp=n_cmp, n_sel=n_sel),
        grid=(batch, NSA_G, nq),
        in_specs=[pl.BlockSpec((tq, hw), lambda b, g, i: (b * nq + i, OFF_NQ // hw + g)),
                  pl.BlockSpec((1, NSA_D), lambda b, g, i: (0, 0)),
                  pl.BlockSpec((1, n_cmp, NSA_D), lambda b, g, i: (b * NSA_G + g, 0, 0)),
                  pl.BlockSpec((1, n_cmp, NSA_D), lambda b, g, i: (b * NSA_G + g, 0, 0)),
                  pl.BlockSpec((n_sel, n_cmp), lambda b, g, i: (0, 0))],
        out_specs=[pl.BlockSpec((tq, hw), lambda b, g, i: (b * nq + i, g)),
                   pl.BlockSpec((1, n_sel, tq), lambda b, g, i: (b * NSA_G + g, 0, i)),
                   pl.BlockSpec((1, n_sel, 1), lambda b, g, i: ((b * NSA_G + g) * nq + i, 0, 0))],
        out_shape=[jax.ShapeDtypeStruct((t, NSA_G * hw), f32),
                   jax.ShapeDtypeStruct((batch * NSA_G, n_sel, seq), bf16),
                   jax.ShapeDtypeStruct((batch * NSA_G * nq, n_sel, 1), f32)],
        scratch_shapes=[pltpu.VMEM((n_cmp, NSA_R * tq), f32), pltpu.VMEM((n_cmp, NSA_R * tq), bf16),
                        pltpu.VMEM((n_cmp, tq), f32)],
        compiler_params=_cp(("parallel", "parallel", "parallel")),
        name="nsa_cmp_select",
    )(p, q_norm.astype(f32).reshape(1, NSA_D), kcc, vcc, ovt)


def _nsa_attn_kernel(flag_ref, q_ref, qn_ref, misc_ref, oc_ref, selt_ref, ks_ref, vst_ref, kw_ref, vwt_ref, gn_ref,
                     o_ref, m_scr, l_scr, acc_scr, *, tq, tk, nkt):
    b = pl.program_id(0)
    g = pl.program_id(1)
    qi = pl.program_id(2)
    nq = pl.num_programs(2)
    q0 = qi * tq
    cols = NSA_R * tq
    qs = _stack_q(q_ref, qn_ref)
    nt_dims = (((1,), (1,)), ((), ()))

    bpt = tk // SEL_BLOCK
    krow = lax.broadcasted_iota(jnp.int32, (tk, tq), 0)
    tcol = lax.broadcasted_iota(jnp.int32, (tk, tq), 1)
    kt_last = (q0 + tq - 1) // tk
    slopes = jnp.concatenate([jnp.full((1, tq), 1.0, f32) * _head_slope(g, r) for r in range(NSA_R)], axis=1)

    m_scr[...] = jnp.full((1, cols), NEG, f32)
    l_scr[...] = jnp.zeros((1, cols), f32)
    acc_scr[...] = jnp.zeros((NSA_D, cols), f32)

    def sel_step(kt, causal):
        k0 = pl.multiple_of(kt * tk, tk)
        kb = ks_ref[0, pl.ds(k0, tk), :]
        vtb = vst_ref[0, :, pl.ds(k0, tk)]
        st = lax.dot_general(kb, qs, nt_dims, preferred_element_type=f32)
        picked = selt_ref[0, pl.ds(pl.multiple_of(kt * bpt, bpt), bpt), :]
        off = (picked.astype(f32) - 1.0) * (-NEG)
        bias = jnp.concatenate([jnp.broadcast_to(off[u:u + 1, :], (SEL_BLOCK, tq)) for u in range(bpt)], axis=0)
        rel = krow + (k0 - q0)
        if causal:
            bias = bias + jnp.where(rel <= tcol, 0.0, NEG)
        relf = rel.astype(f32)
        sr = st + jnp.concatenate([relf] * NSA_R, axis=1) * slopes + jnp.concatenate([bias] * NSA_R, axis=1)
        m_old = m_scr[...]
        m_new = jnp.maximum(m_old, jnp.max(sr, axis=0, keepdims=True))
        alpha = jnp.exp(m_old - m_new)
        pr = jnp.exp(sr - m_new)
        l_scr[...] = alpha * l_scr[...] + jnp.sum(pr, axis=0, keepdims=True)
        acc_scr[...] = alpha * acc_scr[...] + jnp.dot(vtb, pr.astype(bf16), preferred_element_type=f32)
        m_scr[...] = m_new

    fbase = ((b * NSA_G + g) * nq + qi) * nkt

    def sel_loop(kt, carry):
        @pl.when(flag_ref[fbase + kt] > 0)
        def _():
            sel_step(kt, False)
        return carry

    lax.fori_loop(0, kt_last, sel_loop, 0)
    sel_step(kt_last, True)
    o_st = acc_scr[...] * (1.0 / jnp.maximum(l_scr[...], 1e-30))

    wk = WINDOW + tq
    w0 = pl.multiple_of(q0, tq)
    kwin = kw_ref[0, pl.ds(w0, wk), :]
    vwt = vwt_ref[0, :, pl.ds(w0, wk)]
    swt = lax.dot_general(kwin, qs, nt_dims, preferred_element_type=f32)
    jr = lax.broadcasted_iota(jnp.int32, (wk, tq), 0)
    tc = lax.broadcasted_iota(jnp.int32, (wk, tq), 1)
    wbias = jnp.where(jr > tc, jnp.where(jr <= tc + WINDOW, 0.0, NEG), NEG)
    wbias = wbias + jnp.where(jr >= WINDOW - q0, 0.0, NEG)
    jf = jr.astype(f32)
    srw = swt + jnp.concatenate([jf] * NSA_R, axis=1) * slopes + jnp.concatenate([wbias] * NSA_R, axis=1)
    mw = jnp.max(srw, axis=0, keepdims=True)
    prw = jnp.exp(srw - mw)
    denw = jnp.sum(prw, axis=0, keepdims=True)
    o_wt = jnp.dot(vwt, prw.astype(bf16), preferred_element_type=f32) * (1.0 / denw)

    for r in range(NSA_R):
        lanes = []
        for c in range(3):
            l0 = GLA_LR + r * 3 + c
            l1 = l0 + NSA_R * 3
            lanes.append(jnp.where(g == 0, misc_ref[:, l0:l0 + 1], misc_ref[:, l1:l1 + 1]))
        gc, gs, gw = [1.0 / (1.0 + jnp.exp(-x)) for x in lanes]
        hs = slice(r * NSA_D, (r + 1) * NSA_D)
        cs = slice(r * tq, (r + 1) * tq)
        y = gc * oc_ref[:, hs] + gs * o_st[:, cs].T + gw * o_wt[:, cs].T
        o_ref[:, hs] = _group_rms(y, gn_ref[:, hs]).astype(bf16)


def _nsa_attn(p, q_norm, o_c, selt, flags, ks, vst, kw, vwt, gain, batch, seq):
    tq, tk = TQ, TK
    nq = seq // tq
    nkt = seq // tk
    n_sel = seq // SEL_BLOCK
    t = batch * seq
    hw = NSA_R * NSA_D
    slab = lambda rows: pl.BlockSpec((1, rows, NSA_D), lambda b, g, i, f: (b * NSA_G + g, 0, 0),
                                     pipeline_mode=pl.Buffered(1))
    slab_t = lambda rows: pl.BlockSpec((1, NSA_D, rows), lambda b, g, i, f: (b * NSA_G + g, 0, 0),
                                       pipeline_mode=pl.Buffered(1))
    grid_spec = pltpu.PrefetchScalarGridSpec(
        num_scalar_prefetch=1,
        grid=(batch, NSA_G, nq),
        in_specs=[pl.BlockSpec((tq, hw), lambda b, g, i, f: (b * nq + i, OFF_NQ // hw + g)),
                  pl.BlockSpec((1, NSA_D), lambda b, g, i, f: (0, 0)),
                  pl.BlockSpec((tq, 128), lambda b, g, i, f: (b * nq + i, OFF_MISC // 128)),
                  pl.BlockSpec((tq, hw), lambda b, g, i, f: (b * nq + i, g)),
                  pl.BlockSpec((1, n_sel, tq), lambda b, g, i, f: (b * NSA_G + g, 0, i)),
                  slab(seq), slab_t(seq), slab(seq + WINDOW), slab_t(seq + WINDOW),
                  pl.BlockSpec((1, hw), lambda b, g, i, f: (0, g))],
        out_specs=pl.BlockSpec((tq, hw), lambda b, g, i, f: (b * nq + i, g)),
        scratch_shapes=[pltpu.VMEM((1, NSA_R * tq), f32), pltpu.VMEM((1, NSA_R * tq), f32),
                        pltpu.VMEM((NSA_D, NSA_R * tq), f32)],
    )
    return pl.pallas_call(
        functools.partial(_nsa_attn_kernel, tq=tq, tk=tk, nkt=nkt),
        grid_spec=grid_spec,
        out_shape=jax.ShapeDtypeStruct((t, NSA_G * hw), bf16),
        compiler_params=_cp(("parallel", "parallel", "arbitrary")),
        name="nsa_attention",
    )(flags, p, q_norm.astype(f32).reshape(1, NSA_D), p, o_c, selt, ks, vst, kw, vwt, gain.reshape(1, -1))


def _out_proj_kernel(a_ref, b_ref, c_ref, wa_ref, wb_ref, wc_ref, h_ref, o_ref):
    o_ref[...] = (h_ref[...]
                  + jnp.dot(a_ref[...], wa_ref[...], preferred_element_type=f32)
                  + jnp.dot(b_ref[...], wb_ref[...], preferred_element_type=f32)
                  + jnp.dot(c_ref[...], wc_ref[...], preferred_element_type=f32))


def _out_proj(ya, yb, yc, w_out, h):
    t, d = h.shape
    tm, tn = TM_PROJ, 1024
    w = w_out.astype(bf16)
    ka, kb, kc = ya.shape[1], yb.shape[1], yc.shape[1]
    return pl.pallas_call(
        _out_proj_kernel,
        grid=(t // tm, d // tn),
        in_specs=[pl.BlockSpec((tm, ka), lambda i, j: (i, 0)),
                  pl.BlockSpec((tm, kb), lambda i, j: (i, 0)),
                  pl.BlockSpec((tm, kc), lambda i, j: (i, 0)),
                  pl.BlockSpec((ka, tn), lambda i, j: (0, j)),
                  pl.BlockSpec((kb, tn), lambda i, j: (1, j)),
                  pl.BlockSpec((kc, tn), lambda i, j: (1, j)),
                  pl.BlockSpec((tm, tn), lambda i, j: (i, j))],
        out_specs=pl.BlockSpec((tm, tn), lambda i, j: (i, j)),
        out_shape=jax.ShapeDtypeStruct((t, d), f32),
        compiler_params=_cp(("parallel", "parallel")),
        name="out_proj",
    )(ya, yb, yc, w, w, w, h)


def _top_k_columns(v, k, ids=None, want_rank=True):
    if ids is None:
        ids = lax.broadcasted_iota(jnp.int32, v.shape, 0)
    big = jnp.int32(2 ** 30)
    kidx = lax.broadcasted_iota(jnp.int32, (k, v.shape[1]), 0)
    vals = jnp.zeros((k, v.shape[1]), f32)
    firsts = []
    rank = jnp.full(v.shape, float(k), f32) if want_rank else None
    for kk in range(k):
        m = jnp.max(v, axis=0, keepdims=True)
        first = jnp.min(jnp.where(v == m, ids, big), axis=0, keepdims=True)
        hit = ids == first
        if want_rank:
            rank = jnp.where(hit, float(kk), rank)
        v = jnp.where(hit, -jnp.inf, v)
        vals = jnp.where(kidx == kk, m, vals)
        firsts.append(first)
    return vals, rank, firsts


def _pair_candidates(t1, t2):
    n = t1.shape[1]
    k = PEER_TOPK
    i16 = lax.broadcasted_iota(jnp.int32, (k, n), 0)
    i8 = lax.broadcasted_iota(jnp.int32, (8, n), 0)
    vals = [t1[0:1, :] + t2, t1[1:2, :] + t2[0:8, :]]
    pos = [i16, k + i8]
    for a in range(2, 8):
        vals.append(jnp.where(i8 < k // (a + 1), t1[a:a + 1, :] + t2[0:8, :], -jnp.inf))
        pos.append(a * k + i8)
    vals.append(t1[8:16, :] + t2[0:1, :])
    pos.append((8 + i8) * k)
    return jnp.concatenate(vals, axis=0), jnp.concatenate(pos, axis=0)


def _peer_route_kernel(q_ref, key_ref, a_ref, cnt_ref, rank2_ref, e2_ref):
    tm = q_ref.shape[0]
    for h in range(PEER_H):
        sc, tops, ranks = [], [], []
        for half in range(2):
            c0 = (h * 2 + half) * PEER_NK
            qh = q_ref[:, c0:c0 + PEER_NK].astype(bf16)
            s = lax.dot_general(key_ref[h * 2 + half], qh, (((1,), (1,)), ((), ())),
                                preferred_element_type=f32)
            vals, rank, _ = _top_k_columns(s, PEER_TOPK)
            sc.append(s)
            tops.append(vals)
            ranks.append(rank)
        cand, cand_pos = _pair_candidates(tops[0], tops[1])
        best, _, pos = _top_k_columns(cand, PEER_TOPK, ids=cand_pos, want_rank=False)
        mass = jnp.sum(jnp.exp(best - best[0:1, :]), axis=0, keepdims=True)
        aidx = lax.broadcasted_iota(jnp.int32, (PEER_TOPK, tm), 0)
        per_rank = jnp.zeros((PEER_TOPK, tm), f32)
        for p in pos:
            per_rank = per_rank + jnp.where(aidx == p // PEER_TOPK, 1.0, 0.0)
        cnt = jnp.zeros((PEER_NK, tm), f32)
        for a in range(PEER_TOPK):
            cnt = cnt + jnp.where(ranks[0] == float(a), per_rank[a:a + 1, :], 0.0)
        hs = slice(h * PEER_NK, (h + 1) * PEER_NK)
        a_ref[hs, :] = (jnp.exp(sc[0] - tops[0][0:1, :]) * (0.5 / mass)).astype(bf16)
        cnt_ref[hs, :] = cnt.astype(bf16)
        rank2_ref[hs, :] = ranks[1].astype(bf16)
        e2_ref[hs, :] = jnp.exp(sc[1] - tops[1][0:1, :]).astype(bf16)


def _peer_route(qry, keys):
    t = qry.shape[0]
    tm = TM_ROUTE
    narrow = jax.ShapeDtypeStruct((PEER_H * PEER_NK, t), bf16)
    nspec = pl.BlockSpec((PEER_H * PEER_NK, tm), lambda i: (0, i))
    return pl.pallas_call(
        _peer_route_kernel,
        grid=(t // tm,),
        in_specs=[pl.BlockSpec((tm, PEER_H * 2 * PEER_NK), lambda i: (i, 0)),
                  pl.BlockSpec((PEER_H * 2, PEER_NK, PEER_NK), lambda i: (0, 0, 0))],
        out_specs=[nspec] * 4,
        out_shape=[narrow] * 4,
        compiler_params=_cp(("parallel",)),
        name="peer_route",
    )(qry, keys.astype(bf16).reshape(PEER_H * 2, PEER_NK, PEER_NK))


def _peer_expert_kernel(xn_ref, u_ref, vt_ref, a_ref, cnt_ref, rank2_ref, e2_ref, h_ref, o_ref,
                        acc_ref, act_ref, gt_ref, arow_ref, crow_ref, *, tm, te):
    j = pl.program_id(1)

    @pl.when(j == 0)
    def _():
        acc_ref[...] = jnp.zeros_like(acc_ref)

    n_first = te // PEER_NK
    i0 = pl.multiple_of(j * n_first, n_first)

    a_rows = [a_ref[pl.ds(pl.multiple_of(h * PEER_NK + i0, n_first), n_first), :] for h in range(PEER_H)]
    c_rows = [cnt_ref[pl.ds(pl.multiple_of(h * PEER_NK + i0, n_first), n_first), :] for h in range(PEER_H)]
    for h in range(PEER_H):
        arow_ref[h * n_first:(h + 1) * n_first, :] = a_rows[h]
        crow_ref[h * n_first:(h + 1) * n_first, :] = c_rows[h]

    x = lax.dot_general(u_ref[...], xn_ref[...], (((1,), (1,)), ((), ())), preferred_element_type=f32)
    act_ref[...] = (x * (1.0 + jnp.tanh(x * (GELU_C + (GELU_C * GELU_A) * (x * x))))).astype(bf16)

    jb = PEER_JB
    zero = jnp.zeros((jb, BF16_LANES), bf16)
    for c in range(tm // BF16_LANES):
        ls = slice(c * BF16_LANES, (c + 1) * BF16_LANES)
        for u in range(n_first):
            for q in range(PEER_NK // jb):
                rs = slice(u * PEER_NK + q * jb, u * PEER_NK + (q + 1) * jb)
                w = zero
                for h in range(PEER_H):
                    hs = slice(h * PEER_NK + q * jb, h * PEER_NK + (q + 1) * jb)
                    a = arow_ref[h * n_first + u:h * n_first + u + 1, ls]
                    cnt = crow_ref[h * n_first + u:h * n_first + u + 1, ls]
                    keep = rank2_ref[hs, ls] < cnt
                    w = w + jnp.where(keep, e2_ref[hs, ls], zero) * a
                gt_ref[rs, ls] = w * act_ref[rs, ls]

    acc_ref[...] += jnp.dot(vt_ref[...], gt_ref[...], preferred_element_type=f32)

    @pl.when(j == pl.num_programs(1) - 1)
    def _():
        o_ref[...] = h_ref[...] + acc_ref[...].T


def _peer_experts(xn, u, vt, a, cnt, rank2, e2, h):
    t, d = h.shape
    tm, te = TM_PEER, TE_PEER
    nspec = pl.BlockSpec((PEER_H * PEER_NK, tm), lambda i, j: (0, i))
    return pl.pallas_call(
        functools.partial(_peer_expert_kernel, tm=tm, te=te),
        grid=(t // tm, PEER_E // te),
        in_specs=[pl.BlockSpec((tm, d), lambda i, j: (i, 0)),
                  pl.BlockSpec((te, d), lambda i, j: (j, 0)),
                  pl.BlockSpec((d, te), lambda i, j: (0, j)),
                  nspec, nspec, nspec, nspec,
                  pl.BlockSpec((tm, d), lambda i, j: (i, 0))],
        out_specs=pl.BlockSpec((tm, d), lambda i, j: (i, 0)),
        out_shape=jax.ShapeDtypeStruct((t, d), f32),
        scratch_shapes=[pltpu.VMEM((d, tm), f32), pltpu.VMEM((te, tm), bf16), pltpu.VMEM((te, tm), bf16),
                        pltpu.VMEM((PEER_H * te // PEER_NK, tm), bf16),
                        pltpu.VMEM((PEER_H * te // PEER_NK, tm), bf16)],
        compiler_params=_cp(("parallel", "arbitrary")),
        name="peer_experts",
    )(xn, u, vt, a, cnt, rank2, e2, h)


def _transpose_cast_kernel(x_ref, o_ref):
    o_ref[...] = x_ref[...].T.astype(bf16)


def _transpose_cast(x):
    r, c = x.shape
    tr = TE_PEER
    return pl.pallas_call(
        _transpose_cast_kernel,
        grid=(r // tr,),
        in_specs=[pl.BlockSpec((tr, c), lambda i: (i, 0))],
        out_specs=pl.BlockSpec((c, tr), lambda i: (0, i)),
        out_shape=jax.ShapeDtypeStruct((c, r), bf16),
        compiler_params=_cp(("parallel",)),
        name="transpose_cast",
    )(x)


def _permute_w_in(w):
    pad = jnp.zeros((w.shape[0], N_IN - ORIG_END), w.dtype)
    return jnp.concatenate([w[:, :ORIG_GA], w[:, ORIG_NQ:ORIG_GATE], w[:, ORIG_GA:ORIG_NQ],
                            w[:, ORIG_GATE:ORIG_END], pad], axis=1).astype(bf16)


def _tile_flags(blk_any, seq):
    per_tile = blk_any.reshape(-1, seq // TK, TK // SEL_BLOCK).max(axis=-1)
    return (per_tile > 0.5).astype(jnp.int32).reshape(-1)


def _mixers(h, norm_mix, w_in, conv_w, gla_w_a2, gla_b_a, q_norm, k_norm, cmp_pos, cmp_w1, cmp_w2, out_norm,
            batch, seq):
    p = _norm_matmul(h, norm_mix.astype(f32), _permute_w_in(w_in), TN_IN)
    gain = out_norm.astype(f32)
    y_conv = _conv_mixer(p, conv_w, gain[:CONV_W], seq)
    y_gla = _gla_mixer(p, gla_w_a2, gla_b_a, gain[CONV_W:2 * CONV_W], batch, seq)
    kcc = _compress(p, OFF_NKC, cmp_pos[0], cmp_w1[0], cmp_w2[0], k_norm[0], batch, seq, True)
    vcc = _compress(p, OFF_NVC, cmp_pos[1], cmp_w1[1], cmp_w2[1], k_norm[0], batch, seq, False)
    ks, vs, kw, vw = _kv_prep(p, k_norm, batch, seq)
    o_c, selt, blk_any = _nsa_cmp(p, q_norm, kcc, vcc, batch, seq)
    y_nsa = _nsa_attn(p, q_norm, o_c, selt, _tile_flags(blk_any, seq), ks, vs, kw, vw, gain[2 * CONV_W:], batch, seq)
    return y_conv, y_gla, y_nsa


def _layer(h, norm_mix, w_in, conv_w, gla_w_a2, gla_b_a, q_norm, k_norm, cmp_pos, cmp_w1, cmp_w2, out_norm, w_out,
           norm_ffn, peer_w_q, peer_keys, peer_u, peer_v, batch, seq):
    y_conv, y_gla, y_nsa = _mixers(h, norm_mix, w_in, conv_w, gla_w_a2, gla_b_a, q_norm, k_norm, cmp_pos, cmp_w1,
                                   cmp_w2, out_norm, batch, seq)
    h = _out_proj(y_conv, y_gla, y_nsa, w_out, h)
    qry, xn = _norm_matmul(h, norm_ffn.astype(f32), peer_w_q.astype(bf16), 1024, emit_xn=True)
    a, cnt, rank2, e2 = _peer_route(qry, peer_keys)
    return _peer_experts(xn, peer_u.astype(bf16), _transpose_cast(peer_v.astype(f32)), a, cnt, rank2, e2, h)


def kernel(x, norm_mix, w_in, conv_w, gla_w_a2, gla_b_a, nsa_q_norm, nsa_k_norm, nsa_cmp_pos, nsa_cmp_w1,
           nsa_cmp_w2, out_norm, w_out, norm_ffn, peer_w_q, peer_keys, peer_u, peer_v):
    batch, seq, d = x.shape
    h = x.reshape(batch * seq, d)
    for l in range(w_in.shape[0]):
        h = _layer(h, norm_mix[l], w_in[l], conv_w[l], gla_w_a2[l], gla_b_a[l], nsa_q_norm[l], nsa_k_norm[l],
                   nsa_cmp_pos[l], nsa_cmp_w1[l], nsa_cmp_w2[l], out_norm[l], w_out[l], norm_ffn[l], peer_w_q[l],
                   peer_keys[l], peer_u[l], peer_v[l], batch, seq)
    return h.reshape(batch, seq, d)
```

```python
import functools
import math

import numpy as np
import jax
import jax.numpy as jnp
from jax import lax
from jax.experimental import pallas as pl
from jax.experimental.pallas import tpu as pltpu

f32 = jnp.float32
bf16 = jnp.bfloat16

D_MODEL = 2048
EPS = 1e-6
HEAD = 128
CONV_W = 512
CONV_K = 3
GLA_H, GLA_DK, GLA_DV = 4, 64, 128
GLA_LR = 16
GLA_TAU = 16.0
GLA_CHUNK = 64
NSA_G, NSA_R, NSA_D = 2, 4, 128
CMP_LEN, CMP_STRIDE = 32, 16
SEL_BLOCK, SEL_TOPK, SEL_LOCAL = 64, 16, 2
WINDOW = 512
BIG = 1e9
PEER_H, PEER_NK, PEER_TOPK = 8, 128, 16
PEER_E = PEER_NK * PEER_NK
NEG = -1e30
BF16_LANES = 256

OFF_CX, OFF_CB, OFF_CC = 0, 512, 1024
OFF_GQ, OFF_GK, OFF_GV, OFF_GG = 1536, 1792, 2048, 2560
OFF_NQ = 3072
OFF_NKC, OFF_NVC, OFF_NKS, OFF_NVS, OFF_NKW, OFF_NVW = 4096, 4352, 4608, 4864, 5120, 5376
OFF_MISC = 5632
N_IN = 5760
ORIG_GA, ORIG_NQ, ORIG_GATE, ORIG_END = 3072, 3088, 5648, 5672

TM_PROJ = 1024
TN_IN = 1152
TS_ELT = 512
TC_GLA = 512
GLA_UNROLL = 4
TQ = 128
TK = 512
CMP_ROWS = 256
TM_PEER = 512
TE_PEER = 1024
TM_ROUTE = 256
PEER_JB = 128
VMEM_LIMIT = 56 * 1024 * 1024
GELU_C = math.sqrt(2.0 / math.pi)
GELU_A = 0.044715


def _cp(sem):
    return pltpu.CompilerParams(dimension_semantics=sem, vmem_limit_bytes=VMEM_LIMIT)


def _gelu(x):
    return 0.5 * x * (1.0 + jnp.tanh(GELU_C * (x + GELU_A * (x * x * x))))


def _split_bf16(x):
    hi = x.astype(bf16)
    lo = (x - hi.astype(f32)).astype(bf16)
    return hi, lo


def _group_rms(y, gain):
    ms = jnp.mean(y * y, axis=-1, keepdims=True)
    return y * lax.rsqrt(ms + EPS) * gain


def _norm_matmul_kernel(x_ref, g_ref, w_ref, *refs, emit_xn):
    if emit_xn:
        o_ref, xo_ref, xn_ref = refs
    else:
        o_ref, xn_ref = refs

    @pl.when(pl.program_id(1) == 0)
    def _():
        x = x_ref[...]
        ms = jnp.mean(x * x, axis=-1, keepdims=True)
        xn = (x * lax.rsqrt(ms + EPS) * g_ref[...]).astype(bf16)
        xn_ref[...] = xn
        if emit_xn:
            xo_ref[...] = xn

    o_ref[...] = jnp.dot(xn_ref[...], w_ref[...], preferred_element_type=f32)


def _norm_matmul(x, gain, w, tn, emit_xn=False):
    t, d = x.shape
    n = w.shape[1]
    tm = TM_PROJ
    out_shape = [jax.ShapeDtypeStruct((t, n), f32)]
    out_specs = [pl.BlockSpec((tm, tn), lambda i, j: (i, j))]
    if emit_xn:
        out_shape.append(jax.ShapeDtypeStruct((t, d), bf16))
        out_specs.append(pl.BlockSpec((tm, d), lambda i, j: (i, 0)))
    outs = pl.pallas_call(
        functools.partial(_norm_matmul_kernel, emit_xn=emit_xn),
        grid=(t // tm, n // tn),
        in_specs=[
            pl.BlockSpec((tm, d), lambda i, j: (i, 0)),
            pl.BlockSpec((1, d), lambda i, j: (0, 0)),
            pl.BlockSpec((d, tn), lambda i, j: (0, j)),
        ],
        out_specs=out_specs,
        out_shape=out_shape,
        scratch_shapes=[pltpu.VMEM((tm, d), bf16)],
        compiler_params=_cp(("parallel", "arbitrary")),
        name="norm_matmul",
    )(x, gain.reshape(1, d), w)
    return outs if emit_xn else outs[0]


def _conv_kernel(cx_ref, cb_ref, cc_ref, pcx_ref, pcc_ref, w_ref, g_ref, o_ref, z_ref, *, ts, tiles_per_seq):
    first = (pl.program_id(0) % tiles_per_seq) == 0
    zp = pcx_ref[...] * pcc_ref[...]
    z_ref[0:8, :] = jnp.where(first, 0.0, zp)
    z_ref[8:8 + ts, :] = cx_ref[...] * cc_ref[...]
    y = (z_ref[8:8 + ts, :] * w_ref[0:1, :]
         + z_ref[7:7 + ts, :] * w_ref[1:2, :]
         + z_ref[6:6 + ts, :] * w_ref[2:3, :])
    y = cb_ref[...] * y
    for c in range(CONV_W // HEAD):
        sl = slice(c * HEAD, (c + 1) * HEAD)
        o_ref[:, sl] = _group_rms(y[:, sl], g_ref[:, sl]).astype(bf16)


def _conv_mixer(p, conv_w, gain, seq):
    t = p.shape[0]
    ts = TS_ELT
    w_t = conv_w.astype(f32).T
    cur = lambda off: pl.BlockSpec((ts, CONV_W), lambda i, off=off: (i, off // CONV_W))
    prev = lambda off: pl.BlockSpec((8, CONV_W), lambda i, off=off: (jnp.maximum(i * (ts // 8) - 1, 0), off // CONV_W))
    return pl.pallas_call(
        functools.partial(_conv_kernel, ts=ts, tiles_per_seq=seq // ts),
        grid=(t // ts,),
        in_specs=[cur(OFF_CX), cur(OFF_CB), cur(OFF_CC), prev(OFF_CX), prev(OFF_CC),
                  pl.BlockSpec((CONV_K, CONV_W), lambda i: (0, 0)),
                  pl.BlockSpec((1, CONV_W), lambda i: (0, 0))],
        out_specs=pl.BlockSpec((ts, CONV_W), lambda i: (i, 0)),
        out_shape=jax.ShapeDtypeStruct((t, CONV_W), bf16),
        scratch_shapes=[pltpu.VMEM((ts + 8, CONV_W), f32)],
        compiler_params=_cp(("parallel",)),
        name="conv_mixer",
    )(p, p, p, p, p, w_t, gain.reshape(1, CONV_W))


def _gla_kernel(q_ref, k_ref, v_ref, gg_ref, misc_ref, wa_ref, ba_ref, gn_ref, o_ref,
                st_ref, la_ref, y_ref, *, tc):
    @pl.when(pl.program_id(1) == 0)
    def _():
        st_ref[...] = jnp.zeros_like(st_ref)

    a_hi, a_lo = _split_bf16(misc_ref[:, 0:GLA_LR])
    w_hi, w_lo = _split_bf16(wa_ref[...])
    z = (jnp.dot(a_hi, w_hi, preferred_element_type=f32)
         + jnp.dot(a_hi, w_lo, preferred_element_type=f32)
         + jnp.dot(a_lo, w_hi, preferred_element_type=f32)) + ba_ref[...]
    la_ref[...] = (jnp.minimum(z, 0.0) - jnp.log(1.0 + jnp.exp(-jnp.abs(z)))) * (1.0 / GLA_TAU)

    c = GLA_CHUNK
    row = lax.broadcasted_iota(jnp.int32, (c, c), 0)
    col = lax.broadcasted_iota(jnp.int32, (c, c), 1)
    causal = row >= col
    tril = causal.astype(bf16)
    scale = GLA_DK ** -0.5

    def chunk(ci, carry):
        r0 = pl.multiple_of(ci * c, c)
        la = la_ref[pl.ds(r0, c), :]
        la_hi, la_lo = _split_bf16(la)
        la_lo2 = (la - la_hi.astype(f32) - la_lo.astype(f32)).astype(bf16)
        bc = (jnp.dot(tril, la_hi, preferred_element_type=f32)
              + jnp.dot(tril, la_lo, preferred_element_type=f32)
              + jnp.dot(tril, la_lo2, preferred_element_type=f32))
        bl = bc[c - 1:c, :]
        qc = q_ref[pl.ds(r0, c), :]
        kc = k_ref[pl.ds(r0, c), :]
        vc = v_ref[pl.ds(r0, c), :].astype(bf16)
        q_dec = (qc * scale * jnp.exp(bc)).astype(bf16)
        k_inv = (kc * jnp.exp(-bc)).astype(bf16)
        k_end = (kc * jnp.exp(bl - bc)).astype(bf16)
        decay = jnp.exp(bl)
        for h in range(GLA_H):
            ks = slice(h * GLA_DK, (h + 1) * GLA_DK)
            vs = slice(h * GLA_DV, (h + 1) * GLA_DV)
            att = lax.dot_general(q_dec[:, ks], k_inv[:, ks], (((1,), (1,)), ((), ())),
                                  preferred_element_type=f32)
            att = jnp.where(causal, att, 0.0).astype(bf16)
            o = jnp.dot(att, vc[:, vs], preferred_element_type=f32)
            st = st_ref[h]
            o = o + lax.dot_general(q_dec[:, ks], st.astype(bf16), (((1,), (1,)), ((), ())),
                                    preferred_element_type=f32)
            y_ref[pl.ds(r0, c), vs] = o
            upd = lax.dot_general(vc[:, vs], k_end[:, ks], (((0,), (0,)), ((), ())),
                                  preferred_element_type=f32)
            st_ref[h] = st * decay[:, ks] + upd
        return carry

    lax.fori_loop(0, tc // c, chunk, 0, unroll=GLA_UNROLL)

    g = gg_ref[...]
    sw = g * (1.0 / (1.0 + jnp.exp(-g)))
    for h in range(GLA_H):
        vs = slice(h * GLA_DV, (h + 1) * GLA_DV)
        o_ref[:, vs] = (_group_rms(y_ref[:, vs], gn_ref[:, vs]) * sw[:, vs]).astype(bf16)


def _gla_mixer(p, w_a2, b_a, gain, batch, seq):
    t = p.shape[0]
    tc = TC_GLA
    nst = seq // tc
    blk = lambda width, off: pl.BlockSpec((tc, width), lambda b, s, off=off, width=width: (b * nst + s, off // width))
    const = lambda shape: pl.BlockSpec(shape, lambda b, s: (0,) * len(shape))
    return pl.pallas_call(
        functools.partial(_gla_kernel, tc=tc),
        grid=(batch, nst),
        in_specs=[blk(256, OFF_GQ), blk(256, OFF_GK), blk(512, OFF_GV), blk(512, OFF_GG), blk(128, OFF_MISC),
                  const((GLA_LR, GLA_H * GLA_DK)), const((1, GLA_H * GLA_DK)), const((1, GLA_H * GLA_DV))],
        out_specs=pl.BlockSpec((tc, GLA_H * GLA_DV), lambda b, s: (b * nst + s, 0)),
        out_shape=jax.ShapeDtypeStruct((t, GLA_H * GLA_DV), bf16),
        scratch_shapes=[pltpu.VMEM((GLA_H, GLA_DV, GLA_DK), f32),
                        pltpu.VMEM((tc, GLA_H * GLA_DK), f32),
                        pltpu.VMEM((tc, GLA_H * GLA_DV), f32)],
        compiler_params=_cp(("parallel", "arbitrary")),
        name="gla_mixer",
    )(p, p, p, p, p, w_a2.astype(f32), b_a.astype(f32).reshape(1, -1), gain.reshape(1, -1))


def _compress_kernel(t_ref, pos_ref, w1_ref, w2_ref, gn_ref, o_ref, b_scr, *, n_piece, apply_norm):
    half = CMP_STRIDE
    acc_a = jnp.zeros((n_piece, NSA_D), f32)
    acc_b = jnp.zeros((n_piece, NSA_D), f32)
    for r in range(half):
        rows = t_ref[pl.ds(r, n_piece, stride=CMP_STRIDE), :]
        xa = (rows + pos_ref[r:r + 1, :]).astype(bf16)
        acc_a = acc_a + jnp.dot(xa, w1_ref[r], preferred_element_type=f32)
        xb = (rows + pos_ref[half + r:half + r + 1, :]).astype(bf16)
        acc_b = acc_b + jnp.dot(xb, w1_ref[half + r], preferred_element_type=f32)
    b_scr[0:n_piece, :] = acc_b
    b_scr[n_piece:n_piece + 8, :] = jnp.zeros((8, NSA_D), f32)
    hid = acc_a + b_scr[1:n_piece + 1, :]
    out = jnp.dot(_gelu(hid).astype(bf16), w2_ref[...], preferred_element_type=f32)
    if apply_norm:
        out = _group_rms(out, gn_ref[...])
    o_ref[0] = out.astype(bf16)


def _compress(p, off, pos, w1, w2, gain, batch, seq, apply_norm):
    n_piece = seq // CMP_STRIDE
    return pl.pallas_call(
        functools.partial(_compress_kernel, n_piece=n_piece, apply_norm=apply_norm),
        grid=(batch, NSA_G),
        in_specs=[pl.BlockSpec((seq, NSA_D), lambda b, g: (b, off // NSA_D + g)),
                  pl.BlockSpec((CMP_LEN, NSA_D), lambda b, g: (0, 0)),
                  pl.BlockSpec((CMP_LEN, NSA_D, NSA_D), lambda b, g: (0, 0, 0)),
                  pl.BlockSpec((NSA_D, NSA_D), lambda b, g: (0, 0)),
                  pl.BlockSpec((1, NSA_D), lambda b, g: (0, 0))],
        out_specs=pl.BlockSpec((1, n_piece, NSA_D), lambda b, g: (b * NSA_G + g, 0, 0)),
        out_shape=jax.ShapeDtypeStruct((batch * NSA_G, n_piece, NSA_D), bf16),
        scratch_shapes=[pltpu.VMEM((n_piece + 8, NSA_D), f32)],
        compiler_params=_cp(("parallel", "parallel")),
        name="nsa_compress",
    )(p, pos.astype(f32), w1.astype(bf16).reshape(CMP_LEN, NSA_D, NSA_D), w2.astype(bf16),
      gain.astype(f32).reshape(1, NSA_D))


def _kvprep_kernel(ks_ref, vs_ref, kw_ref, vw_ref, gs_ref, gw_ref, oks_ref, ovs_ref, okw_ref, ovw_ref):
    lead = pl.program_id(2) == 0
    oks_ref[0] = _group_rms(ks_ref[...], gs_ref[...]).astype(bf16)
    ovs_ref[0] = vs_ref[...].T.astype(bf16)
    okw_ref[0] = jnp.where(lead, 0.0, _group_rms(kw_ref[...], gw_ref[...])).astype(bf16)
    ovw_ref[0] = jnp.where(lead, 0.0, vw_ref[...]).T.astype(bf16)


def _kv_prep(p, k_norm, batch, seq):
    ts = WINDOW
    nst = seq // ts
    src = lambda s: jnp.maximum(s - 1, 0)
    blk = lambda off: pl.BlockSpec((ts, NSA_D), lambda b, g, s, off=off: (b * nst + src(s), off // NSA_D + g))
    gspec = pl.BlockSpec((1, NSA_D), lambda b, g, s: (0, 0))
    kspec = pl.BlockSpec((1, ts, NSA_D), lambda b, g, s: (b * NSA_G + g, src(s), 0))
    vspec = pl.BlockSpec((1, NSA_D, ts), lambda b, g, s: (b * NSA_G + g, 0, src(s)))
    kwspec = pl.BlockSpec((1, ts, NSA_D), lambda b, g, s: (b * NSA_G + g, s, 0))
    vwspec = pl.BlockSpec((1, NSA_D, ts), lambda b, g, s: (b * NSA_G + g, 0, s))
    bg = batch * NSA_G
    return pl.pallas_call(
        _kvprep_kernel,
        grid=(batch, NSA_G, nst + 1),
        in_specs=[blk(OFF_NKS), blk(OFF_NVS), blk(OFF_NKW), blk(OFF_NVW), gspec, gspec],
        out_specs=[kspec, vspec, kwspec, vwspec],
        out_shape=[jax.ShapeDtypeStruct((bg, seq, NSA_D), bf16), jax.ShapeDtypeStruct((bg, NSA_D, seq), bf16),
                   jax.ShapeDtypeStruct((bg, seq + WINDOW, NSA_D), bf16),
                   jax.ShapeDtypeStruct((bg, NSA_D, seq + WINDOW), bf16)],
        compiler_params=_cp(("parallel", "parallel", "arbitrary")),
        name="nsa_kv_prep",
    )(p, p, p, p, k_norm[1].astype(f32).reshape(1, NSA_D), k_norm[2].astype(f32).reshape(1, NSA_D))


def _stack_q(q_ref, qn_ref):
    parts = []
    for r in range(NSA_R):
        qn = _group_rms(q_ref[:, r * NSA_D:(r + 1) * NSA_D], qn_ref[...])
        parts.append((qn * (NSA_D ** -0.5)).astype(bf16))
    return jnp.concatenate(parts, axis=0)


def _head_slope(g, r):
    return jnp.where(g == 0, 1.0, 2.0 ** -NSA_R).astype(f32) * (2.0 ** -(r + 1))


def _nsa_cmp_kernel(q_ref, qn_ref, kcc_ref, vcc_ref, ovt_ref, oc_ref, selt_ref, any_ref, s_scr, p_scr, psum_scr,
                    *, tq, n_cmp, n_sel):
    g = pl.program_id(1)
    q0 = pl.program_id(2) * tq
    qs = _stack_q(q_ref, qn_ref)
    s_scr[...] = lax.dot_general(kcc_ref[0], qs, (((1,), (1,)), ((), ())), preferred_element_type=f32)
    rc = CMP_ROWS
    n_chunk = n_cmp // rc
    n_vis = jnp.clip((q0 + tq - CMP_LEN) // CMP_STRIDE + 1, 0, n_cmp)
    n_act = (n_vis + rc - 1) // rc
    tcol = lax.broadcasted_iota(jnp.int32, (rc, tq), 1)
    nrow = lax.broadcasted_iota(jnp.int32, (rc, tq), 0)

    def rel_end(c):
        return (c * rc + nrow) * CMP_STRIDE + (CMP_LEN - 1 - q0)

    cols = NSA_R * tq
    slopes = jnp.concatenate([jnp.full((1, tq), 1.0, f32) * _head_slope(g, r) for r in range(NSA_R)], axis=1)
    tcol4 = jnp.concatenate([tcol] * NSA_R, axis=1)

    def biased(c, m):
        r0 = pl.multiple_of(c * rc, rc)
        rel = jnp.concatenate([rel_end(c)] * NSA_R, axis=1)
        sb = jnp.where(rel <= tcol4, s_scr[pl.ds(r0, rc), :] + slopes * rel.astype(f32), NEG)
        s_scr[pl.ds(r0, rc), :] = sb
        return jnp.maximum(m, jnp.max(sb, axis=0, keepdims=True))

    m = lax.fori_loop(0, n_act, biased, jnp.full((1, cols), NEG, f32))
    m = jnp.maximum(m, 0.1 * NEG)

    def expo(c, l):
        r0 = pl.multiple_of(c * rc, rc)
        e = jnp.exp(s_scr[pl.ds(r0, rc), :] - m)
        s_scr[pl.ds(r0, rc), :] = e
        return l + jnp.sum(e, axis=0, keepdims=True)

    l = lax.fori_loop(0, n_act, expo, jnp.zeros((1, cols), f32))
    inv = 1.0 / jnp.maximum(l, 1e-30)

    def normalise(c, carry):
        r0 = pl.multiple_of(c * rc, rc)
        pn = s_scr[pl.ds(r0, rc), :] * inv
        p_scr[pl.ds(r0, rc), :] = pn.astype(bf16)
        psum = pn[:, 0:tq]
        for r in range(1, NSA_R):
            psum = psum + pn[:, r * tq:(r + 1) * tq]
        psum_scr[pl.ds(r0, rc), :] = psum
        return carry

    lax.fori_loop(0, n_act, normalise, 0)

    def clear(c, carry):
        r0 = pl.multiple_of(c * rc, rc)
        p_scr[pl.ds(r0, rc), :] = jnp.zeros((rc, NSA_R * tq), bf16)
        psum_scr[pl.ds(r0, rc), :] = jnp.zeros((rc, tq), f32)
        return carry

    lax.fori_loop(n_act, n_chunk, clear, 0)

    vcc = vcc_ref[0]
    for r in range(NSA_R):
        oc_ref[:, r * NSA_D:(r + 1) * NSA_D] = lax.dot_general(
            p_scr[:, r * tq:(r + 1) * tq], vcc, (((0,), (0,)), ((), ())), preferred_element_type=f32)

    p_hi, p_lo = _split_bf16(psum_scr[...])
    imp = (jnp.dot(ovt_ref[...], p_hi, preferred_element_type=f32)
           + jnp.dot(ovt_ref[...], p_lo, preferred_element_type=f32))
    blk = lax.broadcasted_iota(jnp.int32, (n_sel, tq), 0)
    tblk = (q0 + lax.broadcasted_iota(jnp.int32, (n_sel, tq), 1)) // SEL_BLOCK
    lag = tblk - blk
    score = jnp.where(blk == 0, BIG, jnp.where(lag < SEL_LOCAL, BIG, imp))
    score = jnp.where(lag >= 0, score, -1.0)
    sel = jnp.zeros((n_sel, tq), f32)
    for _ in range(SEL_TOPK):
        m = jnp.max(score, axis=0, keepdims=True)
        first = jnp.min(jnp.where(score == m, blk, n_sel), axis=0, keepdims=True)
        hit = blk == first
        sel = jnp.where(hit, 1.0, sel)
        score = jnp.where(hit, -2.0, score)
    selt_ref[0] = sel.astype(bf16)
    any_ref[0] = jnp.max(sel, axis=1, keepdims=True)


def _nsa_cmp(p, q_norm, kcc, vcc, batch, seq):
    tq = TQ
    nq = seq // tq
    n_cmp = seq // CMP_STRIDE
    n_sel = seq // SEL_BLOCK
    ci = np.arange(n_cmp)[None, :] * CMP_STRIDE
    sj = np.arange(n_sel)[:, None] * SEL_BLOCK
    ovt = np.clip(np.minimum(ci + CMP_LEN, sj + SEL_BLOCK) - np.maximum(ci, sj), 0, None) / CMP_LEN
    ovt = jnp.asarray(ovt, dtype=bf16)
    t = batch * seq
    hw = NSA_R * NSA_D
    return pl.pallas_call(
        functools.partial(_nsa_cmp_kernel, tq=tq, n_cmp=n_cmp, n_sel=n_sel),
        grid=(batch, NSA_G, nq),
        in_specs=[pl.BlockSpec((tq, hw), lambda b, g, i: (b * nq + i, OFF_NQ // hw + g)),
                  pl.BlockSpec((1, NSA_D), lambda b, g, i: (0, 0)),
                  pl.BlockSpec((1, n_cmp, NSA_D), lambda b, g, i: (b * NSA_G + g, 0, 0)),
                  pl.BlockSpec((1, n_cmp, NSA_D), lambda b, g, i: (b * NSA_G + g, 0, 0)),
                  pl.BlockSpec((n_sel, n_cmp), lambda b, g, i: (0, 0))],
        out_specs=[pl.BlockSpec((tq, hw), lambda b, g, i: (b * nq + i, g)),
                   pl.BlockSpec((1, n_sel, tq), lambda b, g, i: (b * NSA_G + g, 0, i)),
                   pl.BlockSpec((1, n_sel, 1), lambda b, g, i: ((b * NSA_G + g) * nq + i, 0, 0))],
        out_shape=[jax.ShapeDtypeStruct((t, NSA_G * hw), f32),
                   jax.ShapeDtypeStruct((batch * NSA_G, n_sel, seq), bf16),
                   jax.ShapeDtypeStruct((batch * NSA_G * nq, n_sel, 1), f32)],
        scratch_shapes=[pltpu.VMEM((n_cmp, NSA_R * tq), f32), pltpu.VMEM((n_cmp, NSA_R * tq), bf16),
                        pltpu.VMEM((n_cmp, tq), f32)],
        compiler_params=_cp(("parallel", "parallel", "parallel")),
        name="nsa_cmp_select",
    )(p, q_norm.astype(f32).reshape(1, NSA_D), kcc, vcc, ovt)


def _nsa_attn_kernel(flag_ref, q_ref, qn_ref, misc_ref, oc_ref, selt_ref, ks_ref, vst_ref, kw_ref, vwt_ref, gn_ref,
                     o_ref, m_scr, l_scr, acc_scr, *, tq, tk, nkt):
    b = pl.program_id(0)
    g = pl.program_id(1)
    qi = pl.program_id(2)
    nq = pl.num_programs(2)
    q0 = qi * tq
    cols = NSA_R * tq
    qs = _stack_q(q_ref, qn_ref)
    nt_dims = (((1,), (1,)), ((), ()))

    bpt = tk // SEL_BLOCK
    krow = lax.broadcasted_iota(jnp.int32, (tk, tq), 0)
    tcol = lax.broadcasted_iota(jnp.int32, (tk, tq), 1)
    kt_last = (q0 + tq - 1) // tk
    slopes = jnp.concatenate([jnp.full((1, tq), 1.0, f32) * _head_slope(g, r) for r in range(NSA_R)], axis=1)

    m_scr[...] = jnp.full((1, cols), NEG, f32)
    l_scr[...] = jnp.zeros((1, cols), f32)
    acc_scr[...] = jnp.zeros((NSA_D, cols), f32)

    def sel_step(kt, causal):
        k0 = pl.multiple_of(kt * tk, tk)
        kb = ks_ref[0, pl.ds(k0, tk), :]
        vtb = vst_ref[0, :, pl.ds(k0, tk)]
        st = lax.dot_general(kb, qs, nt_dims, preferred_element_type=f32)
        picked = selt_ref[0, pl.ds(pl.multiple_of(kt * bpt, bpt), bpt), :]
        off = (picked.astype(f32) - 1.0) * (-NEG)
        bias = jnp.concatenate([jnp.broadcast_to(off[u:u + 1, :], (SEL_BLOCK, tq)) for u in range(bpt)], axis=0)
        rel = krow + (k0 - q0)
        if causal:
            bias = bias + jnp.where(rel <= tcol, 0.0, NEG)
        relf = rel.astype(f32)
        sr = st + jnp.concatenate([relf] * NSA_R, axis=1) * slopes + jnp.concatenate([bias] * NSA_R, axis=1)
        m_old = m_scr[...]
        m_new = jnp.maximum(m_old, jnp.max(sr, axis=0, keepdims=True))
        alpha = jnp.exp(m_old - m_new)
        pr = jnp.exp(sr - m_new)
        l_scr[...] = alpha * l_scr[...] + jnp.sum(pr, axis=0, keepdims=True)
        acc_scr[...] = alpha * acc_scr[...] + jnp.dot(vtb, pr.astype(bf16), preferred_element_type=f32)
        m_scr[...] = m_new

    fbase = ((b * NSA_G + g) * nq + qi) * nkt

    def sel_loop(kt, carry):
        @pl.when(flag_ref[fbase + kt] > 0)
        def _():
            sel_step(kt, False)
        return carry

    lax.fori_loop(0, kt_last, sel_loop, 0)
    sel_step(kt_last, True)
    o_st = acc_scr[...] * (1.0 / jnp.maximum(l_scr[...], 1e-30))

    wk = WINDOW + tq
    w0 = pl.multiple_of(q0, tq)
    kwin = kw_ref[0, pl.ds(w0, wk), :]
    vwt = vwt_ref[0, :, pl.ds(w0, wk)]
    swt = lax.dot_general(kwin, qs, nt_dims, preferred_element_type=f32)
    jr = lax.broadcasted_iota(jnp.int32, (wk, tq), 0)
    tc = lax.broadcasted_iota(jnp.int32, (wk, tq), 1)
    wbias = jnp.where(jr > tc, jnp.where(jr <= tc + WINDOW, 0.0, NEG), NEG)
    wbias = wbias + jnp.where(jr >= WINDOW - q0, 0.0, NEG)
    jf = jr.astype(f32)
    srw = swt + jnp.concatenate([jf] * NSA_R, axis=1) * slopes + jnp.concatenate([wbias] * NSA_R, axis=1)
    mw = jnp.max(srw, axis=0, keepdims=True)
    prw = jnp.exp(srw - mw)
    denw = jnp.sum(prw, axis=0, keepdims=True)
    o_wt = jnp.dot(vwt, prw.astype(bf16), preferred_element_type=f32) * (1.0 / denw)

    for r in range(NSA_R):
        lanes = []
        for c in range(3):
            l0 = GLA_LR + r * 3 + c
            l1 = l0 + NSA_R * 3
            lanes.append(jnp.where(g == 0, misc_ref[:, l0:l0 + 1], misc_ref[:, l1:l1 + 1]))
        gc, gs, gw = [1.0 / (1.0 + jnp.exp(-x)) for x in lanes]
        hs = slice(r * NSA_D, (r + 1) * NSA_D)
        cs = slice(r * tq, (r + 1) * tq)
        y = gc * oc_ref[:, hs] + gs * o_st[:, cs].T + gw * o_wt[:, cs].T
        o_ref[:, hs] = _group_rms(y, gn_ref[:, hs]).astype(bf16)


def _nsa_attn(p, q_norm, o_c, selt, flags, ks, vst, kw, vwt, gain, batch, seq):
    tq, tk = TQ, TK
    nq = seq // tq
    nkt = seq // tk
    n_sel = seq // SEL_BLOCK
    t = batch * seq
    hw = NSA_R * NSA_D
    slab = lambda rows: pl.BlockSpec((1, rows, NSA_D), lambda b, g, i, f: (b * NSA_G + g, 0, 0),
                                     pipeline_mode=pl.Buffered(1))
    slab_t = lambda rows: pl.BlockSpec((1, NSA_D, rows), lambda b, g, i, f: (b * NSA_G + g, 0, 0),
                                       pipeline_mode=pl.Buffered(1))
    grid_spec = pltpu.PrefetchScalarGridSpec(
        num_scalar_prefetch=1,
        grid=(batch, NSA_G, nq),
        in_specs=[pl.BlockSpec((tq, hw), lambda b, g, i, f: (b * nq + i, OFF_NQ // hw + g)),
                  pl.BlockSpec((1, NSA_D), lambda b, g, i, f: (0, 0)),
                  pl.BlockSpec((tq, 128), lambda b, g, i, f: (b * nq + i, OFF_MISC // 128)),
                  pl.BlockSpec((tq, hw), lambda b, g, i, f: (b * nq + i, g)),
                  pl.BlockSpec((1, n_sel, tq), lambda b, g, i, f: (b * NSA_G + g, 0, i)),
                  slab(seq), slab_t(seq), slab(seq + WINDOW), slab_t(seq + WINDOW),
                  pl.BlockSpec((1, hw), lambda b, g, i, f: (0, g))],
        out_specs=pl.BlockSpec((tq, hw), lambda b, g, i, f: (b * nq + i, g)),
        scratch_shapes=[pltpu.VMEM((1, NSA_R * tq), f32), pltpu.VMEM((1, NSA_R * tq), f32),
                        pltpu.VMEM((NSA_D, NSA_R * tq), f32)],
    )
    return pl.pallas_call(
        functools.partial(_nsa_attn_kernel, tq=tq, tk=tk, nkt=nkt),
        grid_spec=grid_spec,
        out_shape=jax.ShapeDtypeStruct((t, NSA_G * hw), bf16),
        compiler_params=_cp(("parallel", "parallel", "arbitrary")),
        name="nsa_attention",
    )(flags, p, q_norm.astype(f32).reshape(1, NSA_D), p, o_c, selt, ks, vst, kw, vwt, gain.reshape(1, -1))


def _out_proj_kernel(a_ref, b_ref, c_ref, wa_ref, wb_ref, wc_ref, h_ref, o_ref):
    o_ref[...] = (h_ref[...]
                  + jnp.dot(a_ref[...], wa_ref[...], preferred_element_type=f32)
                  + jnp.dot(b_ref[...], wb_ref[...], preferred_element_type=f32)
                  + jnp.dot(c_ref[...], wc_ref[...], preferred_element_type=f32))


def _out_proj(ya, yb, yc, w_out, h):
    t, d = h.shape
    tm, tn = TM_PROJ, 1024
    w = w_out.astype(bf16)
    ka, kb, kc = ya.shape[1], yb.shape[1], yc.shape[1]
    return pl.pallas_call(
        _out_proj_kernel,
        grid=(t // tm, d // tn),
        in_specs=[pl.BlockSpec((tm, ka), lambda i, j: (i, 0)),
                  pl.BlockSpec((tm, kb), lambda i, j: (i, 0)),
                  pl.BlockSpec((tm, kc), lambda i, j: (i, 0)),
                  pl.BlockSpec((ka, tn), lambda i, j: (0, j)),
                  pl.BlockSpec((kb, tn), lambda i, j: (1, j)),
                  pl.BlockSpec((kc, tn), lambda i, j: (1, j)),
                  pl.BlockSpec((tm, tn), lambda i, j: (i, j))],
        out_specs=pl.BlockSpec((tm, tn), lambda i, j: (i, j)),
        out_shape=jax.ShapeDtypeStruct((t, d), f32),
        compiler_params=_cp(("parallel", "parallel")),
        name="out_proj",
    )(ya, yb, yc, w, w, w, h)


def _top_k_columns(v, k, ids=None, want_rank=True):
    if ids is None:
        ids = lax.broadcasted_iota(jnp.int32, v.shape, 0)
    big = jnp.int32(2 ** 30)
    kidx = lax.broadcasted_iota(jnp.int32, (k, v.shape[1]), 0)
    vals = jnp.zeros((k, v.shape[1]), f32)
    firsts = []
    rank = jnp.full(v.shape, float(k), f32) if want_rank else None
    for kk in range(k):
        m = jnp.max(v, axis=0, keepdims=True)
        first = jnp.min(jnp.where(v == m, ids, big), axis=0, keepdims=True)
        hit = ids == first
        if want_rank:
            rank = jnp.where(hit, float(kk), rank)
        v = jnp.where(hit, -jnp.inf, v)
        vals = jnp.where(kidx == kk, m, vals)
        firsts.append(first)
    return vals, rank, firsts


def _pair_candidates(t1, t2):
    n = t1.shape[1]
    k = PEER_TOPK
    i16 = lax.broadcasted_iota(jnp.int32, (k, n), 0)
    i8 = lax.broadcasted_iota(jnp.int32, (8, n), 0)
    vals = [t1[0:1, :] + t2, t1[1:2, :] + t2[0:8, :]]
    pos = [i16, k + i8]
    for a in range(2, 8):
        vals.append(jnp.where(i8 < k // (a + 1), t1[a:a + 1, :] + t2[0:8, :], -jnp.inf))
        pos.append(a * k + i8)
    vals.append(t1[8:16, :] + t2[0:1, :])
    pos.append((8 + i8) * k)
    return jnp.concatenate(vals, axis=0), jnp.concatenate(pos, axis=0)


def _peer_route_kernel(q_ref, key_ref, a_ref, cnt_ref, rank2_ref, e2_ref):
    tm = q_ref.shape[0]
    for h in range(PEER_H):
        sc, tops, ranks = [], [], []
        for half in range(2):
            c0 = (h * 2 + half) * PEER_NK
            qh = q_ref[:, c0:c0 + PEER_NK].astype(bf16)
            s = lax.dot_general(key_ref[h * 2 + half], qh, (((1,), (1,)), ((), ())),
                                preferred_element_type=f32)
            vals, rank, _ = _top_k_columns(s, PEER_TOPK)
            sc.append(s)
            tops.append(vals)
            ranks.append(rank)
        cand, cand_pos = _pair_candidates(tops[0], tops[1])
        best, _, pos = _top_k_columns(cand, PEER_TOPK, ids=cand_pos, want_rank=False)
        mass = jnp.sum(jnp.exp(best - best[0:1, :]), axis=0, keepdims=True)
        aidx = lax.broadcasted_iota(jnp.int32, (PEER_TOPK, tm), 0)
        per_rank = jnp.zeros((PEER_TOPK, tm), f32)
        for p in pos:
            per_rank = per_rank + jnp.where(aidx == p // PEER_TOPK, 1.0, 0.0)
        cnt = jnp.zeros((PEER_NK, tm), f32)
        for a in range(PEER_TOPK):
            cnt = cnt + jnp.where(ranks[0] == float(a), per_rank[a:a + 1, :], 0.0)
        hs = slice(h * PEER_NK, (h + 1) * PEER_NK)
        a_ref[hs, :] = (jnp.exp(sc[0] - tops[0][0:1, :]) * (0.5 / mass)).astype(bf16)
        cnt_ref[hs, :] = cnt.astype(bf16)
        rank2_ref[hs, :] = ranks[1].astype(bf16)
        e2_ref[hs, :] = jnp.exp(sc[1] - tops[1][0:1, :]).astype(bf16)


def _peer_route(qry, keys):
    t = qry.shape[0]
    tm = TM_ROUTE
    narrow = jax.ShapeDtypeStruct((PEER_H * PEER_NK, t), bf16)
    nspec = pl.BlockSpec((PEER_H * PEER_NK, tm), lambda i: (0, i))
    return pl.pallas_call(
        _peer_route_kernel,
        grid=(t // tm,),
        in_specs=[pl.BlockSpec((tm, PEER_H * 2 * PEER_NK), lambda i: (i, 0)),
                  pl.BlockSpec((PEER_H * 2, PEER_NK, PEER_NK), lambda i: (0, 0, 0))],
        out_specs=[nspec] * 4,
        out_shape=[narrow] * 4,
        compiler_params=_cp(("parallel",)),
        name="peer_route",
    )(qry, keys.astype(bf16).reshape(PEER_H * 2, PEER_NK, PEER_NK))


def _peer_expert_kernel(xn_ref, u_ref, vt_ref, a_ref, cnt_ref, rank2_ref, e2_ref, h_ref, o_ref,
                        acc_ref, act_ref, gt_ref, arow_ref, crow_ref, *, tm, te):
    j = pl.program_id(1)

    @pl.when(j == 0)
    def _():
        acc_ref[...] = jnp.zeros_like(acc_ref)

    n_first = te // PEER_NK
    i0 = pl.multiple_of(j * n_first, n_first)

    a_rows = [a_ref[pl.ds(pl.multiple_of(h * PEER_NK + i0, n_first), n_first), :] for h in range(PEER_H)]
    c_rows = [cnt_ref[pl.ds(pl.multiple_of(h * PEER_NK + i0, n_first), n_first), :] for h in range(PEER_H)]
    for h in range(PEER_H):
        arow_ref[h * n_first:(h + 1) * n_first, :] = a_rows[h]
        crow_ref[h * n_first:(h + 1) * n_first, :] = c_rows[h]

    x = lax.dot_general(u_ref[...], xn_ref[...], (((1,), (1,)), ((), ())), preferred_element_type=f32)
    act_ref[...] = (x * (1.0 + jnp.tanh(x * (GELU_C + (GELU_C * GELU_A) * (x * x))))).astype(bf16)

    jb = PEER_JB
    zero = jnp.zeros((jb, BF16_LANES), bf16)
    for c in range(tm // BF16_LANES):
        ls = slice(c * BF16_LANES, (c + 1) * BF16_LANES)
        for u in range(n_first):
            for q in range(PEER_NK // jb):
                rs = slice(u * PEER_NK + q * jb, u * PEER_NK + (q + 1) * jb)
                w = zero
                for h in range(PEER_H):
                    hs = slice(h * PEER_NK + q * jb, h * PEER_NK + (q + 1) * jb)
                    a = arow_ref[h * n_first + u:h * n_first + u + 1, ls]
                    cnt = crow_ref[h * n_first + u:h * n_first + u + 1, ls]
                    keep = rank2_ref[hs, ls] < cnt
                    w = w + jnp.where(keep, e2_ref[hs, ls], zero) * a
                gt_ref[rs, ls] = w * act_ref[rs, ls]

    acc_ref[...] += jnp.dot(vt_ref[...], gt_ref[...], preferred_element_type=f32)

    @pl.when(j == pl.num_programs(1) - 1)
    def _():
        o_ref[...] = h_ref[...] + acc_ref[...].T


def _peer_experts(xn, u, vt, a, cnt, rank2, e2, h):
    t, d = h.shape
    tm, te = TM_PEER, TE_PEER
    nspec = pl.BlockSpec((PEER_H * PEER_NK, tm), lambda i, j: (0, i))
    return pl.pallas_call(
        functools.partial(_peer_expert_kernel, tm=tm, te=te),
        grid=(t // tm, PEER_E // te),
        in_specs=[pl.BlockSpec((tm, d), lambda i, j: (i, 0)),
                  pl.BlockSpec((te, d), lambda i, j: (j, 0)),
                  pl.BlockSpec((d, te), lambda i, j: (0, j)),
                  nspec, nspec, nspec, nspec,
                  pl.BlockSpec((tm, d), lambda i, j: (i, 0))],
        out_specs=pl.BlockSpec((tm, d), lambda i, j: (i, 0)),
        out_shape=jax.ShapeDtypeStruct((t, d), f32),
        scratch_shapes=[pltpu.VMEM((d, tm), f32), pltpu.VMEM((te, tm), bf16), pltpu.VMEM((te, tm), bf16),
                        pltpu.VMEM((PEER_H * te // PEER_NK, tm), bf16),
                        pltpu.VMEM((PEER_H * te // PEER_NK, tm), bf16)],
        compiler_params=_cp(("parallel", "arbitrary")),
        name="peer_experts",
    )(xn, u, vt, a, cnt, rank2, e2, h)


def _transpose_cast_kernel(x_ref, o_ref):
    o_ref[...] = x_ref[...].T.astype(bf16)


def _transpose_cast(x):
    r, c = x.shape
    tr = TE_PEER
    return pl.pallas_call(
        _transpose_cast_kernel,
        grid=(r // tr,),
        in_specs=[pl.BlockSpec((tr, c), lambda i: (i, 0))],
        out_specs=pl.BlockSpec((c, tr), lambda i: (0, i)),
        out_shape=jax.ShapeDtypeStruct((c, r), bf16),
        compiler_params=_cp(("parallel",)),
        name="transpose_cast",
    )(x)


def _permute_w_in(w):
    pad = jnp.zeros((w.shape[0], N_IN - ORIG_END), w.dtype)
    return jnp.concatenate([w[:, :ORIG_GA], w[:, ORIG_NQ:ORIG_GATE], w[:, ORIG_GA:ORIG_NQ],
                            w[:, ORIG_GATE:ORIG_END], pad], axis=1).astype(bf16)


def _tile_flags(blk_any, seq):
    per_tile = blk_any.reshape(-1, seq // TK, TK // SEL_BLOCK).max(axis=-1)
    return (per_tile > 0.5).astype(jnp.int32).reshape(-1)


def _mixers(h, norm_mix, w_in, conv_w, gla_w_a2, gla_b_a, q_norm, k_norm, cmp_pos, cmp_w1, cmp_w2, out_norm,
            batch, seq):
    p = _norm_matmul(h, norm_mix.astype(f32), _permute_w_in(w_in), TN_IN)
    gain = out_norm.astype(f32)
    y_conv = _conv_mixer(p, conv_w, gain[:CONV_W], seq)
    y_gla = _gla_mixer(p, gla_w_a2, gla_b_a, gain[CONV_W:2 * CONV_W], batch, seq)
    kcc = _compress(p, OFF_NKC, cmp_pos[0], cmp_w1[0], cmp_w2[0], k_norm[0], batch, seq, True)
    vcc = _compress(p, OFF_NVC, cmp_pos[1], cmp_w1[1], cmp_w2[1], k_norm[0], batch, seq, False)
    ks, vs, kw, vw = _kv_prep(p, k_norm, batch, seq)
    o_c, selt, blk_any = _nsa_cmp(p, q_norm, kcc, vcc, batch, seq)
    y_nsa = _nsa_attn(p, q_norm, o_c, selt, _tile_flags(blk_any, seq), ks, vs, kw, vw, gain[2 * CONV_W:], batch, seq)
    return y_conv, y_gla, y_nsa


def _layer(h, norm_mix, w_in, conv_w, gla_w_a2, gla_b_a, q_norm, k_norm, cmp_pos, cmp_w1, cmp_w2, out_norm, w_out,
           norm_ffn, peer_w_q, peer_keys, peer_u, peer_v, batch, seq):
    y_conv, y_gla, y_nsa = _mixers(h, norm_mix, w_in, conv_w, gla_w_a2, gla_b_a, q_norm, k_norm, cmp_pos, cmp_w1,
                                   cmp_w2, out_norm, batch, seq)
    h = _out_proj(y_conv, y_gla, y_nsa, w_out, h)
    qry, xn = _norm_matmul(h, norm_ffn.astype(f32), peer_w_q.astype(bf16), 1024, emit_xn=True)
    a, cnt, rank2, e2 = _peer_route(qry, peer_keys)
    return _peer_experts(xn, peer_u.astype(bf16), _transpose_cast(peer_v.astype(f32)), a, cnt, rank2, e2, h)


def kernel(x, norm_mix, w_in, conv_w, gla_w_a2, gla_b_a, nsa_q_norm, nsa_k_norm, nsa_cmp_pos, nsa_cmp_w1,
           nsa_cmp_w2, out_norm, w_out, norm_ffn, peer_w_q, peer_keys, peer_u, peer_v):
    batch, seq, d = x.shape
    h = x.reshape(batch * seq, d)
    for l in range(w_in.shape[0]):
        h = _layer(h, norm_mix[l], w_in[l], conv_w[l], gla_w_a2[l], gla_b_a[l], nsa_q_norm[l], nsa_k_norm[l],
                   nsa_cmp_pos[l], nsa_cmp_w1[l], nsa_cmp_w2[l], out_norm[l], w_out[l], norm_ffn[l], peer_w_q[l],
                   peer_keys[l], peer_u[l], peer_v[l], batch, seq)
    return h.reshape(batch, seq, d)
```

```python
import functools
import math

import numpy as np
import jax
import jax.numpy as jnp
from jax import lax
from jax.experimental import pallas as pl
from jax.experimental.pallas import tpu as pltpu

f32 = jnp.float32
bf16 = jnp.bfloat16

D_MODEL = 2048
EPS = 1e-6
HEAD = 128
CONV_W = 512
CONV_K = 3
GLA_H, GLA_DK, GLA_DV = 4, 64, 128
GLA_LR = 16
GLA_TAU = 16.0
GLA_CHUNK = 64
NSA_G, NSA_R, NSA_D = 2, 4, 128
CMP_LEN, CMP_STRIDE = 32, 16
SEL_BLOCK, SEL_TOPK, SEL_LOCAL = 64, 16, 2
WINDOW = 512
BIG = 1e9
PEER_H, PEER_NK, PEER_TOPK = 8, 128, 16
PEER_E = PEER_NK * PEER_NK
NEG = -1e30
BF16_LANES = 256

OFF_CX, OFF_CB, OFF_CC = 0, 512, 1024
OFF_GQ, OFF_GK, OFF_GV, OFF_GG = 1536, 1792, 2048, 2560
OFF_NQ = 3072
OFF_NKC, OFF_NVC, OFF_NKS, OFF_NVS, OFF_NKW, OFF_NVW = 4096, 4352, 4608, 4864, 5120, 5376
OFF_MISC = 5632
N_IN = 5760
ORIG_GA, ORIG_NQ, ORIG_GATE, ORIG_END = 3072, 3088, 5648, 5672

TM_PROJ = 1024
TN_IN = 1152
TS_ELT = 512
TC_GLA = 512
GLA_UNROLL = 4
TQ = 128
TQ_CMP = 256
TK = 512
CMP_ROWS = 256
TM_PEER = 512
TE_PEER = 1024
TM_ROUTE = 256
PEER_JB = 128
VMEM_LIMIT = 56 * 1024 * 1024
GELU_C = math.sqrt(2.0 / math.pi)
GELU_A = 0.044715


def _cp(sem):
    return pltpu.CompilerParams(dimension_semantics=sem, vmem_limit_bytes=VMEM_LIMIT)


def _gelu(x):
    return 0.5 * x * (1.0 + jnp.tanh(GELU_C * (x + GELU_A * (x * x * x))))


def _split_bf16(x):
    hi = x.astype(bf16)
    lo = (x - hi.astype(f32)).astype(bf16)
    return hi, lo


def _group_rms(y, gain):
    ms = jnp.mean(y * y, axis=-1, keepdims=True)
    return y * lax.rsqrt(ms + EPS) * gain


def _norm_matmul_kernel(x_ref, g_ref, w_ref, *refs, emit_xn):
    if emit_xn:
        o_ref, xo_ref, xn_ref = refs
    else:
        o_ref, xn_ref = refs

    @pl.when(pl.program_id(1) == 0)
    def _():
        x = x_ref[...]
        ms = jnp.mean(x * x, axis=-1, keepdims=True)
        xn = (x * lax.rsqrt(ms + EPS) * g_ref[...]).astype(bf16)
        xn_ref[...] = xn
        if emit_xn:
            xo_ref[...] = xn

    o_ref[...] = jnp.dot(xn_ref[...], w_ref[...], preferred_element_type=f32)


def _norm_matmul(x, gain, w, tn, emit_xn=False):
    t, d = x.shape
    n = w.shape[1]
    tm = TM_PROJ
    out_shape = [jax.ShapeDtypeStruct((t, n), f32)]
    out_specs = [pl.BlockSpec((tm, tn), lambda i, j: (i, j))]
    if emit_xn:
        out_shape.append(jax.ShapeDtypeStruct((t, d), bf16))
        out_specs.append(pl.BlockSpec((tm, d), lambda i, j: (i, 0)))
    outs = pl.pallas_call(
        functools.partial(_norm_matmul_kernel, emit_xn=emit_xn),
        grid=(t // tm, n // tn),
        in_specs=[
            pl.BlockSpec((tm, d), lambda i, j: (i, 0)),
            pl.BlockSpec((1, d), lambda i, j: (0, 0)),
            pl.BlockSpec((d, tn), lambda i, j: (0, j)),
        ],
        out_specs=out_specs,
        out_shape=out_shape,
        scratch_shapes=[pltpu.VMEM((tm, d), bf16)],
        compiler_params=_cp(("parallel", "arbitrary")),
        name="norm_matmul",
    )(x, gain.reshape(1, d), w)
    return outs if emit_xn else outs[0]


def _conv_kernel(cx_ref, cb_ref, cc_ref, pcx_ref, pcc_ref, w_ref, g_ref, o_ref, z_ref, *, ts, tiles_per_seq):
    first = (pl.program_id(0) % tiles_per_seq) == 0
    zp = pcx_ref[...] * pcc_ref[...]
    z_ref[0:8, :] = jnp.where(first, 0.0, zp)
    z_ref[8:8 + ts, :] = cx_ref[...] * cc_ref[...]
    y = (z_ref[8:8 + ts, :] * w_ref[0:1, :]
         + z_ref[7:7 + ts, :] * w_ref[1:2, :]
         + z_ref[6:6 + ts, :] * w_ref[2:3, :])
    y = cb_ref[...] * y
    for c in range(CONV_W // HEAD):
        sl = slice(c * HEAD, (c + 1) * HEAD)
        o_ref[:, sl] = _group_rms(y[:, sl], g_ref[:, sl]).astype(bf16)


def _conv_mixer(p, conv_w, gain, seq):
    t = p.shape[0]
    ts = TS_ELT
    w_t = conv_w.astype(f32).T
    cur = lambda off: pl.BlockSpec((ts, CONV_W), lambda i, off=off: (i, off // CONV_W))
    prev = lambda off: pl.BlockSpec((8, CONV_W), lambda i, off=off: (jnp.maximum(i * (ts // 8) - 1, 0), off // CONV_W))
    return pl.pallas_call(
        functools.partial(_conv_kernel, ts=ts, tiles_per_seq=seq // ts),
        grid=(t // ts,),
        in_specs=[cur(OFF_CX), cur(OFF_CB), cur(OFF_CC), prev(OFF_CX), prev(OFF_CC),
                  pl.BlockSpec((CONV_K, CONV_W), lambda i: (0, 0)),
                  pl.BlockSpec((1, CONV_W), lambda i: (0, 0))],
        out_specs=pl.BlockSpec((ts, CONV_W), lambda i: (i, 0)),
        out_shape=jax.ShapeDtypeStruct((t, CONV_W), bf16),
        scratch_shapes=[pltpu.VMEM((ts + 8, CONV_W), f32)],
        compiler_params=_cp(("parallel",)),
        name="conv_mixer",
    )(p, p, p, p, p, w_t, gain.reshape(1, CONV_W))


def _gla_kernel(q_ref, k_ref, v_ref, gg_ref, misc_ref, wa_ref, ba_ref, gn_ref, o_ref,
                st_ref, la_ref, y_ref, *, tc):
    @pl.when(pl.program_id(1) == 0)
    def _():
        st_ref[...] = jnp.zeros_like(st_ref)

    a_hi, a_lo = _split_bf16(misc_ref[:, 0:GLA_LR])
    w_hi, w_lo = _split_bf16(wa_ref[...])
    z = (jnp.dot(a_hi, w_hi, preferred_element_type=f32)
         + jnp.dot(a_hi, w_lo, preferred_element_type=f32)
         + jnp.dot(a_lo, w_hi, preferred_element_type=f32)) + ba_ref[...]
    la_ref[...] = (jnp.minimum(z, 0.0) - jnp.log(1.0 + jnp.exp(-jnp.abs(z)))) * (1.0 / GLA_TAU)

    c = GLA_CHUNK
    row = lax.broadcasted_iota(jnp.int32, (c, c), 0)
    col = lax.broadcasted_iota(jnp.int32, (c, c), 1)
    causal = row >= col
    tril = causal.astype(bf16)
    scale = GLA_DK ** -0.5

    def chunk(ci, carry):
        r0 = pl.multiple_of(ci * c, c)
        la = la_ref[pl.ds(r0, c), :]
        la_hi, la_lo = _split_bf16(la)
        la_lo2 = (la - la_hi.astype(f32) - la_lo.astype(f32)).astype(bf16)
        bc = (jnp.dot(tril, la_hi, preferred_element_type=f32)
              + jnp.dot(tril, la_lo, preferred_element_type=f32)
              + jnp.dot(tril, la_lo2, preferred_element_type=f32))
        bl = bc[c - 1:c, :]
        qc = q_ref[pl.ds(r0, c), :]
        kc = k_ref[pl.ds(r0, c), :]
        vc = v_ref[pl.ds(r0, c), :].astype(bf16)
        q_dec = (qc * scale * jnp.exp(bc)).astype(bf16)
        k_inv = (kc * jnp.exp(-bc)).astype(bf16)
        k_end = (kc * jnp.exp(bl - bc)).astype(bf16)
        decay = jnp.exp(bl)
        for h in range(GLA_H):
            ks = slice(h * GLA_DK, (h + 1) * GLA_DK)
            vs = slice(h * GLA_DV, (h + 1) * GLA_DV)
            att = lax.dot_general(q_dec[:, ks], k_inv[:, ks], (((1,), (1,)), ((), ())),
                                  preferred_element_type=f32)
            att = jnp.where(causal, att, 0.0).astype(bf16)
            o = jnp.dot(att, vc[:, vs], preferred_element_type=f32)
            st = st_ref[h]
            o = o + lax.dot_general(q_dec[:, ks], st.astype(bf16), (((1,), (1,)), ((), ())),
                                    preferred_element_type=f32)
            y_ref[pl.ds(r0, c), vs] = o
            upd = lax.dot_general(vc[:, vs], k_end[:, ks], (((0,), (0,)), ((), ())),
                                  preferred_element_type=f32)
            st_ref[h] = st * decay[:, ks] + upd
        return carry

    lax.fori_loop(0, tc // c, chunk, 0, unroll=GLA_UNROLL)

    g = gg_ref[...]
    sw = g * (1.0 / (1.0 + jnp.exp(-g)))
    for h in range(GLA_H):
        vs = slice(h * GLA_DV, (h + 1) * GLA_DV)
        o_ref[:, vs] = (_group_rms(y_ref[:, vs], gn_ref[:, vs]) * sw[:, vs]).astype(bf16)


def _gla_mixer(p, w_a2, b_a, gain, batch, seq):
    t = p.shape[0]
    tc = TC_GLA
    nst = seq // tc
    blk = lambda width, off: pl.BlockSpec((tc, width), lambda b, s, off=off, width=width: (b * nst + s, off // width))
    const = lambda shape: pl.BlockSpec(shape, lambda b, s: (0,) * len(shape))
    return pl.pallas_call(
        functools.partial(_gla_kernel, tc=tc),
        grid=(batch, nst),
        in_specs=[blk(256, OFF_GQ), blk(256, OFF_GK), blk(512, OFF_GV), blk(512, OFF_GG), blk(128, OFF_MISC),
                  const((GLA_LR, GLA_H * GLA_DK)), const((1, GLA_H * GLA_DK)), const((1, GLA_H * GLA_DV))],
        out_specs=pl.BlockSpec((tc, GLA_H * GLA_DV), lambda b, s: (b * nst + s, 0)),
        out_shape=jax.ShapeDtypeStruct((t, GLA_H * GLA_DV), bf16),
        scratch_shapes=[pltpu.VMEM((GLA_H, GLA_DV, GLA_DK), f32),
                        pltpu.VMEM((tc, GLA_H * GLA_DK), f32),
                        pltpu.VMEM((tc, GLA_H * GLA_DV), f32)],
        compiler_params=_cp(("parallel", "arbitrary")),
        name="gla_mixer",
    )(p, p, p, p, p, w_a2.astype(f32), b_a.astype(f32).reshape(1, -1), gain.reshape(1, -1))


def _compress_kernel(t_ref, pos_ref, w1_ref, w2_ref, gn_ref, o_ref, b_scr, *, n_piece, apply_norm):
    half = CMP_STRIDE
    acc_a = jnp.zeros((n_piece, NSA_D), f32)
    acc_b = jnp.zeros((n_piece, NSA_D), f32)
    for r in range(half):
        rows = t_ref[pl.ds(r, n_piece, stride=CMP_STRIDE), :]
        xa = (rows + pos_ref[r:r + 1, :]).astype(bf16)
        acc_a = acc_a + jnp.dot(xa, w1_ref[r], preferred_element_type=f32)
        xb = (rows + pos_ref[half + r:half + r + 1, :]).astype(bf16)
        acc_b = acc_b + jnp.dot(xb, w1_ref[half + r], preferred_element_type=f32)
    b_scr[0:n_piece, :] = acc_b
    b_scr[n_piece:n_piece + 8, :] = jnp.zeros((8, NSA_D), f32)
    hid = acc_a + b_scr[1:n_piece + 1, :]
    out = jnp.dot(_gelu(hid).astype(bf16), w2_ref[...], preferred_element_type=f32)
    if apply_norm:
        out = _group_rms(out, gn_ref[...])
    o_ref[0] = out.astype(bf16)


def _compress(p, off, pos, w1, w2, gain, batch, seq, apply_norm):
    n_piece = seq // CMP_STRIDE
    return pl.pallas_call(
        functools.partial(_compress_kernel, n_piece=n_piece, apply_norm=apply_norm),
        grid=(batch, NSA_G),
        in_specs=[pl.BlockSpec((seq, NSA_D), lambda b, g: (b, off // NSA_D + g)),
                  pl.BlockSpec((CMP_LEN, NSA_D), lambda b, g: (0, 0)),
                  pl.BlockSpec((CMP_LEN, NSA_D, NSA_D), lambda b, g: (0, 0, 0)),
                  pl.BlockSpec((NSA_D, NSA_D), lambda b, g: (0, 0)),
                  pl.BlockSpec((1, NSA_D), lambda b, g: (0, 0))],
        out_specs=pl.BlockSpec((1, n_piece, NSA_D), lambda b, g: (b * NSA_G + g, 0, 0)),
        out_shape=jax.ShapeDtypeStruct((batch * NSA_G, n_piece, NSA_D), bf16),
        scratch_shapes=[pltpu.VMEM((n_piece + 8, NSA_D), f32)],
        compiler_params=_cp(("parallel", "parallel")),
        name="nsa_compress",
    )(p, pos.astype(f32), w1.astype(bf16).reshape(CMP_LEN, NSA_D, NSA_D), w2.astype(bf16),
      gain.astype(f32).reshape(1, NSA_D))


def _kvprep_kernel(ks_ref, vs_ref, kw_ref, vw_ref, gs_ref, gw_ref, oks_ref, ovs_ref, okw_ref, ovw_ref):
    lead = pl.program_id(2) == 0
    oks_ref[0] = _group_rms(ks_ref[...], gs_ref[...]).astype(bf16)
    ovs_ref[0] = vs_ref[...].T.astype(bf16)
    okw_ref[0] = jnp.where(lead, 0.0, _group_rms(kw_ref[...], gw_ref[...])).astype(bf16)
    ovw_ref[0] = jnp.where(lead, 0.0, vw_ref[...]).T.astype(bf16)


def _kv_prep(p, k_norm, batch, seq):
    ts = WINDOW
    nst = seq // ts
    src = lambda s: jnp.maximum(s - 1, 0)
    blk = lambda off: pl.BlockSpec((ts, NSA_D), lambda b, g, s, off=off: (b * nst + src(s), off // NSA_D + g))
    gspec = pl.BlockSpec((1, NSA_D), lambda b, g, s: (0, 0))
    kspec = pl.BlockSpec((1, ts, NSA_D), lambda b, g, s: (b * NSA_G + g, src(s), 0))
    vspec = pl.BlockSpec((1, NSA_D, ts), lambda b, g, s: (b * NSA_G + g, 0, src(s)))
    kwspec = pl.BlockSpec((1, ts, NSA_D), lambda b, g, s: (b * NSA_G + g, s, 0))
    vwspec = pl.BlockSpec((1, NSA_D, ts), lambda b, g, s: (b * NSA_G + g, 0, s))
    bg = batch * NSA_G
    return pl.pallas_call(
        _kvprep_kernel,
        grid=(batch, NSA_G, nst + 1),
        in_specs=[blk(OFF_NKS), blk(OFF_NVS), blk(OFF_NKW), blk(OFF_NVW), gspec, gspec],
        out_specs=[kspec, vspec, kwspec, vwspec],
        out_shape=[jax.ShapeDtypeStruct((bg, seq, NSA_D), bf16), jax.ShapeDtypeStruct((bg, NSA_D, seq), bf16),
                   jax.ShapeDtypeStruct((bg, seq + WINDOW, NSA_D), bf16),
                   jax.ShapeDtypeStruct((bg, NSA_D, seq + WINDOW), bf16)],
        compiler_params=_cp(("parallel", "parallel", "arbitrary")),
        name="nsa_kv_prep",
    )(p, p, p, p, k_norm[1].astype(f32).reshape(1, NSA_D), k_norm[2].astype(f32).reshape(1, NSA_D))


def _stack_q(q_ref, qn_ref):
    parts = []
    for r in range(NSA_R):
        qn = _group_rms(q_ref[:, r * NSA_D:(r + 1) * NSA_D], qn_ref[...])
        parts.append((qn * (NSA_D ** -0.5)).astype(bf16))
    return jnp.concatenate(parts, axis=0)


def _head_slope(g, r):
    return jnp.where(g == 0, 1.0, 2.0 ** -NSA_R).astype(f32) * (2.0 ** -(r + 1))


def _nsa_cmp_kernel(q_ref, qn_ref, kcc_ref, vcc_ref, ovt_ref, oc_ref, selt_ref, any_ref, s_scr, p_scr, psum_scr,
                    *, tq, n_cmp, n_sel):
    g = pl.program_id(1)
    q0 = pl.program_id(2) * tq
    qs = _stack_q(q_ref, qn_ref)
    s_scr[...] = lax.dot_general(kcc_ref[0], qs, (((1,), (1,)), ((), ())), preferred_element_type=f32)
    rc = CMP_ROWS
    n_chunk = n_cmp // rc
    n_vis = jnp.clip((q0 + tq - CMP_LEN) // CMP_STRIDE + 1, 0, n_cmp)
    n_act = (n_vis + rc - 1) // rc
    tcol = lax.broadcasted_iota(jnp.int32, (rc, tq), 1)
    nrow = lax.broadcasted_iota(jnp.int32, (rc, tq), 0)

    def rel_end(c):
        return (c * rc + nrow) * CMP_STRIDE + (CMP_LEN - 1 - q0)

    cols = NSA_R * tq
    slopes = jnp.concatenate([jnp.full((1, tq), 1.0, f32) * _head_slope(g, r) for r in range(NSA_R)], axis=1)
    tcol4 = jnp.concatenate([tcol] * NSA_R, axis=1)

    def biased(c, m):
        r0 = pl.multiple_of(c * rc, rc)
        rel = jnp.concatenate([rel_end(c)] * NSA_R, axis=1)
        sb = jnp.where(rel <= tcol4, s_scr[pl.ds(r0, rc), :] + slopes * rel.astype(f32), NEG)
        s_scr[pl.ds(r0, rc), :] = sb
        return jnp.maximum(m, jnp.max(sb, axis=0, keepdims=True))

    m = lax.fori_loop(0, n_act, biased, jnp.full((1, cols), NEG, f32))
    m = jnp.maximum(m, 0.1 * NEG)

    def expo(c, l):
        r0 = pl.multiple_of(c * rc, rc)
        e = jnp.exp(s_scr[pl.ds(r0, rc), :] - m)
        s_scr[pl.ds(r0, rc), :] = e
        return l + jnp.sum(e, axis=0, keepdims=True)

    l = lax.fori_loop(0, n_act, expo, jnp.zeros((1, cols), f32))
    inv = 1.0 / jnp.maximum(l, 1e-30)

    def normalise(c, carry):
        r0 = pl.multiple_of(c * rc, rc)
        pn = s_scr[pl.ds(r0, rc), :] * inv
        p_scr[pl.ds(r0, rc), :] = pn.astype(bf16)
        psum = pn[:, 0:tq]
        for r in range(1, NSA_R):
            psum = psum + pn[:, r * tq:(r + 1) * tq]
        psum_scr[pl.ds(r0, rc), :] = psum
        return carry

    lax.fori_loop(0, n_act, normalise, 0)

    def clear(c, carry):
        r0 = pl.multiple_of(c * rc, rc)
        p_scr[pl.ds(r0, rc), :] = jnp.zeros((rc, NSA_R * tq), bf16)
        psum_scr[pl.ds(r0, rc), :] = jnp.zeros((rc, tq), f32)
        return carry

    lax.fori_loop(n_act, n_chunk, clear, 0)

    vcc = vcc_ref[0]
    for r in range(NSA_R):
        oc_ref[:, r * NSA_D:(r + 1) * NSA_D] = lax.dot_general(
            p_scr[:, r * tq:(r + 1) * tq], vcc, (((0,), (0,)), ((), ())), preferred_element_type=f32)

    p_hi, p_lo = _split_bf16(psum_scr[...])
    imp = (jnp.dot(ovt_ref[...], p_hi, preferred_element_type=f32)
           + jnp.dot(ovt_ref[...], p_lo, preferred_element_type=f32))
    blk = lax.broadcasted_iota(jnp.int32, (n_sel, tq), 0)
    tblk = (q0 + lax.broadcasted_iota(jnp.int32, (n_sel, tq), 1)) // SEL_BLOCK
    lag = tblk - blk
    score = jnp.where(blk == 0, BIG, jnp.where(lag < SEL_LOCAL, BIG, imp))
    score = jnp.where(lag >= 0, score, -1.0)
    sel = jnp.zeros((n_sel, tq), f32)
    for _ in range(SEL_TOPK):
        m = jnp.max(score, axis=0, keepdims=True)
        first = jnp.min(jnp.where(score == m, blk, n_sel), axis=0, keepdims=True)
        hit = blk == first
        sel = jnp.where(hit, 1.0, sel)
        score = jnp.where(hit, -2.0, score)
    selt_ref[0] = sel.astype(bf16)
    for a in range(tq // TQ):
        any_ref[a] = jnp.max(sel[:, a * TQ:(a + 1) * TQ], axis=1, keepdims=True)


def _nsa_cmp(p, q_norm, kcc, vcc, batch, seq):
    tq = TQ_CMP
    nq = seq // tq
    n_cmp = seq // CMP_STRIDE
    n_sel = seq // SEL_BLOCK
    ci = np.arange(n_cmp)[None, :] * CMP_STRIDE
    sj = np.arange(n_sel)[:, None] * SEL_BLOCK
    ovt = np.clip(np.minimum(ci + CMP_LEN, sj + SEL_BLOCK) - np.maximum(ci, sj), 0, None) / CMP_LEN
    ovt = jnp.asarray(ovt, dtype=bf16)
    t = batch * seq
    hw = NSA_R * NSA_D
    return pl.pallas_call(
        functools.partial(_nsa_cmp_kernel, tq=tq, n_cmp=n_cmp, n_sel=n_sel),
        grid=(batch, NSA_G, nq),
        in_specs=[pl.BlockSpec((tq, hw), lambda b, g, i: (b * nq + i, OFF_NQ // hw + g)),
                  pl.BlockSpec((1, NSA_D), lambda b, g, i: (0, 0)),
                  pl.BlockSpec((1, n_cmp, NSA_D), lambda b, g, i: (b * NSA_G + g, 0, 0)),
                  pl.BlockSpec((1, n_cmp, NSA_D), lambda b, g, i: (b * NSA_G + g, 0, 0)),
                  pl.BlockSpec((n_sel, n_cmp), lambda b, g, i: (0, 0))],
        out_specs=[pl.BlockSpec((tq, hw), lambda b, g, i: (b * nq + i, g)),
                   pl.BlockSpec((1, n_sel, tq), lambda b, g, i: (b * NSA_G + g, 0, i)),
                   pl.BlockSpec((tq // TQ, n_sel, 1), lambda b, g, i: ((b * NSA_G + g) * nq + i, 0, 0))],
        out_shape=[jax.ShapeDtypeStruct((t, NSA_G * hw), f32),
                   jax.ShapeDtypeStruct((batch * NSA_G, n_sel, seq), bf16),
                   jax.ShapeDtypeStruct((batch * NSA_G * (seq // TQ), n_sel, 1), f32)],
        scratch_shapes=[pltpu.VMEM((n_cmp, NSA_R * tq), f32), pltpu.VMEM((n_cmp, NSA_R * tq), bf16),
                        pltpu.VMEM((n_cmp, tq), f32)],
        compiler_params=_cp(("parallel", "parallel", "parallel")),
        name="nsa_cmp_select",
    )(p, q_norm.astype(f32).reshape(1, NSA_D), kcc, vcc, ovt)


def _nsa_attn_kernel(flag_ref, q_ref, qn_ref, misc_ref, oc_ref, selt_ref, ks_ref, vst_ref, kw_ref, vwt_ref, gn_ref,
                     o_ref, m_scr, l_scr, acc_scr, *, tq, tk, nkt):
    b = pl.program_id(0)
    g = pl.program_id(1)
    qi = pl.program_id(2)
    nq = pl.num_programs(2)
    q0 = qi * tq
    cols = NSA_R * tq
    qs = _stack_q(q_ref, qn_ref)
    nt_dims = (((1,), (1,)), ((), ()))

    bpt = tk // SEL_BLOCK
    krow = lax.broadcasted_iota(jnp.int32, (tk, tq), 0)
    tcol = lax.broadcasted_iota(jnp.int32, (tk, tq), 1)
    kt_last = (q0 + tq - 1) // tk
    slopes = jnp.concatenate([jnp.full((1, tq), 1.0, f32) * _head_slope(g, r) for r in range(NSA_R)], axis=1)

    m_scr[...] = jnp.full((1, cols), NEG, f32)
    l_scr[...] = jnp.zeros((1, cols), f32)
    acc_scr[...] = jnp.zeros((NSA_D, cols), f32)

    def sel_step(kt, causal):
        k0 = pl.multiple_of(kt * tk, tk)
        kb = ks_ref[0, pl.ds(k0, tk), :]
        vtb = vst_ref[0, :, pl.ds(k0, tk)]
        st = lax.dot_general(kb, qs, nt_dims, preferred_element_type=f32)
        picked = selt_ref[0, pl.ds(pl.multiple_of(kt * bpt, bpt), bpt), :]
        off = (picked.astype(f32) - 1.0) * (-NEG)
        bias = jnp.concatenate([jnp.broadcast_to(off[u:u + 1, :], (SEL_BLOCK, tq)) for u in range(bpt)], axis=0)
        rel = krow + (k0 - q0)
        if causal:
            bias = bias + jnp.where(rel <= tcol, 0.0, NEG)
        relf = rel.astype(f32)
        sr = st + jnp.concatenate([relf] * NSA_R, axis=1) * slopes + jnp.concatenate([bias] * NSA_R, axis=1)
        m_old = m_scr[...]
        m_new = jnp.maximum(m_old, jnp.max(sr, axis=0, keepdims=True))
        alpha = jnp.exp(m_old - m_new)
        pr = jnp.exp(sr - m_new)
        l_scr[...] = alpha * l_scr[...] + jnp.sum(pr, axis=0, keepdims=True)
        acc_scr[...] = alpha * acc_scr[...] + jnp.dot(vtb, pr.astype(bf16), preferred_element_type=f32)
        m_scr[...] = m_new

    fbase = ((b * NSA_G + g) * nq + qi) * nkt

    def sel_loop(kt, carry):
        @pl.when(flag_ref[fbase + kt] > 0)
        def _():
            sel_step(kt, False)
        return carry

    lax.fori_loop(0, kt_last, sel_loop, 0)
    sel_step(kt_last, True)
    o_st = acc_scr[...] * (1.0 / jnp.maximum(l_scr[...], 1e-30))

    wk = WINDOW + tq
    w0 = pl.multiple_of(q0, tq)
    kwin = kw_ref[0, pl.ds(w0, wk), :]
    vwt = vwt_ref[0, :, pl.ds(w0, wk)]
    swt = lax.dot_general(kwin, qs, nt_dims, preferred_element_type=f32)
    jr = lax.broadcasted_iota(jnp.int32, (wk, tq), 0)
    tc = lax.broadcasted_iota(jnp.int32, (wk, tq), 1)
    wbias = jnp.where(jr > tc, jnp.where(jr <= tc + WINDOW, 0.0, NEG), NEG)
    wbias = wbias + jnp.where(jr >= WINDOW - q0, 0.0, NEG)
    jf = jr.astype(f32)
    srw = swt + jnp.concatenate([jf] * NSA_R, axis=1) * slopes + jnp.concatenate([wbias] * NSA_R, axis=1)
    mw = jnp.max(srw, axis=0, keepdims=True)
    prw = jnp.exp(srw - mw)
    denw = jnp.sum(prw, axis=0, keepdims=True)
    o_wt = jnp.dot(vwt, prw.astype(bf16), preferred_element_type=f32) * (1.0 / denw)

    for r in range(NSA_R):
        lanes = []
        for c in range(3):
            l0 = GLA_LR + r * 3 + c
            l1 = l0 + NSA_R * 3
            lanes.append(jnp.where(g == 0, misc_ref[:, l0:l0 + 1], misc_ref[:, l1:l1 + 1]))
        gc, gs, gw = [1.0 / (1.0 + jnp.exp(-x)) for x in lanes]
        hs = slice(r * NSA_D, (r + 1) * NSA_D)
        cs = slice(r * tq, (r + 1) * tq)
        y = gc * oc_ref[:, hs] + gs * o_st[:, cs].T + gw * o_wt[:, cs].T
        o_ref[:, hs] = _group_rms(y, gn_ref[:, hs]).astype(bf16)


def _nsa_attn(p, q_norm, o_c, selt, flags, ks, vst, kw, vwt, gain, batch, seq):
    tq, tk = TQ, TK
    nq = seq // tq
    nkt = seq // tk
    n_sel = seq // SEL_BLOCK
    t = batch * seq
    hw = NSA_R * NSA_D
    slab = lambda rows: pl.BlockSpec((1, rows, NSA_D), lambda b, g, i, f: (b * NSA_G + g, 0, 0),
                                     pipeline_mode=pl.Buffered(1))
    slab_t = lambda rows: pl.BlockSpec((1, NSA_D, rows), lambda b, g, i, f: (b * NSA_G + g, 0, 0),
                                       pipeline_mode=pl.Buffered(1))
    grid_spec = pltpu.PrefetchScalarGridSpec(
        num_scalar_prefetch=1,
        grid=(batch, NSA_G, nq),
        in_specs=[pl.BlockSpec((tq, hw), lambda b, g, i, f: (b * nq + i, OFF_NQ // hw + g)),
                  pl.BlockSpec((1, NSA_D), lambda b, g, i, f: (0, 0)),
                  pl.BlockSpec((tq, 128), lambda b, g, i, f: (b * nq + i, OFF_MISC // 128)),
                  pl.BlockSpec((tq, hw), lambda b, g, i, f: (b * nq + i, g)),
                  pl.BlockSpec((1, n_sel, tq), lambda b, g, i, f: (b * NSA_G + g, 0, i)),
                  slab(seq), slab_t(seq), slab(seq + WINDOW), slab_t(seq + WINDOW),
                  pl.BlockSpec((1, hw), lambda b, g, i, f: (0, g))],
        out_specs=pl.BlockSpec((tq, hw), lambda b, g, i, f: (b * nq + i, g)),
        scratch_shapes=[pltpu.VMEM((1, NSA_R * tq), f32), pltpu.VMEM((1, NSA_R * tq), f32),
                        pltpu.VMEM((NSA_D, NSA_R * tq), f32)],
    )
    return pl.pallas_call(
        functools.partial(_nsa_attn_kernel, tq=tq, tk=tk, nkt=nkt),
        grid_spec=grid_spec,
        out_shape=jax.ShapeDtypeStruct((t, NSA_G * hw), bf16),
        compiler_params=_cp(("parallel", "parallel", "arbitrary")),
        name="nsa_attention",
    )(flags, p, q_norm.astype(f32).reshape(1, NSA_D), p, o_c, selt, ks, vst, kw, vwt, gain.reshape(1, -1))


def _out_proj_kernel(a_ref, b_ref, c_ref, wa_ref, wb_ref, wc_ref, h_ref, o_ref):
    o_ref[...] = (h_ref[...]
                  + jnp.dot(a_ref[...], wa_ref[...], preferred_element_type=f32)
                  + jnp.dot(b_ref[...], wb_ref[...], preferred_element_type=f32)
                  + jnp.dot(c_ref[...], wc_ref[...], preferred_element_type=f32))


def _out_proj(ya, yb, yc, w_out, h):
    t, d = h.shape
    tm, tn = TM_PROJ, 1024
    w = w_out.astype(bf16)
    ka, kb, kc = ya.shape[1], yb.shape[1], yc.shape[1]
    return pl.pallas_call(
        _out_proj_kernel,
        grid=(t // tm, d // tn),
        in_specs=[pl.BlockSpec((tm, ka), lambda i, j: (i, 0)),
                  pl.BlockSpec((tm, kb), lambda i, j: (i, 0)),
                  pl.BlockSpec((tm, kc), lambda i, j: (i, 0)),
                  pl.BlockSpec((ka, tn), lambda i, j: (0, j)),
                  pl.BlockSpec((kb, tn), lambda i, j: (1, j)),
                  pl.BlockSpec((kc, tn), lambda i, j: (1, j)),
                  pl.BlockSpec((tm, tn), lambda i, j: (i, j))],
        out_specs=pl.BlockSpec((tm, tn), lambda i, j: (i, j)),
        out_shape=jax.ShapeDtypeStruct((t, d), f32),
        compiler_params=_cp(("parallel", "parallel")),
        name="out_proj",
    )(ya, yb, yc, w, w, w, h)


def _top_k_columns(v, k, ids=None, want_rank=True):
    if ids is None:
        ids = lax.broadcasted_iota(jnp.int32, v.shape, 0)
    big = jnp.int32(2 ** 30)
    kidx = lax.broadcasted_iota(jnp.int32, (k, v.shape[1]), 0)
    vals = jnp.zeros((k, v.shape[1]), f32)
    firsts = []
    rank = jnp.full(v.shape, float(k), f32) if want_rank else None
    for kk in range(k):
        m = jnp.max(v, axis=0, keepdims=True)
        first = jnp.min(jnp.where(v == m, ids, big), axis=0, keepdims=True)
        hit = ids == first
        if want_rank:
            rank = jnp.where(hit, float(kk), rank)
        v = jnp.where(hit, -jnp.inf, v)
        vals = jnp.where(kidx == kk, m, vals)
        firsts.append(first)
    return vals, rank, firsts


def _pair_candidates(t1, t2):
    n = t1.shape[1]
    k = PEER_TOPK
    i16 = lax.broadcasted_iota(jnp.int32, (k, n), 0)
    i8 = lax.broadcasted_iota(jnp.int32, (8, n), 0)
    vals = [t1[0:1, :] + t2, t1[1:2, :] + t2[0:8, :]]
    pos = [i16, k + i8]
    for a in range(2, 8):
        vals.append(jnp.where(i8 < k // (a + 1), t1[a:a + 1, :] + t2[0:8, :], -jnp.inf))
        pos.append(a * k + i8)
    vals.append(t1[8:16, :] + t2[0:1, :])
    pos.append((8 + i8) * k)
    return jnp.concatenate(vals, axis=0), jnp.concatenate(pos, axis=0)


def _peer_route_kernel(q_ref, key_ref, a_ref, cnt_ref, rank2_ref, e2_ref):
    tm = q_ref.shape[0]
    for h in range(PEER_H):
        sc, tops, ranks = [], [], []
        for half in range(2):
            c0 = (h * 2 + half) * PEER_NK
            qh = q_ref[:, c0:c0 + PEER_NK].astype(bf16)
            s = lax.dot_general(key_ref[h * 2 + half], qh, (((1,), (1,)), ((), ())),
                                preferred_element_type=f32)
            vals, rank, _ = _top_k_columns(s, PEER_TOPK)
            sc.append(s)
            tops.append(vals)
            ranks.append(rank)
        cand, cand_pos = _pair_candidates(tops[0], tops[1])
        best, _, pos = _top_k_columns(cand, PEER_TOPK, ids=cand_pos, want_rank=False)
        mass = jnp.sum(jnp.exp(best - best[0:1, :]), axis=0, keepdims=True)
        aidx = lax.broadcasted_iota(jnp.int32, (PEER_TOPK, tm), 0)
        per_rank = jnp.zeros((PEER_TOPK, tm), f32)
        for p in pos:
            per_rank = per_rank + jnp.where(aidx == p // PEER_TOPK, 1.0, 0.0)
        cnt = jnp.zeros((PEER_NK, tm), f32)
        for a in range(PEER_TOPK):
            cnt = jnp.where(ranks[0] == float(a), per_rank[a:a + 1, :], cnt)
        hs = slice(h * PEER_NK, (h + 1) * PEER_NK)
        a_ref[hs, :] = (jnp.exp(sc[0] - tops[0][0:1, :]) * (0.5 / mass)).astype(bf16)
        cnt_ref[hs, :] = cnt.astype(bf16)
        rank2_ref[hs, :] = ranks[1].astype(bf16)
        e2_ref[hs, :] = jnp.exp(sc[1] - tops[1][0:1, :]).astype(bf16)


def _peer_route(qry, keys):
    t = qry.shape[0]
    tm = TM_ROUTE
    narrow = jax.ShapeDtypeStruct((PEER_H * PEER_NK, t), bf16)
    nspec = pl.BlockSpec((PEER_H * PEER_NK, tm), lambda i: (0, i))
    return pl.pallas_call(
        _peer_route_kernel,
        grid=(t // tm,),
        in_specs=[pl.BlockSpec((tm, PEER_H * 2 * PEER_NK), lambda i: (i, 0)),
                  pl.BlockSpec((PEER_H * 2, PEER_NK, PEER_NK), lambda i: (0, 0, 0))],
        out_specs=[nspec] * 4,
        out_shape=[narrow] * 4,
        compiler_params=_cp(("parallel",)),
        name="peer_route",
    )(qry, keys.astype(bf16).reshape(PEER_H * 2, PEER_NK, PEER_NK))


def _peer_expert_kernel(xn_ref, u_ref, vt_ref, a_ref, cnt_ref, rank2_ref, e2_ref, h_ref, o_ref,
                        acc_ref, act_ref, gt_ref, arow_ref, crow_ref, *, tm, te):
    j = pl.program_id(1)

    @pl.when(j == 0)
    def _():
        acc_ref[...] = jnp.zeros_like(acc_ref)

    n_first = te // PEER_NK
    i0 = pl.multiple_of(j * n_first, n_first)

    a_rows = [a_ref[pl.ds(pl.multiple_of(h * PEER_NK + i0, n_first), n_first), :] for h in range(PEER_H)]
    c_rows = [cnt_ref[pl.ds(pl.multiple_of(h * PEER_NK + i0, n_first), n_first), :] for h in range(PEER_H)]
    for h in range(PEER_H):
        arow_ref[h * n_first:(h + 1) * n_first, :] = a_rows[h]
        crow_ref[h * n_first:(h + 1) * n_first, :] = c_rows[h]

    x = lax.dot_general(u_ref[...], xn_ref[...], (((1,), (1,)), ((), ())), preferred_element_type=f32)
    act_ref[...] = (x * (1.0 + jnp.tanh(x * (GELU_C + (GELU_C * GELU_A) * (x * x))))).astype(bf16)

    jb = PEER_JB
    zero = jnp.zeros((jb, BF16_LANES), bf16)
    for c in range(tm // BF16_LANES):
        ls = slice(c * BF16_LANES, (c + 1) * BF16_LANES)
        for u in range(n_first):
            for q in range(PEER_NK // jb):
                rs = slice(u * PEER_NK + q * jb, u * PEER_NK + (q + 1) * jb)
                w = zero
                for h in range(PEER_H):
                    hs = slice(h * PEER_NK + q * jb, h * PEER_NK + (q + 1) * jb)
                    a = arow_ref[h * n_first + u:h * n_first + u + 1, ls]
                    cnt = crow_ref[h * n_first + u:h * n_first + u + 1, ls]
                    keep = rank2_ref[hs, ls] < cnt
                    w = w + jnp.where(keep, e2_ref[hs, ls], zero) * a
                gt_ref[rs, ls] = w * act_ref[rs, ls]

    acc_ref[...] += jnp.dot(vt_ref[...], gt_ref[...], preferred_element_type=f32)

    @pl.when(j == pl.num_programs(1) - 1)
    def _():
        o_ref[...] = h_ref[...] + acc_ref[...].T


def _peer_experts(xn, u, vt, a, cnt, rank2, e2, h):
    t, d = h.shape
    tm, te = TM_PEER, TE_PEER
    nspec = pl.BlockSpec((PEER_H * PEER_NK, tm), lambda i, j: (0, i))
    return pl.pallas_call(
        functools.partial(_peer_expert_kernel, tm=tm, te=te),
        grid=(t // tm, PEER_E // te),
        in_specs=[pl.BlockSpec((tm, d), lambda i, j: (i, 0)),
                  pl.BlockSpec((te, d), lambda i, j: (j, 0)),
                  pl.BlockSpec((d, te), lambda i, j: (0, j)),
                  nspec, nspec, nspec, nspec,
                  pl.BlockSpec((tm, d), lambda i, j: (i, 0))],
        out_specs=pl.BlockSpec((tm, d), lambda i, j: (i, 0)),
        out_shape=jax.ShapeDtypeStruct((t, d), f32),
        scratch_shapes=[pltpu.VMEM((d, tm), f32), pltpu.VMEM((te, tm), bf16), pltpu.VMEM((te, tm), bf16),
                        pltpu.VMEM((PEER_H * te // PEER_NK, tm), bf16),
                        pltpu.VMEM((PEER_H * te // PEER_NK, tm), bf16)],
        compiler_params=_cp(("parallel", "arbitrary")),
        name="peer_experts",
    )(xn, u, vt, a, cnt, rank2, e2, h)


def _transpose_cast_kernel(x_ref, o_ref):
    o_ref[...] = x_ref[...].T.astype(bf16)


def _transpose_cast(x):
    r, c = x.shape
    tr = TE_PEER
    return pl.pallas_call(
        _transpose_cast_kernel,
        grid=(r // tr,),
        in_specs=[pl.BlockSpec((tr, c), lambda i: (i, 0))],
        out_specs=pl.BlockSpec((c, tr), lambda i: (0, i)),
        out_shape=jax.ShapeDtypeStruct((c, r), bf16),
        compiler_params=_cp(("parallel",)),
        name="transpose_cast",
    )(x)


def _permute_w_in(w):
    pad = jnp.zeros((w.shape[0], N_IN - ORIG_END), w.dtype)
    return jnp.concatenate([w[:, :ORIG_GA], w[:, ORIG_NQ:ORIG_GATE], w[:, ORIG_GA:ORIG_NQ],
                            w[:, ORIG_GATE:ORIG_END], pad], axis=1).astype(bf16)


def _tile_flags(blk_any, seq):
    per_tile = blk_any.reshape(-1, seq // TK, TK // SEL_BLOCK).max(axis=-1)
    return (per_tile > 0.5).astype(jnp.int32).reshape(-1)


def _mixers(h, norm_mix, w_in, conv_w, gla_w_a2, gla_b_a, q_norm, k_norm, cmp_pos, cmp_w1, cmp_w2, out_norm,
            batch, seq):
    p = _norm_matmul(h, norm_mix.astype(f32), _permute_w_in(w_in), TN_IN)
    gain = out_norm.astype(f32)
    y_conv = _conv_mixer(p, conv_w, gain[:CONV_W], seq)
    y_gla = _gla_mixer(p, gla_w_a2, gla_b_a, gain[CONV_W:2 * CONV_W], batch, seq)
    kcc = _compress(p, OFF_NKC, cmp_pos[0], cmp_w1[0], cmp_w2[0], k_norm[0], batch, seq, True)
    vcc = _compress(p, OFF_NVC, cmp_pos[1], cmp_w1[1], cmp_w2[1], k_norm[0], batch, seq, False)
    ks, vs, kw, vw = _kv_prep(p, k_norm, batch, seq)
    o_c, selt, blk_any = _nsa_cmp(p, q_norm, kcc, vcc, batch, seq)
    y_nsa = _nsa_attn(p, q_norm, o_c, selt, _tile_flags(blk_any, seq), ks, vs, kw, vw, gain[2 * CONV_W:], batch, seq)
    return y_conv, y_gla, y_nsa


def _layer(h, norm_mix, w_in, conv_w, gla_w_a2, gla_b_a, q_norm, k_norm, cmp_pos, cmp_w1, cmp_w2, out_norm, w_out,
           norm_ffn, peer_w_q, peer_keys, peer_u, peer_v, batch, seq):
    y_conv, y_gla, y_nsa = _mixers(h, norm_mix, w_in, conv_w, gla_w_a2, gla_b_a, q_norm, k_norm, cmp_pos, cmp_w1,
                                   cmp_w2, out_norm, batch, seq)
    h = _out_proj(y_conv, y_gla, y_nsa, w_out, h)
    qry, xn = _norm_matmul(h, norm_ffn.astype(f32), peer_w_q.astype(bf16), 1024, emit_xn=True)
    a, cnt, rank2, e2 = _peer_route(qry, peer_keys)
    return _peer_experts(xn, peer_u.astype(bf16), _transpose_cast(peer_v.astype(f32)), a, cnt, rank2, e2, h)


def kernel(x, norm_mix, w_in, conv_w, gla_w_a2, gla_b_a, nsa_q_norm, nsa_k_norm, nsa_cmp_pos, nsa_cmp_w1,
           nsa_cmp_w2, out_norm, w_out, norm_ffn, peer_w_q, peer_keys, peer_u, peer_v):
    batch, seq, d = x.shape
    h = x.reshape(batch * seq, d)
    for l in range(w_in.shape[0]):
        h = _layer(h, norm_mix[l], w_in[l], conv_w[l], gla_w_a2[l], gla_b_a[l], nsa_q_norm[l], nsa_k_norm[l],
                   nsa_cmp_pos[l], nsa_cmp_w1[l], nsa_cmp_w2[l], out_norm[l], w_out[l], norm_ffn[l], peer_w_q[l],
                   peer_keys[l], peer_u[l], peer_v[l], batch, seq)
    return h.reshape(batch, seq, d)
```

```python
import functools
import math

import numpy as np
import jax
import jax.numpy as jnp
from jax import lax
from jax.experimental import pallas as pl
from jax.experimental.pallas import tpu as pltpu

f32 = jnp.float32
bf16 = jnp.bfloat16

D_MODEL = 2048
EPS = 1e-6
HEAD = 128
CONV_W = 512
CONV_K = 3
GLA_H, GLA_DK, GLA_DV = 4, 64, 128
GLA_LR = 16
GLA_TAU = 16.0
GLA_CHUNK = 64
NSA_G, NSA_R, NSA_D = 2, 4, 128
CMP_LEN, CMP_STRIDE = 32, 16
SEL_BLOCK, SEL_TOPK, SEL_LOCAL = 64, 16, 2
WINDOW = 512
BIG = 1e9
PEER_H, PEER_NK, PEER_TOPK = 8, 128, 16
PEER_E = PEER_NK * PEER_NK
NEG = -1e30
BF16_LANES = 256

OFF_CX, OFF_CB, OFF_CC = 0, 512, 1024
OFF_GQ, OFF_GK, OFF_GV, OFF_GG = 1536, 1792, 2048, 2560
OFF_NQ = 3072
OFF_NKC, OFF_NVC, OFF_NKS, OFF_NVS, OFF_NKW, OFF_NVW = 4096, 4352, 4608, 4864, 5120, 5376
OFF_MISC = 5632
N_IN = 5760
ORIG_GA, ORIG_NQ, ORIG_GATE, ORIG_END = 3072, 3088, 5648, 5672

TM_PROJ = 1024
TN_IN = 1152
TS_ELT = 512
TC_GLA = 512
GLA_UNROLL = 4
TQ = 128
TQ_CMP = 512
TK = 512
CMP_ROWS = 256
TM_PEER = 512
TE_PEER = 1024
TM_ROUTE = 512
PEER_JB = 128
VMEM_LIMIT = 56 * 1024 * 1024
GELU_C = math.sqrt(2.0 / math.pi)
GELU_A = 0.044715


def _cp(sem):
    return pltpu.CompilerParams(dimension_semantics=sem, vmem_limit_bytes=VMEM_LIMIT)


def _gelu(x):
    return 0.5 * x * (1.0 + jnp.tanh(GELU_C * (x + GELU_A * (x * x * x))))


def _split_bf16(x):
    hi = x.astype(bf16)
    lo = (x - hi.astype(f32)).astype(bf16)
    return hi, lo


def _group_rms(y, gain):
    ms = jnp.mean(y * y, axis=-1, keepdims=True)
    return y * lax.rsqrt(ms + EPS) * gain


def _norm_matmul_kernel(x_ref, g_ref, w_ref, *refs, emit_xn):
    if emit_xn:
        o_ref, xo_ref, xn_ref = refs
    else:
        o_ref, xn_ref = refs

    @pl.when(pl.program_id(1) == 0)
    def _():
        x = x_ref[...]
        ms = jnp.mean(x * x, axis=-1, keepdims=True)
        xn = (x * lax.rsqrt(ms + EPS) * g_ref[...]).astype(bf16)
        xn_ref[...] = xn
        if emit_xn:
            xo_ref[...] = xn

    o_ref[...] = jnp.dot(xn_ref[...], w_ref[...], preferred_element_type=f32)


def _norm_matmul(x, gain, w, tn, emit_xn=False):
    t, d = x.shape
    n = w.shape[1]
    tm = TM_PROJ
    out_shape = [jax.ShapeDtypeStruct((t, n), f32)]
    out_specs = [pl.BlockSpec((tm, tn), lambda i, j: (i, j))]
    if emit_xn:
        out_shape.append(jax.ShapeDtypeStruct((t, d), bf16))
        out_specs.append(pl.BlockSpec((tm, d), lambda i, j: (i, 0)))
    outs = pl.pallas_call(
        functools.partial(_norm_matmul_kernel, emit_xn=emit_xn),
        grid=(t // tm, n // tn),
        in_specs=[
            pl.BlockSpec((tm, d), lambda i, j: (i, 0)),
            pl.BlockSpec((1, d), lambda i, j: (0, 0)),
            pl.BlockSpec((d, tn), lambda i, j: (0, j)),
        ],
        out_specs=out_specs,
        out_shape=out_shape,
        scratch_shapes=[pltpu.VMEM((tm, d), bf16)],
        compiler_params=_cp(("parallel", "arbitrary")),
        name="norm_matmul",
    )(x, gain.reshape(1, d), w)
    return outs if emit_xn else outs[0]


def _conv_kernel(cx_ref, cb_ref, cc_ref, pcx_ref, pcc_ref, w_ref, g_ref, o_ref, z_ref, *, ts, tiles_per_seq):
    first = (pl.program_id(0) % tiles_per_seq) == 0
    zp = pcx_ref[...] * pcc_ref[...]
    z_ref[0:8, :] = jnp.where(first, 0.0, zp)
    z_ref[8:8 + ts, :] = cx_ref[...] * cc_ref[...]
    y = (z_ref[8:8 + ts, :] * w_ref[0:1, :]
         + z_ref[7:7 + ts, :] * w_ref[1:2, :]
         + z_ref[6:6 + ts, :] * w_ref[2:3, :])
    y = cb_ref[...] * y
    for c in range(CONV_W // HEAD):
        sl = slice(c * HEAD, (c + 1) * HEAD)
        o_ref[:, sl] = _group_rms(y[:, sl], g_ref[:, sl]).astype(bf16)


def _conv_mixer(p, conv_w, gain, seq):
    t = p.shape[0]
    ts = TS_ELT
    w_t = conv_w.astype(f32).T
    cur = lambda off: pl.BlockSpec((ts, CONV_W), lambda i, off=off: (i, off // CONV_W))
    prev = lambda off: pl.BlockSpec((8, CONV_W), lambda i, off=off: (jnp.maximum(i * (ts // 8) - 1, 0), off // CONV_W))
    return pl.pallas_call(
        functools.partial(_conv_kernel, ts=ts, tiles_per_seq=seq // ts),
        grid=(t // ts,),
        in_specs=[cur(OFF_CX), cur(OFF_CB), cur(OFF_CC), prev(OFF_CX), prev(OFF_CC),
                  pl.BlockSpec((CONV_K, CONV_W), lambda i: (0, 0)),
                  pl.BlockSpec((1, CONV_W), lambda i: (0, 0))],
        out_specs=pl.BlockSpec((ts, CONV_W), lambda i: (i, 0)),
        out_shape=jax.ShapeDtypeStruct((t, CONV_W), bf16),
        scratch_shapes=[pltpu.VMEM((ts + 8, CONV_W), f32)],
        compiler_params=_cp(("parallel",)),
        name="conv_mixer",
    )(p, p, p, p, p, w_t, gain.reshape(1, CONV_W))


def _gla_kernel(q_ref, k_ref, v_ref, gg_ref, misc_ref, wa_ref, ba_ref, gn_ref, o_ref,
                st_ref, la_ref, y_ref, *, tc):
    @pl.when(pl.program_id(1) == 0)
    def _():
        st_ref[...] = jnp.zeros_like(st_ref)

    a_hi, a_lo = _split_bf16(misc_ref[:, 0:GLA_LR])
    w_hi, w_lo = _split_bf16(wa_ref[...])
    z = (jnp.dot(a_hi, w_hi, preferred_element_type=f32)
         + jnp.dot(a_hi, w_lo, preferred_element_type=f32)
         + jnp.dot(a_lo, w_hi, preferred_element_type=f32)) + ba_ref[...]
    la_ref[...] = (jnp.minimum(z, 0.0) - jnp.log(1.0 + jnp.exp(-jnp.abs(z)))) * (1.0 / GLA_TAU)

    c = GLA_CHUNK
    row = lax.broadcasted_iota(jnp.int32, (c, c), 0)
    col = lax.broadcasted_iota(jnp.int32, (c, c), 1)
    causal = row >= col
    tril = causal.astype(bf16)
    scale = GLA_DK ** -0.5

    def chunk(ci, carry):
        r0 = pl.multiple_of(ci * c, c)
        la = la_ref[pl.ds(r0, c), :]
        la_hi, la_lo = _split_bf16(la)
        la_lo2 = (la - la_hi.astype(f32) - la_lo.astype(f32)).astype(bf16)
        bc = (jnp.dot(tril, la_hi, preferred_element_type=f32)
              + jnp.dot(tril, la_lo, preferred_element_type=f32)
              + jnp.dot(tril, la_lo2, preferred_element_type=f32))
        bl = bc[c - 1:c, :]
        qc = q_ref[pl.ds(r0, c), :]
        kc = k_ref[pl.ds(r0, c), :]
        vc = v_ref[pl.ds(r0, c), :].astype(bf16)
        q_dec = (qc * scale * jnp.exp(bc)).astype(bf16)
        k_inv = (kc * jnp.exp(-bc)).astype(bf16)
        k_end = (kc * jnp.exp(bl - bc)).astype(bf16)
        decay = jnp.exp(bl)
        for h in range(GLA_H):
            ks = slice(h * GLA_DK, (h + 1) * GLA_DK)
            vs = slice(h * GLA_DV, (h + 1) * GLA_DV)
            att = lax.dot_general(q_dec[:, ks], k_inv[:, ks], (((1,), (1,)), ((), ())),
                                  preferred_element_type=f32)
            att = jnp.where(causal, att, 0.0).astype(bf16)
            o = jnp.dot(att, vc[:, vs], preferred_element_type=f32)
            st = st_ref[h]
            o = o + lax.dot_general(q_dec[:, ks], st.astype(bf16), (((1,), (1,)), ((), ())),
                                    preferred_element_type=f32)
            y_ref[pl.ds(r0, c), vs] = o
            upd = lax.dot_general(vc[:, vs], k_end[:, ks], (((0,), (0,)), ((), ())),
                                  preferred_element_type=f32)
            st_ref[h] = st * decay[:, ks] + upd
        return carry

    lax.fori_loop(0, tc // c, chunk, 0, unroll=GLA_UNROLL)

    g = gg_ref[...]
    sw = g * (1.0 / (1.0 + jnp.exp(-g)))
    for h in range(GLA_H):
        vs = slice(h * GLA_DV, (h + 1) * GLA_DV)
        o_ref[:, vs] = (_group_rms(y_ref[:, vs], gn_ref[:, vs]) * sw[:, vs]).astype(bf16)


def _gla_mixer(p, w_a2, b_a, gain, batch, seq):
    t = p.shape[0]
    tc = TC_GLA
    nst = seq // tc
    blk = lambda width, off: pl.BlockSpec((tc, width), lambda b, s, off=off, width=width: (b * nst + s, off // width))
    const = lambda shape: pl.BlockSpec(shape, lambda b, s: (0,) * len(shape))
    return pl.pallas_call(
        functools.partial(_gla_kernel, tc=tc),
        grid=(batch, nst),
        in_specs=[blk(256, OFF_GQ), blk(256, OFF_GK), blk(512, OFF_GV), blk(512, OFF_GG), blk(128, OFF_MISC),
                  const((GLA_LR, GLA_H * GLA_DK)), const((1, GLA_H * GLA_DK)), const((1, GLA_H * GLA_DV))],
        out_specs=pl.BlockSpec((tc, GLA_H * GLA_DV), lambda b, s: (b * nst + s, 0)),
        out_shape=jax.ShapeDtypeStruct((t, GLA_H * GLA_DV), bf16),
        scratch_shapes=[pltpu.VMEM((GLA_H, GLA_DV, GLA_DK), f32),
                        pltpu.VMEM((tc, GLA_H * GLA_DK), f32),
                        pltpu.VMEM((tc, GLA_H * GLA_DV), f32)],
        compiler_params=_cp(("parallel", "arbitrary")),
        name="gla_mixer",
    )(p, p, p, p, p, w_a2.astype(f32), b_a.astype(f32).reshape(1, -1), gain.reshape(1, -1))


def _compress_kernel(t_ref, pos_ref, w1_ref, w2_ref, gn_ref, o_ref, b_scr, *, n_piece, apply_norm):
    half = CMP_STRIDE
    acc_a = jnp.zeros((n_piece, NSA_D), f32)
    acc_b = jnp.zeros((n_piece, NSA_D), f32)
    for r in range(half):
        rows = t_ref[pl.ds(r, n_piece, stride=CMP_STRIDE), :]
        xa = (rows + pos_ref[r:r + 1, :]).astype(bf16)
        acc_a = acc_a + jnp.dot(xa, w1_ref[r], preferred_element_type=f32)
        xb = (rows + pos_ref[half + r:half + r + 1, :]).astype(bf16)
        acc_b = acc_b + jnp.dot(xb, w1_ref[half + r], preferred_element_type=f32)
    b_scr[0:n_piece, :] = acc_b
    b_scr[n_piece:n_piece + 8, :] = jnp.zeros((8, NSA_D), f32)
    hid = acc_a + b_scr[1:n_piece + 1, :]
    out = jnp.dot(_gelu(hid).astype(bf16), w2_ref[...], preferred_element_type=f32)
    if apply_norm:
        out = _group_rms(out, gn_ref[...])
    o_ref[0] = out.astype(bf16)


def _compress(p, off, pos, w1, w2, gain, batch, seq, apply_norm):
    n_piece = seq // CMP_STRIDE
    return pl.pallas_call(
        functools.partial(_compress_kernel, n_piece=n_piece, apply_norm=apply_norm),
        grid=(batch, NSA_G),
        in_specs=[pl.BlockSpec((seq, NSA_D), lambda b, g: (b, off // NSA_D + g)),
                  pl.BlockSpec((CMP_LEN, NSA_D), lambda b, g: (0, 0)),
                  pl.BlockSpec((CMP_LEN, NSA_D, NSA_D), lambda b, g: (0, 0, 0)),
                  pl.BlockSpec((NSA_D, NSA_D), lambda b, g: (0, 0)),
                  pl.BlockSpec((1, NSA_D), lambda b, g: (0, 0))],
        out_specs=pl.BlockSpec((1, n_piece, NSA_D), lambda b, g: (b * NSA_G + g, 0, 0)),
        out_shape=jax.ShapeDtypeStruct((batch * NSA_G, n_piece, NSA_D), bf16),
        scratch_shapes=[pltpu.VMEM((n_piece + 8, NSA_D), f32)],
        compiler_params=_cp(("parallel", "parallel")),
        name="nsa_compress",
    )(p, pos.astype(f32), w1.astype(bf16).reshape(CMP_LEN, NSA_D, NSA_D), w2.astype(bf16),
      gain.astype(f32).reshape(1, NSA_D))


def _kvprep_kernel(ks_ref, vs_ref, kw_ref, vw_ref, gs_ref, gw_ref, oks_ref, ovs_ref, okw_ref, ovw_ref):
    lead = pl.program_id(2) == 0
    oks_ref[0] = _group_rms(ks_ref[...], gs_ref[...]).astype(bf16)
    ovs_ref[0] = vs_ref[...].T.astype(bf16)
    okw_ref[0] = jnp.where(lead, 0.0, _group_rms(kw_ref[...], gw_ref[...])).astype(bf16)
    ovw_ref[0] = jnp.where(lead, 0.0, vw_ref[...]).T.astype(bf16)


def _kv_prep(p, k_norm, batch, seq):
    ts = WINDOW
    nst = seq // ts
    src = lambda s: jnp.maximum(s - 1, 0)
    blk = lambda off: pl.BlockSpec((ts, NSA_D), lambda b, g, s, off=off: (b * nst + src(s), off // NSA_D + g))
    gspec = pl.BlockSpec((1, NSA_D), lambda b, g, s: (0, 0))
    kspec = pl.BlockSpec((1, ts, NSA_D), lambda b, g, s: (b * NSA_G + g, src(s), 0))
    vspec = pl.BlockSpec((1, NSA_D, ts), lambda b, g, s: (b * NSA_G + g, 0, src(s)))
    kwspec = pl.BlockSpec((1, ts, NSA_D), lambda b, g, s: (b * NSA_G + g, s, 0))
    vwspec = pl.BlockSpec((1, NSA_D, ts), lambda b, g, s: (b * NSA_G + g, 0, s))
    bg = batch * NSA_G
    return pl.pallas_call(
        _kvprep_kernel,
        grid=(batch, NSA_G, nst + 1),
        in_specs=[blk(OFF_NKS), blk(OFF_NVS), blk(OFF_NKW), blk(OFF_NVW), gspec, gspec],
        out_specs=[kspec, vspec, kwspec, vwspec],
        out_shape=[jax.ShapeDtypeStruct((bg, seq, NSA_D), bf16), jax.ShapeDtypeStruct((bg, NSA_D, seq), bf16),
                   jax.ShapeDtypeStruct((bg, seq + WINDOW, NSA_D), bf16),
                   jax.ShapeDtypeStruct((bg, NSA_D, seq + WINDOW), bf16)],
        compiler_params=_cp(("parallel", "parallel", "arbitrary")),
        name="nsa_kv_prep",
    )(p, p, p, p, k_norm[1].astype(f32).reshape(1, NSA_D), k_norm[2].astype(f32).reshape(1, NSA_D))


def _stack_q(q_ref, qn_ref):
    parts = []
    for r in range(NSA_R):
        qn = _group_rms(q_ref[:, r * NSA_D:(r + 1) * NSA_D], qn_ref[...])
        parts.append((qn * (NSA_D ** -0.5)).astype(bf16))
    return jnp.concatenate(parts, axis=0)


def _head_slope(g, r):
    return jnp.where(g == 0, 1.0, 2.0 ** -NSA_R).astype(f32) * (2.0 ** -(r + 1))


def _nsa_cmp_kernel(q_ref, qn_ref, kcc_ref, vcc_ref, ovt_ref, oc_ref, selt_ref, any_ref, s_scr, p_scr, psum_scr,
                    *, tq, n_cmp, n_sel):
    g = pl.program_id(1)
    q0 = pl.program_id(2) * tq
    qs = _stack_q(q_ref, qn_ref)
    s_scr[...] = lax.dot_general(kcc_ref[0], qs, (((1,), (1,)), ((), ())), preferred_element_type=f32)
    rc = CMP_ROWS
    n_chunk = n_cmp // rc
    n_vis = jnp.clip((q0 + tq - CMP_LEN) // CMP_STRIDE + 1, 0, n_cmp)
    n_act = (n_vis + rc - 1) // rc
    tcol = lax.broadcasted_iota(jnp.int32, (rc, tq), 1)
    nrow = lax.broadcasted_iota(jnp.int32, (rc, tq), 0)

    def rel_end(c):
        return (c * rc + nrow) * CMP_STRIDE + (CMP_LEN - 1 - q0)

    cols = NSA_R * tq
    slopes = jnp.concatenate([jnp.full((1, tq), 1.0, f32) * _head_slope(g, r) for r in range(NSA_R)], axis=1)
    tcol4 = jnp.concatenate([tcol] * NSA_R, axis=1)

    def biased(c, m):
        r0 = pl.multiple_of(c * rc, rc)
        rel = jnp.concatenate([rel_end(c)] * NSA_R, axis=1)
        sb = jnp.where(rel <= tcol4, s_scr[pl.ds(r0, rc), :] + slopes * rel.astype(f32), NEG)
        s_scr[pl.ds(r0, rc), :] = sb
        return jnp.maximum(m, jnp.max(sb, axis=0, keepdims=True))

    m = lax.fori_loop(0, n_act, biased, jnp.full((1, cols), NEG, f32))
    m = jnp.maximum(m, 0.1 * NEG)

    def expo(c, l):
        r0 = pl.multiple_of(c * rc, rc)
        e = jnp.exp(s_scr[pl.ds(r0, rc), :] - m)
        s_scr[pl.ds(r0, rc), :] = e
        return l + jnp.sum(e, axis=0, keepdims=True)

    l = lax.fori_loop(0, n_act, expo, jnp.zeros((1, cols), f32))
    inv = 1.0 / jnp.maximum(l, 1e-30)

    def normalise(c, carry):
        r0 = pl.multiple_of(c * rc, rc)
        pn = s_scr[pl.ds(r0, rc), :] * inv
        p_scr[pl.ds(r0, rc), :] = pn.astype(bf16)
        psum = pn[:, 0:tq]
        for r in range(1, NSA_R):
            psum = psum + pn[:, r * tq:(r + 1) * tq]
        psum_scr[pl.ds(r0, rc), :] = psum
        return carry

    lax.fori_loop(0, n_act, normalise, 0)

    def clear(c, carry):
        r0 = pl.multiple_of(c * rc, rc)
        p_scr[pl.ds(r0, rc), :] = jnp.zeros((rc, NSA_R * tq), bf16)
        psum_scr[pl.ds(r0, rc), :] = jnp.zeros((rc, tq), f32)
        return carry

    lax.fori_loop(n_act, n_chunk, clear, 0)

    vcc = vcc_ref[0]
    for r in range(NSA_R):
        oc_ref[:, r * NSA_D:(r + 1) * NSA_D] = lax.dot_general(
            p_scr[:, r * tq:(r + 1) * tq], vcc, (((0,), (0,)), ((), ())), preferred_element_type=f32)

    p_hi, p_lo = _split_bf16(psum_scr[...])
    imp = (jnp.dot(ovt_ref[...], p_hi, preferred_element_type=f32)
           + jnp.dot(ovt_ref[...], p_lo, preferred_element_type=f32))
    blk = lax.broadcasted_iota(jnp.int32, (n_sel, tq), 0)
    tblk = (q0 + lax.broadcasted_iota(jnp.int32, (n_sel, tq), 1)) // SEL_BLOCK
    lag = tblk - blk
    score = jnp.where(blk == 0, BIG, jnp.where(lag < SEL_LOCAL, BIG, imp))
    score = jnp.where(lag >= 0, score, -1.0)
    sel = jnp.zeros((n_sel, tq), f32)
    for _ in range(SEL_TOPK):
        m = jnp.max(score, axis=0, keepdims=True)
        first = jnp.min(jnp.where(score == m, blk, n_sel), axis=0, keepdims=True)
        hit = blk == first
        sel = jnp.where(hit, 1.0, sel)
        score = jnp.where(hit, -2.0, score)
    selt_ref[0] = sel.astype(bf16)
    for a in range(tq // TQ):
        any_ref[a] = jnp.max(sel[:, a * TQ:(a + 1) * TQ], axis=1, keepdims=True)


def _nsa_cmp(p, q_norm, kcc, vcc, batch, seq):
    tq = TQ_CMP
    nq = seq // tq
    n_cmp = seq // CMP_STRIDE
    n_sel = seq // SEL_BLOCK
    ci = np.arange(n_cmp)[None, :] * CMP_STRIDE
    sj = np.arange(n_sel)[:, None] * SEL_BLOCK
    ovt = np.clip(np.minimum(ci + CMP_LEN, sj + SEL_BLOCK) - np.maximum(ci, sj), 0, None) / CMP_LEN
    ovt = jnp.asarray(ovt, dtype=bf16)
    t = batch * seq
    hw = NSA_R * NSA_D
    return pl.pallas_call(
        functools.partial(_nsa_cmp_kernel, tq=tq, n_cmp=n_cmp, n_sel=n_sel),
        grid=(batch, NSA_G, nq),
        in_specs=[pl.BlockSpec((tq, hw), lambda b, g, i: (b * nq + i, OFF_NQ // hw + g)),
                  pl.BlockSpec((1, NSA_D), lambda b, g, i: (0, 0)),
                  pl.BlockSpec((1, n_cmp, NSA_D), lambda b, g, i: (b * NSA_G + g, 0, 0)),
                  pl.BlockSpec((1, n_cmp, NSA_D), lambda b, g, i: (b * NSA_G + g, 0, 0)),
                  pl.BlockSpec((n_sel, n_cmp), lambda b, g, i: (0, 0))],
        out_specs=[pl.BlockSpec((tq, hw), lambda b, g, i: (b * nq + i, g)),
                   pl.BlockSpec((1, n_sel, tq), lambda b, g, i: (b * NSA_G + g, 0, i)),
                   pl.BlockSpec((tq // TQ, n_sel, 1), lambda b, g, i: ((b * NSA_G + g) * nq + i, 0, 0))],
        out_shape=[jax.ShapeDtypeStruct((t, NSA_G * hw), f32),
                   jax.ShapeDtypeStruct((batch * NSA_G, n_sel, seq), bf16),
                   jax.ShapeDtypeStruct((batch * NSA_G * (seq // TQ), n_sel, 1), f32)],
        scratch_shapes=[pltpu.VMEM((n_cmp, NSA_R * tq), f32), pltpu.VMEM((n_cmp, NSA_R * tq), bf16),
                        pltpu.VMEM((n_cmp, tq), f32)],
        compiler_params=_cp(("parallel", "parallel", "parallel")),
        name="nsa_cmp_select",
    )(p, q_norm.astype(f32).reshape(1, NSA_D), kcc, vcc, ovt)


def _nsa_attn_kernel(flag_ref, q_ref, qn_ref, misc_ref, oc_ref, selt_ref, ks_ref, vst_ref, kw_ref, vwt_ref, gn_ref,
                     o_ref, m_scr, l_scr, acc_scr, *, tq, tk, nkt):
    b = pl.program_id(0)
    g = pl.program_id(1)
    qi = pl.program_id(2)
    nq = pl.num_programs(2)
    q0 = qi * tq
    cols = NSA_R * tq
    qs = _stack_q(q_ref, qn_ref)
    nt_dims = (((1,), (1,)), ((), ()))

    bpt = tk // SEL_BLOCK
    krow = lax.broadcasted_iota(jnp.int32, (tk, tq), 0)
    tcol = lax.broadcasted_iota(jnp.int32, (tk, tq), 1)
    kt_last = (q0 + tq - 1) // tk
    slopes = jnp.concatenate([jnp.full((1, tq), 1.0, f32) * _head_slope(g, r) for r in range(NSA_R)], axis=1)

    m_scr[...] = jnp.full((1, cols), NEG, f32)
    l_scr[...] = jnp.zeros((1, cols), f32)
    acc_scr[...] = jnp.zeros((NSA_D, cols), f32)

    def sel_step(kt, causal):
        k0 = pl.multiple_of(kt * tk, tk)
        kb = ks_ref[0, pl.ds(k0, tk), :]
        vtb = vst_ref[0, :, pl.ds(k0, tk)]
        st = lax.dot_general(kb, qs, nt_dims, preferred_element_type=f32)
        picked = selt_ref[0, pl.ds(pl.multiple_of(kt * bpt, bpt), bpt), :]
        off = (picked.astype(f32) - 1.0) * (-NEG)
        bias = jnp.concatenate([jnp.broadcast_to(off[u:u + 1, :], (SEL_BLOCK, tq)) for u in range(bpt)], axis=0)
        rel = krow + (k0 - q0)
        if causal:
            bias = bias + jnp.where(rel <= tcol, 0.0, NEG)
        relf = rel.astype(f32)
        sr = st + jnp.concatenate([relf] * NSA_R, axis=1) * slopes + jnp.concatenate([bias] * NSA_R, axis=1)
        m_old = m_scr[...]
        m_new = jnp.maximum(m_old, jnp.max(sr, axis=0, keepdims=True))
        alpha = jnp.exp(m_old - m_new)
        pr = jnp.exp(sr - m_new)
        l_scr[...] = alpha * l_scr[...] + jnp.sum(pr, axis=0, keepdims=True)
        acc_scr[...] = alpha * acc_scr[...] + jnp.dot(vtb, pr.astype(bf16), preferred_element_type=f32)
        m_scr[...] = m_new

    fbase = ((b * NSA_G + g) * nq + qi) * nkt

    def sel_loop(kt, carry):
        @pl.when(flag_ref[fbase + kt] > 0)
        def _():
            sel_step(kt, False)
        return carry

    lax.fori_loop(0, kt_last, sel_loop, 0)
    sel_step(kt_last, True)
    o_st = acc_scr[...] * (1.0 / jnp.maximum(l_scr[...], 1e-30))

    wk = WINDOW + tq
    w0 = pl.multiple_of(q0, tq)
    kwin = kw_ref[0, pl.ds(w0, wk), :]
    vwt = vwt_ref[0, :, pl.ds(w0, wk)]
    swt = lax.dot_general(kwin, qs, nt_dims, preferred_element_type=f32)
    jr = lax.broadcasted_iota(jnp.int32, (wk, tq), 0)
    tc = lax.broadcasted_iota(jnp.int32, (wk, tq), 1)
    wbias = jnp.where(jr > tc, jnp.where(jr <= tc + WINDOW, 0.0, NEG), NEG)
    wbias = wbias + jnp.where(jr >= WINDOW - q0, 0.0, NEG)
    jf = jr.astype(f32)
    srw = swt + jnp.concatenate([jf] * NSA_R, axis=1) * slopes + jnp.concatenate([wbias] * NSA_R, axis=1)
    mw = jnp.max(srw, axis=0, keepdims=True)
    prw = jnp.exp(srw - mw)
    denw = jnp.sum(prw, axis=0, keepdims=True)
    o_wt = jnp.dot(vwt, prw.astype(bf16), preferred_element_type=f32) * (1.0 / denw)

    for r in range(NSA_R):
        lanes = []
        for c in range(3):
            l0 = GLA_LR + r * 3 + c
            l1 = l0 + NSA_R * 3
            lanes.append(jnp.where(g == 0, misc_ref[:, l0:l0 + 1], misc_ref[:, l1:l1 + 1]))
        gc, gs, gw = [1.0 / (1.0 + jnp.exp(-x)) for x in lanes]
        hs = slice(r * NSA_D, (r + 1) * NSA_D)
        cs = slice(r * tq, (r + 1) * tq)
        y = gc * oc_ref[:, hs] + gs * o_st[:, cs].T + gw * o_wt[:, cs].T
        o_ref[:, hs] = _group_rms(y, gn_ref[:, hs]).astype(bf16)


def _nsa_attn(p, q_norm, o_c, selt, flags, ks, vst, kw, vwt, gain, batch, seq):
    tq, tk = TQ, TK
    nq = seq // tq
    nkt = seq // tk
    n_sel = seq // SEL_BLOCK
    t = batch * seq
    hw = NSA_R * NSA_D
    slab = lambda rows: pl.BlockSpec((1, rows, NSA_D), lambda b, g, i, f: (b * NSA_G + g, 0, 0),
                                     pipeline_mode=pl.Buffered(1))
    slab_t = lambda rows: pl.BlockSpec((1, NSA_D, rows), lambda b, g, i, f: (b * NSA_G + g, 0, 0),
                                       pipeline_mode=pl.Buffered(1))
    grid_spec = pltpu.PrefetchScalarGridSpec(
        num_scalar_prefetch=1,
        grid=(batch, NSA_G, nq),
        in_specs=[pl.BlockSpec((tq, hw), lambda b, g, i, f: (b * nq + i, OFF_NQ // hw + g)),
                  pl.BlockSpec((1, NSA_D), lambda b, g, i, f: (0, 0)),
                  pl.BlockSpec((tq, 128), lambda b, g, i, f: (b * nq + i, OFF_MISC // 128)),
                  pl.BlockSpec((tq, hw), lambda b, g, i, f: (b * nq + i, g)),
                  pl.BlockSpec((1, n_sel, tq), lambda b, g, i, f: (b * NSA_G + g, 0, i)),
                  slab(seq), slab_t(seq), slab(seq + WINDOW), slab_t(seq + WINDOW),
                  pl.BlockSpec((1, hw), lambda b, g, i, f: (0, g))],
        out_specs=pl.BlockSpec((tq, hw), lambda b, g, i, f: (b * nq + i, g)),
        scratch_shapes=[pltpu.VMEM((1, NSA_R * tq), f32), pltpu.VMEM((1, NSA_R * tq), f32),
                        pltpu.VMEM((NSA_D, NSA_R * tq), f32)],
    )
    return pl.pallas_call(
        functools.partial(_nsa_attn_kernel, tq=tq, tk=tk, nkt=nkt),
        grid_spec=grid_spec,
        out_shape=jax.ShapeDtypeStruct((t, NSA_G * hw), bf16),
        compiler_params=_cp(("parallel", "parallel", "arbitrary")),
        name="nsa_attention",
    )(flags, p, q_norm.astype(f32).reshape(1, NSA_D), p, o_c, selt, ks, vst, kw, vwt, gain.reshape(1, -1))


def _out_proj_kernel(a_ref, b_ref, c_ref, wa_ref, wb_ref, wc_ref, h_ref, o_ref):
    o_ref[...] = (h_ref[...]
                  + jnp.dot(a_ref[...], wa_ref[...], preferred_element_type=f32)
                  + jnp.dot(b_ref[...], wb_ref[...], preferred_element_type=f32)
                  + jnp.dot(c_ref[...], wc_ref[...], preferred_element_type=f32))


def _out_proj(ya, yb, yc, w_out, h):
    t, d = h.shape
    tm, tn = TM_PROJ, 1024
    w = w_out.astype(bf16)
    ka, kb, kc = ya.shape[1], yb.shape[1], yc.shape[1]
    return pl.pallas_call(
        _out_proj_kernel,
        grid=(t // tm, d // tn),
        in_specs=[pl.BlockSpec((tm, ka), lambda i, j: (i, 0)),
                  pl.BlockSpec((tm, kb), lambda i, j: (i, 0)),
                  pl.BlockSpec((tm, kc), lambda i, j: (i, 0)),
                  pl.BlockSpec((ka, tn), lambda i, j: (0, j)),
                  pl.BlockSpec((kb, tn), lambda i, j: (1, j)),
                  pl.BlockSpec((kc, tn), lambda i, j: (1, j)),
                  pl.BlockSpec((tm, tn), lambda i, j: (i, j))],
        out_specs=pl.BlockSpec((tm, tn), lambda i, j: (i, j)),
        out_shape=jax.ShapeDtypeStruct((t, d), f32),
        compiler_params=_cp(("parallel", "parallel")),
        name="out_proj",
    )(ya, yb, yc, w, w, w, h)


def _top_k_columns(v, k, ids=None, want_rank=True):
    if ids is None:
        ids = lax.broadcasted_iota(jnp.int32, v.shape, 0)
    big = jnp.int32(2 ** 30)
    kidx = lax.broadcasted_iota(jnp.int32, (k, v.shape[1]), 0)
    vals = jnp.zeros((k, v.shape[1]), f32)
    firsts = []
    rank = jnp.full(v.shape, float(k), f32) if want_rank else None
    for kk in range(k):
        m = jnp.max(v, axis=0, keepdims=True)
        first = jnp.min(jnp.where(v == m, ids, big), axis=0, keepdims=True)
        hit = ids == first
        if want_rank:
            rank = jnp.where(hit, float(kk), rank)
        v = jnp.where(hit, -jnp.inf, v)
        vals = jnp.where(kidx == kk, m, vals)
        firsts.append(first)
    return vals, rank, firsts


def _pair_candidates(t1, t2):
    n = t1.shape[1]
    k = PEER_TOPK
    i16 = lax.broadcasted_iota(jnp.int32, (k, n), 0)
    i8 = lax.broadcasted_iota(jnp.int32, (8, n), 0)
    vals = [t1[0:1, :] + t2, t1[1:2, :] + t2[0:8, :]]
    pos = [i16, k + i8]
    for a in range(2, 8):
        vals.append(jnp.where(i8 < k // (a + 1), t1[a:a + 1, :] + t2[0:8, :], -jnp.inf))
        pos.append(a * k + i8)
    vals.append(t1[8:16, :] + t2[0:1, :])
    pos.append((8 + i8) * k)
    return jnp.concatenate(vals, axis=0), jnp.concatenate(pos, axis=0)


def _peer_route_kernel(q_ref, key_ref, a_ref, cnt_ref, rank2_ref, e2_ref):
    tm = q_ref.shape[0]
    for h in range(PEER_H):
        sc, tops, ranks = [], [], []
        for half in range(2):
            c0 = (h * 2 + half) * PEER_NK
            qh = q_ref[:, c0:c0 + PEER_NK].astype(bf16)
            s = lax.dot_general(key_ref[h * 2 + half], qh, (((1,), (1,)), ((), ())),
                                preferred_element_type=f32)
            vals, rank, _ = _top_k_columns(s, PEER_TOPK)
            sc.append(s)
            tops.append(vals)
            ranks.append(rank)
        cand, cand_pos = _pair_candidates(tops[0], tops[1])
        best, _, pos = _top_k_columns(cand, PEER_TOPK, ids=cand_pos, want_rank=False)
        mass = jnp.sum(jnp.exp(best - best[0:1, :]), axis=0, keepdims=True)
        aidx = lax.broadcasted_iota(jnp.int32, (PEER_TOPK, tm), 0)
        per_rank = jnp.zeros((PEER_TOPK, tm), f32)
        for p in pos:
            per_rank = per_rank + jnp.where(aidx == p // PEER_TOPK, 1.0, 0.0)
        cnt = jnp.zeros((PEER_NK, tm), f32)
        for a in range(PEER_TOPK):
            cnt = jnp.where(ranks[0] == float(a), per_rank[a:a + 1, :], cnt)
        hs = slice(h * PEER_NK, (h + 1) * PEER_NK)
        a_ref[hs, :] = (jnp.exp(sc[0] - tops[0][0:1, :]) * (0.5 / mass)).astype(bf16)
        cnt_ref[hs, :] = cnt.astype(bf16)
        rank2_ref[hs, :] = ranks[1].astype(bf16)
        e2_ref[hs, :] = jnp.exp(sc[1] - tops[1][0:1, :]).astype(bf16)


def _peer_route(qry, keys):
    t = qry.shape[0]
    tm = TM_ROUTE
    narrow = jax.ShapeDtypeStruct((PEER_H * PEER_NK, t), bf16)
    nspec = pl.BlockSpec((PEER_H * PEER_NK, tm), lambda i: (0, i))
    return pl.pallas_call(
        _peer_route_kernel,
        grid=(t // tm,),
        in_specs=[pl.BlockSpec((tm, PEER_H * 2 * PEER_NK), lambda i: (i, 0)),
                  pl.BlockSpec((PEER_H * 2, PEER_NK, PEER_NK), lambda i: (0, 0, 0))],
        out_specs=[nspec] * 4,
        out_shape=[narrow] * 4,
        compiler_params=_cp(("parallel",)),
        name="peer_route",
    )(qry, keys.astype(bf16).reshape(PEER_H * 2, PEER_NK, PEER_NK))


def _peer_expert_kernel(xn_ref, u_ref, vt_ref, a_ref, cnt_ref, rank2_ref, e2_ref, h_ref, o_ref,
                        acc_ref, act_ref, gt_ref, arow_ref, crow_ref, *, tm, te):
    j = pl.program_id(1)

    @pl.when(j == 0)
    def _():
        acc_ref[...] = jnp.zeros_like(acc_ref)

    n_first = te // PEER_NK
    i0 = pl.multiple_of(j * n_first, n_first)

    a_rows = [a_ref[pl.ds(pl.multiple_of(h * PEER_NK + i0, n_first), n_first), :] for h in range(PEER_H)]
    c_rows = [cnt_ref[pl.ds(pl.multiple_of(h * PEER_NK + i0, n_first), n_first), :] for h in range(PEER_H)]
    for h in range(PEER_H):
        arow_ref[h * n_first:(h + 1) * n_first, :] = a_rows[h]
        crow_ref[h * n_first:(h + 1) * n_first, :] = c_rows[h]

    x = lax.dot_general(u_ref[...], xn_ref[...], (((1,), (1,)), ((), ())), preferred_element_type=f32)
    act_ref[...] = (x * (1.0 + jnp.tanh(x * (GELU_C + (GELU_C * GELU_A) * (x * x))))).astype(bf16)

    jb = PEER_JB
    zero = jnp.zeros((jb, BF16_LANES), bf16)
    for c in range(tm // BF16_LANES):
        ls = slice(c * BF16_LANES, (c + 1) * BF16_LANES)
        for u in range(n_first):
            for q in range(PEER_NK // jb):
                rs = slice(u * PEER_NK + q * jb, u * PEER_NK + (q + 1) * jb)
                w = zero
                for h in range(PEER_H):
                    hs = slice(h * PEER_NK + q * jb, h * PEER_NK + (q + 1) * jb)
                    a = arow_ref[h * n_first + u:h * n_first + u + 1, ls]
                    cnt = crow_ref[h * n_first + u:h * n_first + u + 1, ls]
                    keep = rank2_ref[hs, ls] < cnt
                    w = w + jnp.where(keep, e2_ref[hs, ls], zero) * a
                gt_ref[rs, ls] = w * act_ref[rs, ls]

    acc_ref[...] += jnp.dot(vt_ref[...], gt_ref[...], preferred_element_type=f32)

    @pl.when(j == pl.num_programs(1) - 1)
    def _():
        o_ref[...] = h_ref[...] + acc_ref[...].T


def _peer_experts(xn, u, vt, a, cnt, rank2, e2, h):
    t, d = h.shape
    tm, te = TM_PEER, TE_PEER
    nspec = pl.BlockSpec((PEER_H * PEER_NK, tm), lambda i, j: (0, i))
    return pl.pallas_call(
        functools.partial(_peer_expert_kernel, tm=tm, te=te),
        grid=(t // tm, PEER_E // te),
        in_specs=[pl.BlockSpec((tm, d), lambda i, j: (i, 0)),
                  pl.BlockSpec((te, d), lambda i, j: (j, 0)),
                  pl.BlockSpec((d, te), lambda i, j: (0, j)),
                  nspec, nspec, nspec, nspec,
                  pl.BlockSpec((tm, d), lambda i, j: (i, 0))],
        out_specs=pl.BlockSpec((tm, d), lambda i, j: (i, 0)),
        out_shape=jax.ShapeDtypeStruct((t, d), f32),
        scratch_shapes=[pltpu.VMEM((d, tm), f32), pltpu.VMEM((te, tm), bf16), pltpu.VMEM((te, tm), bf16),
                        pltpu.VMEM((PEER_H * te // PEER_NK, tm), bf16),
                        pltpu.VMEM((PEER_H * te // PEER_NK, tm), bf16)],
        compiler_params=_cp(("parallel", "arbitrary")),
        name="peer_experts",
    )(xn, u, vt, a, cnt, rank2, e2, h)


def _transpose_cast_kernel(x_ref, o_ref):
    o_ref[...] = x_ref[...].T.astype(bf16)


def _transpose_cast(x):
    r, c = x.shape
    tr = TE_PEER
    return pl.pallas_call(
        _transpose_cast_kernel,
        grid=(r // tr,),
        in_specs=[pl.BlockSpec((tr, c), lambda i: (i, 0))],
        out_specs=pl.BlockSpec((c, tr), lambda i: (0, i)),
        out_shape=jax.ShapeDtypeStruct((c, r), bf16),
        compiler_params=_cp(("parallel",)),
        name="transpose_cast",
    )(x)


def _permute_w_in(w):
    pad = jnp.zeros((w.shape[0], N_IN - ORIG_END), w.dtype)
    return jnp.concatenate([w[:, :ORIG_GA], w[:, ORIG_NQ:ORIG_GATE], w[:, ORIG_GA:ORIG_NQ],
                            w[:, ORIG_GATE:ORIG_END], pad], axis=1).astype(bf16)


def _tile_flags(blk_any, seq):
    per_tile = blk_any.reshape(-1, seq // TK, TK // SEL_BLOCK).max(axis=-1)
    return (per_tile > 0.5).astype(jnp.int32).reshape(-1)


def _mixers(h, norm_mix, w_in, conv_w, gla_w_a2, gla_b_a, q_norm, k_norm, cmp_pos, cmp_w1, cmp_w2, out_norm,
            batch, seq):
    p = _norm_matmul(h, norm_mix.astype(f32), _permute_w_in(w_in), TN_IN)
    gain = out_norm.astype(f32)
    y_conv = _conv_mixer(p, conv_w, gain[:CONV_W], seq)
    y_gla = _gla_mixer(p, gla_w_a2, gla_b_a, gain[CONV_W:2 * CONV_W], batch, seq)
    kcc = _compress(p, OFF_NKC, cmp_pos[0], cmp_w1[0], cmp_w2[0], k_norm[0], batch, seq, True)
    vcc = _compress(p, OFF_NVC, cmp_pos[1], cmp_w1[1], cmp_w2[1], k_norm[0], batch, seq, False)
    ks, vs, kw, vw = _kv_prep(p, k_norm, batch, seq)
    o_c, selt, blk_any = _nsa_cmp(p, q_norm, kcc, vcc, batch, seq)
    y_nsa = _nsa_attn(p, q_norm, o_c, selt, _tile_flags(blk_any, seq), ks, vs, kw, vw, gain[2 * CONV_W:], batch, seq)
    return y_conv, y_gla, y_nsa


def _layer(h, norm_mix, w_in, conv_w, gla_w_a2, gla_b_a, q_norm, k_norm, cmp_pos, cmp_w1, cmp_w2, out_norm, w_out,
           norm_ffn, peer_w_q, peer_keys, peer_u, peer_v, batch, seq):
    y_conv, y_gla, y_nsa = _mixers(h, norm_mix, w_in, conv_w, gla_w_a2, gla_b_a, q_norm, k_norm, cmp_pos, cmp_w1,
                                   cmp_w2, out_norm, batch, seq)
    h = _out_proj(y_conv, y_gla, y_nsa, w_out, h)
    qry, xn = _norm_matmul(h, norm_ffn.astype(f32), peer_w_q.astype(bf16), 1024, emit_xn=True)
    a, cnt, rank2, e2 = _peer_route(qry, peer_keys)
    return _peer_experts(xn, peer_u.astype(bf16), _transpose_cast(peer_v.astype(f32)), a, cnt, rank2, e2, h)


def kernel(x, norm_mix, w_in, conv_w, gla_w_a2, gla_b_a, nsa_q_norm, nsa_k_norm, nsa_cmp_pos, nsa_cmp_w1,
           nsa_cmp_w2, out_norm, w_out, norm_ffn, peer_w_q, peer_keys, peer_u, peer_v):
    batch, seq, d = x.shape
    h = x.reshape(batch * seq, d)
    for l in range(w_in.shape[0]):
        h = _layer(h, norm_mix[l], w_in[l], conv_w[l], gla_w_a2[l], gla_b_a[l], nsa_q_norm[l], nsa_k_norm[l],
                   nsa_cmp_pos[l], nsa_cmp_w1[l], nsa_cmp_w2[l], out_norm[l], w_out[l], norm_ffn[l], peer_w_q[l],
                   peer_keys[l], peer_u[l], peer_v[l], batch, seq)
    return h.reshape(batch, seq, d)
```

```python
import functools
import math

import numpy as np
import jax
import jax.numpy as jnp
from jax import lax
from jax.experimental import pallas as pl
from jax.experimental.pallas import tpu as pltpu

f32 = jnp.float32
bf16 = jnp.bfloat16

D_MODEL = 2048
EPS = 1e-6
HEAD = 128
CONV_W = 512
CONV_K = 3
GLA_H, GLA_DK, GLA_DV = 4, 64, 128
GLA_LR = 16
GLA_TAU = 16.0
GLA_CHUNK = 64
NSA_G, NSA_R, NSA_D = 2, 4, 128
CMP_LEN, CMP_STRIDE = 32, 16
SEL_BLOCK, SEL_TOPK, SEL_LOCAL = 64, 16, 2
WINDOW = 512
BIG = 1e9
PEER_H, PEER_NK, PEER_TOPK = 8, 128, 16
PEER_E = PEER_NK * PEER_NK
NEG = -1e30
BF16_LANES = 256

OFF_CX, OFF_CB, OFF_CC = 0, 512, 1024
OFF_GQ, OFF_GK, OFF_GV, OFF_GG = 1536, 1792, 2048, 2560
OFF_NQ = 3072
OFF_NKC, OFF_NVC, OFF_NKS, OFF_NVS, OFF_NKW, OFF_NVW = 4096, 4352, 4608, 4864, 5120, 5376
OFF_MISC = 5632
N_IN = 5760
ORIG_GA, ORIG_NQ, ORIG_GATE, ORIG_END = 3072, 3088, 5648, 5672

TM_PROJ = 1024
TN_IN = 1152
TM_OUTQ = 512
TS_ELT = 512
TC_GLA = 512
GLA_UNROLL = 4
TQ = 128
TQ_CMP = 512
TK = 512
CMP_ROWS = 256
TM_PEER = 512
TE_PEER = 1024
TM_ROUTE = 512
PEER_JB = 128
VMEM_LIMIT = 56 * 1024 * 1024
GELU_C = math.sqrt(2.0 / math.pi)
GELU_A = 0.044715


def _cp(sem):
    return pltpu.CompilerParams(dimension_semantics=sem, vmem_limit_bytes=VMEM_LIMIT)


def _gelu(x):
    return 0.5 * x * (1.0 + jnp.tanh(GELU_C * (x + GELU_A * (x * x * x))))


def _split_bf16(x):
    hi = x.astype(bf16)
    lo = (x - hi.astype(f32)).astype(bf16)
    return hi, lo


def _group_rms(y, gain):
    ms = jnp.mean(y * y, axis=-1, keepdims=True)
    return y * lax.rsqrt(ms + EPS) * gain


def _norm_matmul_kernel(x_ref, g_ref, w_ref, *refs, emit_xn):
    if emit_xn:
        o_ref, xo_ref, xn_ref = refs
    else:
        o_ref, xn_ref = refs

    @pl.when(pl.program_id(1) == 0)
    def _():
        x = x_ref[...]
        ms = jnp.mean(x * x, axis=-1, keepdims=True)
        xn = (x * lax.rsqrt(ms + EPS) * g_ref[...]).astype(bf16)
        xn_ref[...] = xn
        if emit_xn:
            xo_ref[...] = xn

    o_ref[...] = jnp.dot(xn_ref[...], w_ref[...], preferred_element_type=f32)


def _norm_matmul(x, gain, w, tn, emit_xn=False):
    t, d = x.shape
    n = w.shape[1]
    tm = TM_PROJ
    out_shape = [jax.ShapeDtypeStruct((t, n), f32)]
    out_specs = [pl.BlockSpec((tm, tn), lambda i, j: (i, j))]
    if emit_xn:
        out_shape.append(jax.ShapeDtypeStruct((t, d), bf16))
        out_specs.append(pl.BlockSpec((tm, d), lambda i, j: (i, 0)))
    outs = pl.pallas_call(
        functools.partial(_norm_matmul_kernel, emit_xn=emit_xn),
        grid=(t // tm, n // tn),
        in_specs=[
            pl.BlockSpec((tm, d), lambda i, j: (i, 0)),
            pl.BlockSpec((1, d), lambda i, j: (0, 0)),
            pl.BlockSpec((d, tn), lambda i, j: (0, j)),
        ],
        out_specs=out_specs,
        out_shape=out_shape,
        scratch_shapes=[pltpu.VMEM((tm, d), bf16)],
        compiler_params=_cp(("parallel", "arbitrary")),
        name="norm_matmul",
    )(x, gain.reshape(1, d), w)
    return outs if emit_xn else outs[0]


def _conv_kernel(cx_ref, cb_ref, cc_ref, pcx_ref, pcc_ref, w_ref, g_ref, o_ref, z_ref, *, ts, tiles_per_seq):
    first = (pl.program_id(0) % tiles_per_seq) == 0
    zp = pcx_ref[...] * pcc_ref[...]
    z_ref[0:8, :] = jnp.where(first, 0.0, zp)
    z_ref[8:8 + ts, :] = cx_ref[...] * cc_ref[...]
    y = (z_ref[8:8 + ts, :] * w_ref[0:1, :]
         + z_ref[7:7 + ts, :] * w_ref[1:2, :]
         + z_ref[6:6 + ts, :] * w_ref[2:3, :])
    y = cb_ref[...] * y
    for c in range(CONV_W // HEAD):
        sl = slice(c * HEAD, (c + 1) * HEAD)
        o_ref[:, sl] = _group_rms(y[:, sl], g_ref[:, sl]).astype(bf16)


def _conv_mixer(p, conv_w, gain, seq):
    t = p.shape[0]
    ts = TS_ELT
    w_t = conv_w.astype(f32).T
    cur = lambda off: pl.BlockSpec((ts, CONV_W), lambda i, off=off: (i, off // CONV_W))
    prev = lambda off: pl.BlockSpec((8, CONV_W), lambda i, off=off: (jnp.maximum(i * (ts // 8) - 1, 0), off // CONV_W))
    return pl.pallas_call(
        functools.partial(_conv_kernel, ts=ts, tiles_per_seq=seq // ts),
        grid=(t // ts,),
        in_specs=[cur(OFF_CX), cur(OFF_CB), cur(OFF_CC), prev(OFF_CX), prev(OFF_CC),
                  pl.BlockSpec((CONV_K, CONV_W), lambda i: (0, 0)),
                  pl.BlockSpec((1, CONV_W), lambda i: (0, 0))],
        out_specs=pl.BlockSpec((ts, CONV_W), lambda i: (i, 0)),
        out_shape=jax.ShapeDtypeStruct((t, CONV_W), bf16),
        scratch_shapes=[pltpu.VMEM((ts + 8, CONV_W), f32)],
        compiler_params=_cp(("parallel",)),
        name="conv_mixer",
    )(p, p, p, p, p, w_t, gain.reshape(1, CONV_W))


def _gla_kernel(q_ref, k_ref, v_ref, gg_ref, misc_ref, wa_ref, ba_ref, gn_ref, o_ref,
                st_ref, la_ref, y_ref, *, tc):
    @pl.when(pl.program_id(1) == 0)
    def _():
        st_ref[...] = jnp.zeros_like(st_ref)

    a_hi, a_lo = _split_bf16(misc_ref[:, 0:GLA_LR])
    w_hi, w_lo = _split_bf16(wa_ref[...])
    z = (jnp.dot(a_hi, w_hi, preferred_element_type=f32)
         + jnp.dot(a_hi, w_lo, preferred_element_type=f32)
         + jnp.dot(a_lo, w_hi, preferred_element_type=f32)) + ba_ref[...]
    la_ref[...] = (jnp.minimum(z, 0.0) - jnp.log(1.0 + jnp.exp(-jnp.abs(z)))) * (1.0 / GLA_TAU)

    c = GLA_CHUNK
    row = lax.broadcasted_iota(jnp.int32, (c, c), 0)
    col = lax.broadcasted_iota(jnp.int32, (c, c), 1)
    causal = row >= col
    tril = causal.astype(bf16)
    scale = GLA_DK ** -0.5

    def chunk(ci, carry):
        r0 = pl.multiple_of(ci * c, c)
        la = la_ref[pl.ds(r0, c), :]
        la_hi, la_lo = _split_bf16(la)
        la_lo2 = (la - la_hi.astype(f32) - la_lo.astype(f32)).astype(bf16)
        bc = (jnp.dot(tril, la_hi, preferred_element_type=f32)
              + jnp.dot(tril, la_lo, preferred_element_type=f32)
              + jnp.dot(tril, la_lo2, preferred_element_type=f32))
        bl = bc[c - 1:c, :]
        qc = q_ref[pl.ds(r0, c), :]
        kc = k_ref[pl.ds(r0, c), :]
        vc = v_ref[pl.ds(r0, c), :].astype(bf16)
        q_dec = (qc * scale * jnp.exp(bc)).astype(bf16)
        k_inv = (kc * jnp.exp(-bc)).astype(bf16)
        k_end = (kc * jnp.exp(bl - bc)).astype(bf16)
        decay = jnp.exp(bl)
        for h in range(GLA_H):
            ks = slice(h * GLA_DK, (h + 1) * GLA_DK)
            vs = slice(h * GLA_DV, (h + 1) * GLA_DV)
            att = lax.dot_general(q_dec[:, ks], k_inv[:, ks], (((1,), (1,)), ((), ())),
                                  preferred_element_type=f32)
            att = jnp.where(causal, att, 0.0).astype(bf16)
            o = jnp.dot(att, vc[:, vs], preferred_element_type=f32)
            st = st_ref[h]
            o = o + lax.dot_general(q_dec[:, ks], st.astype(bf16), (((1,), (1,)), ((), ())),
                                    preferred_element_type=f32)
            y_ref[pl.ds(r0, c), vs] = o
            upd = lax.dot_general(vc[:, vs], k_end[:, ks], (((0,), (0,)), ((), ())),
                                  preferred_element_type=f32)
            st_ref[h] = st * decay[:, ks] + upd
        return carry

    lax.fori_loop(0, tc // c, chunk, 0, unroll=GLA_UNROLL)

    g = gg_ref[...]
    sw = g * (1.0 / (1.0 + jnp.exp(-g)))
    for h in range(GLA_H):
        vs = slice(h * GLA_DV, (h + 1) * GLA_DV)
        o_ref[:, vs] = (_group_rms(y_ref[:, vs], gn_ref[:, vs]) * sw[:, vs]).astype(bf16)


def _gla_mixer(p, w_a2, b_a, gain, batch, seq):
    t = p.shape[0]
    tc = TC_GLA
    nst = seq // tc
    blk = lambda width, off: pl.BlockSpec((tc, width), lambda b, s, off=off, width=width: (b * nst + s, off // width))
    const = lambda shape: pl.BlockSpec(shape, lambda b, s: (0,) * len(shape))
    return pl.pallas_call(
        functools.partial(_gla_kernel, tc=tc),
        grid=(batch, nst),
        in_specs=[blk(256, OFF_GQ), blk(256, OFF_GK), blk(512, OFF_GV), blk(512, OFF_GG), blk(128, OFF_MISC),
                  const((GLA_LR, GLA_H * GLA_DK)), const((1, GLA_H * GLA_DK)), const((1, GLA_H * GLA_DV))],
        out_specs=pl.BlockSpec((tc, GLA_H * GLA_DV), lambda b, s: (b * nst + s, 0)),
        out_shape=jax.ShapeDtypeStruct((t, GLA_H * GLA_DV), bf16),
        scratch_shapes=[pltpu.VMEM((GLA_H, GLA_DV, GLA_DK), f32),
                        pltpu.VMEM((tc, GLA_H * GLA_DK), f32),
                        pltpu.VMEM((tc, GLA_H * GLA_DV), f32)],
        compiler_params=_cp(("parallel", "arbitrary")),
        name="gla_mixer",
    )(p, p, p, p, p, w_a2.astype(f32), b_a.astype(f32).reshape(1, -1), gain.reshape(1, -1))


def _compress_kernel(t_ref, pos_ref, w1_ref, w2_ref, gn_ref, o_ref, b_scr, *, n_piece, apply_norm):
    half = CMP_STRIDE
    acc_a = jnp.zeros((n_piece, NSA_D), f32)
    acc_b = jnp.zeros((n_piece, NSA_D), f32)
    for r in range(half):
        rows = t_ref[pl.ds(r, n_piece, stride=CMP_STRIDE), :]
        xa = (rows + pos_ref[r:r + 1, :]).astype(bf16)
        acc_a = acc_a + jnp.dot(xa, w1_ref[r], preferred_element_type=f32)
        xb = (rows + pos_ref[half + r:half + r + 1, :]).astype(bf16)
        acc_b = acc_b + jnp.dot(xb, w1_ref[half + r], preferred_element_type=f32)
    b_scr[0:n_piece, :] = acc_b
    b_scr[n_piece:n_piece + 8, :] = jnp.zeros((8, NSA_D), f32)
    hid = acc_a + b_scr[1:n_piece + 1, :]
    out = jnp.dot(_gelu(hid).astype(bf16), w2_ref[...], preferred_element_type=f32)
    if apply_norm:
        out = _group_rms(out, gn_ref[...])
    o_ref[0] = out.astype(bf16)


def _compress(p, off, pos, w1, w2, gain, batch, seq, apply_norm):
    n_piece = seq // CMP_STRIDE
    return pl.pallas_call(
        functools.partial(_compress_kernel, n_piece=n_piece, apply_norm=apply_norm),
        grid=(batch, NSA_G),
        in_specs=[pl.BlockSpec((seq, NSA_D), lambda b, g: (b, off // NSA_D + g)),
                  pl.BlockSpec((CMP_LEN, NSA_D), lambda b, g: (0, 0)),
                  pl.BlockSpec((CMP_LEN, NSA_D, NSA_D), lambda b, g: (0, 0, 0)),
                  pl.BlockSpec((NSA_D, NSA_D), lambda b, g: (0, 0)),
                  pl.BlockSpec((1, NSA_D), lambda b, g: (0, 0))],
        out_specs=pl.BlockSpec((1, n_piece, NSA_D), lambda b, g: (b * NSA_G + g, 0, 0)),
        out_shape=jax.ShapeDtypeStruct((batch * NSA_G, n_piece, NSA_D), bf16),
        scratch_shapes=[pltpu.VMEM((n_piece + 8, NSA_D), f32)],
        compiler_params=_cp(("parallel", "parallel")),
        name="nsa_compress",
    )(p, pos.astype(f32), w1.astype(bf16).reshape(CMP_LEN, NSA_D, NSA_D), w2.astype(bf16),
      gain.astype(f32).reshape(1, NSA_D))


def _kvprep_kernel(ks_ref, vs_ref, kw_ref, vw_ref, gs_ref, gw_ref, oks_ref, ovs_ref, okw_ref, ovw_ref):
    lead = pl.program_id(2) == 0
    oks_ref[0] = _group_rms(ks_ref[...], gs_ref[...]).astype(bf16)
    ovs_ref[0] = vs_ref[...].T.astype(bf16)
    okw_ref[0] = jnp.where(lead, 0.0, _group_rms(kw_ref[...], gw_ref[...])).astype(bf16)
    ovw_ref[0] = jnp.where(lead, 0.0, vw_ref[...]).T.astype(bf16)


def _kv_prep(p, k_norm, batch, seq):
    ts = WINDOW
    nst = seq // ts
    src = lambda s: jnp.maximum(s - 1, 0)
    blk = lambda off: pl.BlockSpec((ts, NSA_D), lambda b, g, s, off=off: (b * nst + src(s), off // NSA_D + g))
    gspec = pl.BlockSpec((1, NSA_D), lambda b, g, s: (0, 0))
    kspec = pl.BlockSpec((1, ts, NSA_D), lambda b, g, s: (b * NSA_G + g, src(s), 0))
    vspec = pl.BlockSpec((1, NSA_D, ts), lambda b, g, s: (b * NSA_G + g, 0, src(s)))
    kwspec = pl.BlockSpec((1, ts, NSA_D), lambda b, g, s: (b * NSA_G + g, s, 0))
    vwspec = pl.BlockSpec((1, NSA_D, ts), lambda b, g, s: (b * NSA_G + g, 0, s))
    bg = batch * NSA_G
    return pl.pallas_call(
        _kvprep_kernel,
        grid=(batch, NSA_G, nst + 1),
        in_specs=[blk(OFF_NKS), blk(OFF_NVS), blk(OFF_NKW), blk(OFF_NVW), gspec, gspec],
        out_specs=[kspec, vspec, kwspec, vwspec],
        out_shape=[jax.ShapeDtypeStruct((bg, seq, NSA_D), bf16), jax.ShapeDtypeStruct((bg, NSA_D, seq), bf16),
                   jax.ShapeDtypeStruct((bg, seq + WINDOW, NSA_D), bf16),
                   jax.ShapeDtypeStruct((bg, NSA_D, seq + WINDOW), bf16)],
        compiler_params=_cp(("parallel", "parallel", "arbitrary")),
        name="nsa_kv_prep",
    )(p, p, p, p, k_norm[1].astype(f32).reshape(1, NSA_D), k_norm[2].astype(f32).reshape(1, NSA_D))


def _stack_q(q_ref, qn_ref):
    parts = []
    for r in range(NSA_R):
        qn = _group_rms(q_ref[:, r * NSA_D:(r + 1) * NSA_D], qn_ref[...])
        parts.append((qn * (NSA_D ** -0.5)).astype(bf16))
    return jnp.concatenate(parts, axis=0)


def _head_slope(g, r):
    return jnp.where(g == 0, 1.0, 2.0 ** -NSA_R).astype(f32) * (2.0 ** -(r + 1))


def _nsa_cmp_kernel(q_ref, qn_ref, kcc_ref, vcc_ref, ovt_ref, oc_ref, selt_ref, any_ref, s_scr, p_scr, psum_scr,
                    *, tq, n_cmp, n_sel):
    g = pl.program_id(1)
    q0 = pl.program_id(2) * tq
    qs = _stack_q(q_ref, qn_ref)
    s_scr[...] = lax.dot_general(kcc_ref[0], qs, (((1,), (1,)), ((), ())), preferred_element_type=f32)
    rc = CMP_ROWS
    n_chunk = n_cmp // rc
    n_vis = jnp.clip((q0 + tq - CMP_LEN) // CMP_STRIDE + 1, 0, n_cmp)
    n_act = (n_vis + rc - 1) // rc
    tcol = lax.broadcasted_iota(jnp.int32, (rc, tq), 1)
    nrow = lax.broadcasted_iota(jnp.int32, (rc, tq), 0)

    def rel_end(c):
        return (c * rc + nrow) * CMP_STRIDE + (CMP_LEN - 1 - q0)

    cols = NSA_R * tq
    slopes = jnp.concatenate([jnp.full((1, tq), 1.0, f32) * _head_slope(g, r) for r in range(NSA_R)], axis=1)
    tcol4 = jnp.concatenate([tcol] * NSA_R, axis=1)

    def biased(c, m):
        r0 = pl.multiple_of(c * rc, rc)
        rel = jnp.concatenate([rel_end(c)] * NSA_R, axis=1)
        sb = jnp.where(rel <= tcol4, s_scr[pl.ds(r0, rc), :] + slopes * rel.astype(f32), NEG)
        s_scr[pl.ds(r0, rc), :] = sb
        return jnp.maximum(m, jnp.max(sb, axis=0, keepdims=True))

    m = lax.fori_loop(0, n_act, biased, jnp.full((1, cols), NEG, f32))
    m = jnp.maximum(m, 0.1 * NEG)

    def expo(c, l):
        r0 = pl.multiple_of(c * rc, rc)
        e = jnp.exp(s_scr[pl.ds(r0, rc), :] - m)
        s_scr[pl.ds(r0, rc), :] = e
        return l + jnp.sum(e, axis=0, keepdims=True)

    l = lax.fori_loop(0, n_act, expo, jnp.zeros((1, cols), f32))
    inv = 1.0 / jnp.maximum(l, 1e-30)

    def normalise(c, carry):
        r0 = pl.multiple_of(c * rc, rc)
        pn = s_scr[pl.ds(r0, rc), :] * inv
        p_scr[pl.ds(r0, rc), :] = pn.astype(bf16)
        psum = pn[:, 0:tq]
        for r in range(1, NSA_R):
            psum = psum + pn[:, r * tq:(r + 1) * tq]
        psum_scr[pl.ds(r0, rc), :] = psum
        return carry

    lax.fori_loop(0, n_act, normalise, 0)

    def clear(c, carry):
        r0 = pl.multiple_of(c * rc, rc)
        p_scr[pl.ds(r0, rc), :] = jnp.zeros((rc, NSA_R * tq), bf16)
        psum_scr[pl.ds(r0, rc), :] = jnp.zeros((rc, tq), f32)
        return carry

    lax.fori_loop(n_act, n_chunk, clear, 0)

    vcc = vcc_ref[0]
    for r in range(NSA_R):
        oc_ref[:, r * NSA_D:(r + 1) * NSA_D] = lax.dot_general(
            p_scr[:, r * tq:(r + 1) * tq], vcc, (((0,), (0,)), ((), ())), preferred_element_type=f32)

    p_hi, p_lo = _split_bf16(psum_scr[...])
    imp = (jnp.dot(ovt_ref[...], p_hi, preferred_element_type=f32)
           + jnp.dot(ovt_ref[...], p_lo, preferred_element_type=f32))
    blk = lax.broadcasted_iota(jnp.int32, (n_sel, tq), 0)
    tblk = (q0 + lax.broadcasted_iota(jnp.int32, (n_sel, tq), 1)) // SEL_BLOCK
    lag = tblk - blk
    score = jnp.where(blk == 0, BIG, jnp.where(lag < SEL_LOCAL, BIG, imp))
    score = jnp.where(lag >= 0, score, -1.0)
    sel = jnp.zeros((n_sel, tq), f32)
    for _ in range(SEL_TOPK):
        m = jnp.max(score, axis=0, keepdims=True)
        first = jnp.min(jnp.where(score == m, blk, n_sel), axis=0, keepdims=True)
        hit = blk == first
        sel = jnp.where(hit, 1.0, sel)
        score = jnp.where(hit, -2.0, score)
    selt_ref[0] = sel.astype(bf16)
    for a in range(tq // TQ):
        any_ref[a] = jnp.max(sel[:, a * TQ:(a + 1) * TQ], axis=1, keepdims=True)


def _nsa_cmp(p, q_norm, kcc, vcc, batch, seq):
    tq = TQ_CMP
    nq = seq // tq
    n_cmp = seq // CMP_STRIDE
    n_sel = seq // SEL_BLOCK
    ci = np.arange(n_cmp)[None, :] * CMP_STRIDE
    sj = np.arange(n_sel)[:, None] * SEL_BLOCK
    ovt = np.clip(np.minimum(ci + CMP_LEN, sj + SEL_BLOCK) - np.maximum(ci, sj), 0, None) / CMP_LEN
    ovt = jnp.asarray(ovt, dtype=bf16)
    t = batch * seq
    hw = NSA_R * NSA_D
    return pl.pallas_call(
        functools.partial(_nsa_cmp_kernel, tq=tq, n_cmp=n_cmp, n_sel=n_sel),
        grid=(batch, NSA_G, nq),
        in_specs=[pl.BlockSpec((tq, hw), lambda b, g, i: (b * nq + i, OFF_NQ // hw + g)),
                  pl.BlockSpec((1, NSA_D), lambda b, g, i: (0, 0)),
                  pl.BlockSpec((1, n_cmp, NSA_D), lambda b, g, i: (b * NSA_G + g, 0, 0)),
                  pl.BlockSpec((1, n_cmp, NSA_D), lambda b, g, i: (b * NSA_G + g, 0, 0)),
                  pl.BlockSpec((n_sel, n_cmp), lambda b, g, i: (0, 0))],
        out_specs=[pl.BlockSpec((tq, hw), lambda b, g, i: (b * nq + i, g)),
                   pl.BlockSpec((1, n_sel, tq), lambda b, g, i: (b * NSA_G + g, 0, i)),
                   pl.BlockSpec((tq // TQ, n_sel, 1), lambda b, g, i: ((b * NSA_G + g) * nq + i, 0, 0))],
        out_shape=[jax.ShapeDtypeStruct((t, NSA_G * hw), f32),
                   jax.ShapeDtypeStruct((batch * NSA_G, n_sel, seq), bf16),
                   jax.ShapeDtypeStruct((batch * NSA_G * (seq // TQ), n_sel, 1), f32)],
        scratch_shapes=[pltpu.VMEM((n_cmp, NSA_R * tq), f32), pltpu.VMEM((n_cmp, NSA_R * tq), bf16),
                        pltpu.VMEM((n_cmp, tq), f32)],
        compiler_params=_cp(("parallel", "parallel", "parallel")),
        name="nsa_cmp_select",
    )(p, q_norm.astype(f32).reshape(1, NSA_D), kcc, vcc, ovt)


def _nsa_attn_kernel(flag_ref, q_ref, qn_ref, misc_ref, oc_ref, selt_ref, ks_ref, vst_ref, kw_ref, vwt_ref, gn_ref,
                     o_ref, m_scr, l_scr, acc_scr, *, tq, tk, nkt):
    b = pl.program_id(0)
    g = pl.program_id(1)
    qi = pl.program_id(2)
    nq = pl.num_programs(2)
    q0 = qi * tq
    cols = NSA_R * tq
    qs = _stack_q(q_ref, qn_ref)
    nt_dims = (((1,), (1,)), ((), ()))

    bpt = tk // SEL_BLOCK
    krow = lax.broadcasted_iota(jnp.int32, (tk, tq), 0)
    tcol = lax.broadcasted_iota(jnp.int32, (tk, tq), 1)
    kt_last = (q0 + tq - 1) // tk
    slopes = jnp.concatenate([jnp.full((1, tq), 1.0, f32) * _head_slope(g, r) for r in range(NSA_R)], axis=1)

    m_scr[...] = jnp.full((1, cols), NEG, f32)
    l_scr[...] = jnp.zeros((1, cols), f32)
    acc_scr[...] = jnp.zeros((NSA_D, cols), f32)

    def sel_step(kt, causal):
        k0 = pl.multiple_of(kt * tk, tk)
        kb = ks_ref[0, pl.ds(k0, tk), :]
        vtb = vst_ref[0, :, pl.ds(k0, tk)]
        st = lax.dot_general(kb, qs, nt_dims, preferred_element_type=f32)
        picked = selt_ref[0, pl.ds(pl.multiple_of(kt * bpt, bpt), bpt), :]
        off = (picked.astype(f32) - 1.0) * (-NEG)
        bias = jnp.concatenate([jnp.broadcast_to(off[u:u + 1, :], (SEL_BLOCK, tq)) for u in range(bpt)], axis=0)
        rel = krow + (k0 - q0)
        if causal:
            bias = bias + jnp.where(rel <= tcol, 0.0, NEG)
        relf = rel.astype(f32)
        sr = st + jnp.concatenate([relf] * NSA_R, axis=1) * slopes + jnp.concatenate([bias] * NSA_R, axis=1)
        m_old = m_scr[...]
        m_new = jnp.maximum(m_old, jnp.max(sr, axis=0, keepdims=True))
        alpha = jnp.exp(m_old - m_new)
        pr = jnp.exp(sr - m_new)
        l_scr[...] = alpha * l_scr[...] + jnp.sum(pr, axis=0, keepdims=True)
        acc_scr[...] = alpha * acc_scr[...] + jnp.dot(vtb, pr.astype(bf16), preferred_element_type=f32)
        m_scr[...] = m_new

    fbase = ((b * NSA_G + g) * nq + qi) * nkt

    def sel_loop(kt, carry):
        @pl.when(flag_ref[fbase + kt] > 0)
        def _():
            sel_step(kt, False)
        return carry

    lax.fori_loop(0, kt_last, sel_loop, 0)
    sel_step(kt_last, True)
    o_st = acc_scr[...] * (1.0 / jnp.maximum(l_scr[...], 1e-30))

    wk = WINDOW + tq
    w0 = pl.multiple_of(q0, tq)
    kwin = kw_ref[0, pl.ds(w0, wk), :]
    vwt = vwt_ref[0, :, pl.ds(w0, wk)]
    swt = lax.dot_general(kwin, qs, nt_dims, preferred_element_type=f32)
    jr = lax.broadcasted_iota(jnp.int32, (wk, tq), 0)
    tc = lax.broadcasted_iota(jnp.int32, (wk, tq), 1)
    wbias = jnp.where(jr > tc, jnp.where(jr <= tc + WINDOW, 0.0, NEG), NEG)
    wbias = wbias + jnp.where(jr >= WINDOW - q0, 0.0, NEG)
    jf = jr.astype(f32)
    srw = swt + jnp.concatenate([jf] * NSA_R, axis=1) * slopes + jnp.concatenate([wbias] * NSA_R, axis=1)
    mw = jnp.max(srw, axis=0, keepdims=True)
    prw = jnp.exp(srw - mw)
    denw = jnp.sum(prw, axis=0, keepdims=True)
    o_wt = jnp.dot(vwt, prw.astype(bf16), preferred_element_type=f32) * (1.0 / denw)

    for r in range(NSA_R):
        lanes = []
        for c in range(3):
            l0 = GLA_LR + r * 3 + c
            l1 = l0 + NSA_R * 3
            lanes.append(jnp.where(g == 0, misc_ref[:, l0:l0 + 1], misc_ref[:, l1:l1 + 1]))
        gc, gs, gw = [1.0 / (1.0 + jnp.exp(-x)) for x in lanes]
        hs = slice(r * NSA_D, (r + 1) * NSA_D)
        cs = slice(r * tq, (r + 1) * tq)
        y = gc * oc_ref[:, hs] + gs * o_st[:, cs].T + gw * o_wt[:, cs].T
        o_ref[:, hs] = _group_rms(y, gn_ref[:, hs]).astype(bf16)


def _nsa_attn(p, q_norm, o_c, selt, flags, ks, vst, kw, vwt, gain, batch, seq):
    tq, tk = TQ, TK
    nq = seq // tq
    nkt = seq // tk
    n_sel = seq // SEL_BLOCK
    t = batch * seq
    hw = NSA_R * NSA_D
    slab = lambda rows: pl.BlockSpec((1, rows, NSA_D), lambda b, g, i, f: (b * NSA_G + g, 0, 0),
                                     pipeline_mode=pl.Buffered(1))
    slab_t = lambda rows: pl.BlockSpec((1, NSA_D, rows), lambda b, g, i, f: (b * NSA_G + g, 0, 0),
                                       pipeline_mode=pl.Buffered(1))
    grid_spec = pltpu.PrefetchScalarGridSpec(
        num_scalar_prefetch=1,
        grid=(batch, NSA_G, nq),
        in_specs=[pl.BlockSpec((tq, hw), lambda b, g, i, f: (b * nq + i, OFF_NQ // hw + g)),
                  pl.BlockSpec((1, NSA_D), lambda b, g, i, f: (0, 0)),
                  pl.BlockSpec((tq, 128), lambda b, g, i, f: (b * nq + i, OFF_MISC // 128)),
                  pl.BlockSpec((tq, hw), lambda b, g, i, f: (b * nq + i, g)),
                  pl.BlockSpec((1, n_sel, tq), lambda b, g, i, f: (b * NSA_G + g, 0, i)),
                  slab(seq), slab_t(seq), slab(seq + WINDOW), slab_t(seq + WINDOW),
                  pl.BlockSpec((1, hw), lambda b, g, i, f: (0, g))],
        out_specs=pl.BlockSpec((tq, hw), lambda b, g, i, f: (b * nq + i, g)),
        scratch_shapes=[pltpu.VMEM((1, NSA_R * tq), f32), pltpu.VMEM((1, NSA_R * tq), f32),
                        pltpu.VMEM((NSA_D, NSA_R * tq), f32)],
    )
    return pl.pallas_call(
        functools.partial(_nsa_attn_kernel, tq=tq, tk=tk, nkt=nkt),
        grid_spec=grid_spec,
        out_shape=jax.ShapeDtypeStruct((t, NSA_G * hw), bf16),
        compiler_params=_cp(("parallel", "parallel", "arbitrary")),
        name="nsa_attention",
    )(flags, p, q_norm.astype(f32).reshape(1, NSA_D), p, o_c, selt, ks, vst, kw, vwt, gain.reshape(1, -1))


def _out_qproj_kernel(a_ref, b_ref, c_ref, wa_ref, wb_ref, wc_ref, h_ref, g_ref, wq_ref, o_ref, xo_ref, q_ref):
    h1 = (h_ref[...]
          + jnp.dot(a_ref[...], wa_ref[...], preferred_element_type=f32)
          + jnp.dot(b_ref[...], wb_ref[...], preferred_element_type=f32)
          + jnp.dot(c_ref[...], wc_ref[...], preferred_element_type=f32))
    o_ref[...] = h1
    ms = jnp.mean(h1 * h1, axis=-1, keepdims=True)
    xn = (h1 * lax.rsqrt(ms + EPS) * g_ref[...]).astype(bf16)
    xo_ref[...] = xn
    q_ref[...] = jnp.dot(xn, wq_ref[...], preferred_element_type=f32)


def _out_qproj(ya, yb, yc, w_out, h, gain, w_q):
    t, d = h.shape
    tm = TM_OUTQ
    w = w_out.astype(bf16)
    ka, kb, kc = ya.shape[1], yb.shape[1], yc.shape[1]
    once = pl.Buffered(1)
    row = lambda width: pl.BlockSpec((tm, width), lambda i: (i, 0))
    return pl.pallas_call(
        _out_qproj_kernel,
        grid=(t // tm,),
        in_specs=[row(ka), row(kb), row(kc),
                  pl.BlockSpec((ka, d), lambda i: (0, 0), pipeline_mode=once),
                  pl.BlockSpec((kb, d), lambda i: (1, 0), pipeline_mode=once),
                  pl.BlockSpec((kc, d), lambda i: (1, 0), pipeline_mode=once),
                  row(d),
                  pl.BlockSpec((1, d), lambda i: (0, 0)),
                  pl.BlockSpec((d, d), lambda i: (0, 0), pipeline_mode=once)],
        out_specs=[row(d), row(d), row(d)],
        out_shape=[jax.ShapeDtypeStruct((t, d), f32), jax.ShapeDtypeStruct((t, d), bf16),
                   jax.ShapeDtypeStruct((t, d), f32)],
        compiler_params=_cp(("parallel",)),
        name="out_qproj",
    )(ya, yb, yc, w, w, w, h, gain.reshape(1, d), w_q)


def _top_k_columns(v, k, ids=None, want_rank=True):
    if ids is None:
        ids = lax.broadcasted_iota(jnp.int32, v.shape, 0)
    big = jnp.int32(2 ** 30)
    kidx = lax.broadcasted_iota(jnp.int32, (k, v.shape[1]), 0)
    vals = jnp.zeros((k, v.shape[1]), f32)
    firsts = []
    rank = jnp.full(v.shape, float(k), f32) if want_rank else None
    for kk in range(k):
        m = jnp.max(v, axis=0, keepdims=True)
        first = jnp.min(jnp.where(v == m, ids, big), axis=0, keepdims=True)
        hit = ids == first
        if want_rank:
            rank = jnp.where(hit, float(kk), rank)
        v = jnp.where(hit, -jnp.inf, v)
        vals = jnp.where(kidx == kk, m, vals)
        firsts.append(first)
    return vals, rank, firsts


def _pair_candidates(t1, t2):
    n = t1.shape[1]
    k = PEER_TOPK
    i16 = lax.broadcasted_iota(jnp.int32, (k, n), 0)
    i8 = lax.broadcasted_iota(jnp.int32, (8, n), 0)
    vals = [t1[0:1, :] + t2, t1[1:2, :] + t2[0:8, :]]
    pos = [i16, k + i8]
    for a in range(2, 8):
        vals.append(jnp.where(i8 < k // (a + 1), t1[a:a + 1, :] + t2[0:8, :], -jnp.inf))
        pos.append(a * k + i8)
    vals.append(t1[8:16, :] + t2[0:1, :])
    pos.append((8 + i8) * k)
    return jnp.concatenate(vals, axis=0), jnp.concatenate(pos, axis=0)


def _peer_route_kernel(q_ref, key_ref, a_ref, cnt_ref, rank2_ref, e2_ref):
    tm = q_ref.shape[0]
    for h in range(PEER_H):
        sc, tops, ranks = [], [], []
        for half in range(2):
            c0 = (h * 2 + half) * PEER_NK
            qh = q_ref[:, c0:c0 + PEER_NK].astype(bf16)
            s = lax.dot_general(key_ref[h * 2 + half], qh, (((1,), (1,)), ((), ())),
                                preferred_element_type=f32)
            vals, rank, _ = _top_k_columns(s, PEER_TOPK)
            sc.append(s)
            tops.append(vals)
            ranks.append(rank)
        cand, cand_pos = _pair_candidates(tops[0], tops[1])
        best, _, pos = _top_k_columns(cand, PEER_TOPK, ids=cand_pos, want_rank=False)
        mass = jnp.sum(jnp.exp(best - best[0:1, :]), axis=0, keepdims=True)
        aidx = lax.broadcasted_iota(jnp.int32, (PEER_TOPK, tm), 0)
        per_rank = jnp.zeros((PEER_TOPK, tm), f32)
        for p in pos:
            per_rank = per_rank + jnp.where(aidx == p // PEER_TOPK, 1.0, 0.0)
        cnt = jnp.zeros((PEER_NK, tm), f32)
        for a in range(PEER_TOPK):
            cnt = jnp.where(ranks[0] == float(a), per_rank[a:a + 1, :], cnt)
        hs = slice(h * PEER_NK, (h + 1) * PEER_NK)
        a_ref[hs, :] = (jnp.exp(sc[0] - tops[0][0:1, :]) * (0.5 / mass)).astype(bf16)
        cnt_ref[hs, :] = cnt.astype(bf16)
        rank2_ref[hs, :] = ranks[1].astype(bf16)
        e2_ref[hs, :] = jnp.exp(sc[1] - tops[1][0:1, :]).astype(bf16)


def _peer_route(qry, keys):
    t = qry.shape[0]
    tm = TM_ROUTE
    narrow = jax.ShapeDtypeStruct((PEER_H * PEER_NK, t), bf16)
    nspec = pl.BlockSpec((PEER_H * PEER_NK, tm), lambda i: (0, i))
    return pl.pallas_call(
        _peer_route_kernel,
        grid=(t // tm,),
        in_specs=[pl.BlockSpec((tm, PEER_H * 2 * PEER_NK), lambda i: (i, 0)),
                  pl.BlockSpec((PEER_H * 2, PEER_NK, PEER_NK), lambda i: (0, 0, 0))],
        out_specs=[nspec] * 4,
        out_shape=[narrow] * 4,
        compiler_params=_cp(("parallel",)),
        name="peer_route",
    )(qry, keys.astype(bf16).reshape(PEER_H * 2, PEER_NK, PEER_NK))


def _peer_expert_kernel(xn_ref, u_ref, vt_ref, a_ref, cnt_ref, rank2_ref, e2_ref, h_ref, o_ref,
                        acc_ref, act_ref, gt_ref, arow_ref, crow_ref, *, tm, te):
    j = pl.program_id(1)

    @pl.when(j == 0)
    def _():
        acc_ref[...] = jnp.zeros_like(acc_ref)

    n_first = te // PEER_NK
    i0 = pl.multiple_of(j * n_first, n_first)

    a_rows = [a_ref[pl.ds(pl.multiple_of(h * PEER_NK + i0, n_first), n_first), :] for h in range(PEER_H)]
    c_rows = [cnt_ref[pl.ds(pl.multiple_of(h * PEER_NK + i0, n_first), n_first), :] for h in range(PEER_H)]
    for h in range(PEER_H):
        arow_ref[h * n_first:(h + 1) * n_first, :] = a_rows[h]
        crow_ref[h * n_first:(h + 1) * n_first, :] = c_rows[h]

    x = lax.dot_general(u_ref[...], xn_ref[...], (((1,), (1,)), ((), ())), preferred_element_type=f32)
    act_ref[...] = (x * (1.0 + jnp.tanh(x * (GELU_C + (GELU_C * GELU_A) * (x * x))))).astype(bf16)

    jb = PEER_JB
    zero = jnp.zeros((jb, BF16_LANES), bf16)
    for c in range(tm // BF16_LANES):
        ls = slice(c * BF16_LANES, (c + 1) * BF16_LANES)
        for u in range(n_first):
            for q in range(PEER_NK // jb):
                rs = slice(u * PEER_NK + q * jb, u * PEER_NK + (q + 1) * jb)
                w = zero
                for h in range(PEER_H):
                    hs = slice(h * PEER_NK + q * jb, h * PEER_NK + (q + 1) * jb)
                    a = arow_ref[h * n_first + u:h * n_first + u + 1, ls]
                    cnt = crow_ref[h * n_first + u:h * n_first + u + 1, ls]
                    keep = rank2_ref[hs, ls] < cnt
                    w = w + jnp.where(keep, e2_ref[hs, ls], zero) * a
                gt_ref[rs, ls] = w * act_ref[rs, ls]

    acc_ref[...] += jnp.dot(vt_ref[...], gt_ref[...], preferred_element_type=f32)

    @pl.when(j == pl.num_programs(1) - 1)
    def _():
        o_ref[...] = h_ref[...] + acc_ref[...].T


def _peer_experts(xn, u, vt, a, cnt, rank2, e2, h):
    t, d = h.shape
    tm, te = TM_PEER, TE_PEER
    nspec = pl.BlockSpec((PEER_H * PEER_NK, tm), lambda i, j: (0, i))
    return pl.pallas_call(
        functools.partial(_peer_expert_kernel, tm=tm, te=te),
        grid=(t // tm, PEER_E // te),
        in_specs=[pl.BlockSpec((tm, d), lambda i, j: (i, 0)),
                  pl.BlockSpec((te, d), lambda i, j: (j, 0)),
                  pl.BlockSpec((d, te), lambda i, j: (0, j)),
                  nspec, nspec, nspec, nspec,
                  pl.BlockSpec((tm, d), lambda i, j: (i, 0))],
        out_specs=pl.BlockSpec((tm, d), lambda i, j: (i, 0)),
        out_shape=jax.ShapeDtypeStruct((t, d), f32),
        scratch_shapes=[pltpu.VMEM((d, tm), f32), pltpu.VMEM((te, tm), bf16), pltpu.VMEM((te, tm), bf16),
                        pltpu.VMEM((PEER_H * te // PEER_NK, tm), bf16),
                        pltpu.VMEM((PEER_H * te // PEER_NK, tm), bf16)],
        compiler_params=_cp(("parallel", "arbitrary")),
        name="peer_experts",
    )(xn, u, vt, a, cnt, rank2, e2, h)


def _transpose_cast_kernel(x_ref, o_ref):
    o_ref[...] = x_ref[...].T.astype(bf16)


def _transpose_cast(x):
    r, c = x.shape
    tr = TE_PEER
    return pl.pallas_call(
        _transpose_cast_kernel,
        grid=(r // tr,),
        in_specs=[pl.BlockSpec((tr, c), lambda i: (i, 0))],
        out_specs=pl.BlockSpec((c, tr), lambda i: (0, i)),
        out_shape=jax.ShapeDtypeStruct((c, r), bf16),
        compiler_params=_cp(("parallel",)),
        name="transpose_cast",
    )(x)


def _permute_w_in(w):
    pad = jnp.zeros((w.shape[0], N_IN - ORIG_END), w.dtype)
    return jnp.concatenate([w[:, :ORIG_GA], w[:, ORIG_NQ:ORIG_GATE], w[:, ORIG_GA:ORIG_NQ],
                            w[:, ORIG_GATE:ORIG_END], pad], axis=1).astype(bf16)


def _tile_flags(blk_any, seq):
    per_tile = blk_any.reshape(-1, seq // TK, TK // SEL_BLOCK).max(axis=-1)
    return (per_tile > 0.5).astype(jnp.int32).reshape(-1)


def _mixers(h, norm_mix, w_in, conv_w, gla_w_a2, gla_b_a, q_norm, k_norm, cmp_pos, cmp_w1, cmp_w2, out_norm,
            batch, seq):
    p = _norm_matmul(h, norm_mix.astype(f32), _permute_w_in(w_in), TN_IN)
    gain = out_norm.astype(f32)
    y_conv = _conv_mixer(p, conv_w, gain[:CONV_W], seq)
    y_gla = _gla_mixer(p, gla_w_a2, gla_b_a, gain[CONV_W:2 * CONV_W], batch, seq)
    kcc = _compress(p, OFF_NKC, cmp_pos[0], cmp_w1[0], cmp_w2[0], k_norm[0], batch, seq, True)
    vcc = _compress(p, OFF_NVC, cmp_pos[1], cmp_w1[1], cmp_w2[1], k_norm[0], batch, seq, False)
    ks, vs, kw, vw = _kv_prep(p, k_norm, batch, seq)
    o_c, selt, blk_any = _nsa_cmp(p, q_norm, kcc, vcc, batch, seq)
    y_nsa = _nsa_attn(p, q_norm, o_c, selt, _tile_flags(blk_any, seq), ks, vs, kw, vw, gain[2 * CONV_W:], batch, seq)
    return y_conv, y_gla, y_nsa


def _layer(h, norm_mix, w_in, conv_w, gla_w_a2, gla_b_a, q_norm, k_norm, cmp_pos, cmp_w1, cmp_w2, out_norm, w_out,
           norm_ffn, peer_w_q, peer_keys, peer_u, peer_v, batch, seq):
    y_conv, y_gla, y_nsa = _mixers(h, norm_mix, w_in, conv_w, gla_w_a2, gla_b_a, q_norm, k_norm, cmp_pos, cmp_w1,
                                   cmp_w2, out_norm, batch, seq)
    h, xn, qry = _out_qproj(y_conv, y_gla, y_nsa, w_out, h, norm_ffn.astype(f32), peer_w_q.astype(bf16))
    a, cnt, rank2, e2 = _peer_route(qry, peer_keys)
    return _peer_experts(xn, peer_u.astype(bf16), _transpose_cast(peer_v.astype(f32)), a, cnt, rank2, e2, h)


def kernel(x, norm_mix, w_in, conv_w, gla_w_a2, gla_b_a, nsa_q_norm, nsa_k_norm, nsa_cmp_pos, nsa_cmp_w1,
           nsa_cmp_w2, out_norm, w_out, norm_ffn, peer_w_q, peer_keys, peer_u, peer_v):
    batch, seq, d = x.shape
    h = x.reshape(batch * seq, d)
    for l in range(w_in.shape[0]):
        h = _layer(h, norm_mix[l], w_in[l], conv_w[l], gla_w_a2[l], gla_b_a[l], nsa_q_norm[l], nsa_k_norm[l],
                   nsa_cmp_pos[l], nsa_cmp_w1[l], nsa_cmp_w2[l], out_norm[l], w_out[l], norm_ffn[l], peer_w_q[l],
                   peer_keys[l], peer_u[l], peer_v[l], batch, seq)
    return h.reshape(batch, seq, d)
```
